```python
import math
import jax, jax.numpy as jnp
from jax import lax
import numpy as np

D_MODEL = 1024
BATCH = 8
SEQ = 4096
DEPTH = 2

S5_CHANNELS = D_MODEL // 2
S5_GROUP = 16
S5_GROUPS = S5_CHANNELS // S5_GROUP
S5_STATE = 64
RET_HEADS = 4
RET_WIDTH = D_MODEL // 2
RET_HEAD_DIM = RET_WIDTH // RET_HEADS
RET_CHUNK = 128
EVEN_IN_WIDTH = S5_CHANNELS + 4 * RET_WIDTH
EVEN_MIX_WIDTH = S5_CHANNELS + RET_WIDTH
ATT_HEAD_DIM = 64
ATT_HEADS = D_MODEL // ATT_HEAD_DIM
ATT_WIDTH = ATT_HEADS * ATT_HEAD_DIM
DILATED_BRANCHES = ((128, 1), (512, 4), (2048, 16))
ATT_BLOCK = 128
D_FF = ((8 * D_MODEL // 3 + 127) // 128) * 128
N_EXPERTS = 8
TOP_K = 2
MOE_BLOCK = 512
LN_EPS = 1e-5
N_EVEN = (DEPTH + 1) // 2
N_ODD = DEPTH // 2

kernel_name = 'hybrid_s5_retention_dilated_moe'


def _layernorm(x, g, b):
    xf = x.astype(jnp.float32)
    mu = jnp.mean(xf, -1, keepdims=True)
    var = jnp.mean(jnp.square(xf - mu), -1, keepdims=True)
    return ((xf - mu) * lax.rsqrt(var + LN_EPS) * g + b).astype(x.dtype)


def _s5_combine(e_i, e_j):
    ar_i, ai_i, br_i, bi_i = e_i
    ar_j, ai_j, br_j, bi_j = e_j
    return (ar_j * ar_i - ai_j * ai_i,
            ar_j * ai_i + ai_j * ar_i,
            ar_j * br_i - ai_j * bi_i + br_j,
            ar_j * bi_i + ai_j * br_i + bi_j)


def _s5_mixer(u, a_re, a_im, log_dt, b_re, b_im, c_re, c_im, d_skip, glu_w, glu_b):
    f32 = jnp.float32
    bsz, seq, _ = u.shape
    uf = u.astype(f32).reshape(bsz, seq, S5_GROUPS, S5_GROUP)
    ar = a_re.astype(f32)
    ai = a_im.astype(f32)
    dt = jnp.exp(log_dt.astype(f32))[:, None]
    mag = jnp.exp(ar * dt)
    lr = mag * jnp.cos(ai * dt)
    li = mag * jnp.sin(ai * dt)
    den = ar * ar + ai * ai
    zr = ((lr - 1.0) * ar + li * ai) / den
    zi = (li * ar - (lr - 1.0) * ai) / den
    br = b_re.astype(f32)
    bi = b_im.astype(f32)
    bbr = zr[..., None] * br - zi[..., None] * bi
    bbi = zr[..., None] * bi + zi[..., None] * br
    bu_r = jnp.einsum('bsgc,gpc->bsgp', uf, bbr)
    bu_i = jnp.einsum('bsgc,gpc->bsgp', uf, bbi)
    lam_r = jnp.broadcast_to(lr, (1, seq, S5_GROUPS, S5_STATE))
    lam_i = jnp.broadcast_to(li, (1, seq, S5_GROUPS, S5_STATE))
    _, _, xr, xi = lax.associative_scan(_s5_combine, (lam_r, lam_i, bu_r, bu_i), axis=1)
    y = (jnp.einsum('bsgp,gcp->bsgc', xr, c_re.astype(f32))
         - jnp.einsum('bsgp,gcp->bsgc', xi, c_im.astype(f32))
         + d_skip.astype(f32).reshape(S5_GROUPS, S5_GROUP) * uf)
    y = jax.nn.gelu(y.reshape(bsz, seq, S5_CHANNELS))
    y = y * jax.nn.sigmoid(y @ glu_w.astype(f32) + glu_b.astype(f32))
    return y.astype(u.dtype)


def _retention(q, k, v, g):
    f32 = jnp.float32
    bsz, seq, _ = q.shape
    nc = seq // RET_CHUNK
    shp = (bsz, nc, RET_CHUNK, RET_HEADS, RET_HEAD_DIM)
    q = q.astype(f32).reshape(shp)
    k = k.astype(f32).reshape(shp) * (RET_HEAD_DIM ** -0.5)
    v = v.astype(f32).reshape(shp)
    log_gamma = jnp.log1p(-jnp.exp2(-5.0 - jnp.arange(RET_HEADS, dtype=f32)))
    pos = jnp.arange(RET_CHUNK, dtype=f32)
    diff = pos[:, None] - pos[None, :]
    decay = jnp.where(diff >= 0, jnp.exp(log_gamma[:, None, None] * jnp.maximum(diff, 0.0)), 0.0)
    scores = jnp.einsum('bcnhd,bcmhd->bchnm', q, k) * decay
    inner = jnp.einsum('bchnm,bcmhe->bcnhe', scores, v)
    zeta = jnp.exp(log_gamma[:, None] * (RET_CHUNK - 1.0 - pos))
    kv = jnp.einsum('bcmhd,hm,bcmhe->bchde', k, zeta, v)
    chunk_decay = jnp.exp(log_gamma * RET_CHUNK)[:, None, None]

    def step(state, kv_c):
        return state * chunk_decay + kv_c, state

    _, state_prev = lax.scan(step, jnp.zeros_like(kv[:, 0]), jnp.moveaxis(kv, 1, 0))
    xi = jnp.exp(log_gamma[:, None] * (pos + 1.0))
    cross = jnp.einsum('bcnhd,cbhde,hn->bcnhe', q, state_prev, xi)
    ret = (inner + cross).reshape(bsz, seq, RET_HEADS, RET_HEAD_DIM)
    mu = jnp.mean(ret, -1, keepdims=True)
    var = jnp.mean(jnp.square(ret - mu), -1, keepdims=True)
    ret = ((ret - mu) * lax.rsqrt(var + LN_EPS)).reshape(bsz, seq, RET_WIDTH)
    return (jax.nn.silu(g.astype(f32)) * ret).astype(g.dtype)


def _dilated_branch(q, k, v, window, dilation, slopes):
    f32 = jnp.float32
    bsz, seq, nh, hd = q.shape
    sub_len = seq // dilation
    span = window // dilation
    nb = -(-sub_len // ATT_BLOCK)
    padded = nb * ATT_BLOCK

    def to_sub(t, front):
        t = jnp.moveaxis(t.reshape(bsz, sub_len, dilation, nh, hd), 2, 1)
        return jnp.pad(t, ((0, 0), (0, 0), (front, padded - sub_len), (0, 0), (0, 0)))

    qb = to_sub(q, 0).reshape(bsz, dilation, nb, ATT_BLOCK, nh, hd)
    kb = to_sub(k, ATT_BLOCK).reshape(bsz, dilation, nb + 1, ATT_BLOCK, nh, hd)
    vb = to_sub(v, ATT_BLOCK).reshape(bsz, dilation, nb + 1, ATT_BLOCK, nh, hd)
    kw = jnp.concatenate([kb[:, :, :-1], kb[:, :, 1:]], axis=3)
    vw = jnp.concatenate([vb[:, :, :-1], vb[:, :, 1:]], axis=3)
    s = jnp.einsum('bdnqhe,bdnkhe->bdnhqk', qb, kw).astype(f32) * (hd ** -0.5)
    qi = jnp.arange(ATT_BLOCK)
    ki = jnp.arange(2 * ATT_BLOCK)
    steps = ATT_BLOCK + qi[:, None] - ki[None, :]
    key_idx = (jnp.arange(nb)[:, None] - 1) * ATT_BLOCK + ki[None, :]
    allowed = (steps >= 0)[None] & (steps <= span)[None] & (key_idx >= 0)[:, None, :]
    bias = -slopes[:, None, None] * (steps * dilation).astype(f32)[None]
    s = jnp.where(allowed[None, None, :, None], s + bias, -jnp.inf)
    m = jnp.max(s, -1, keepdims=True)
    p = jnp.exp(s - m)
    l = jnp.sum(p, -1)
    o = jnp.einsum('bdnhqk,bdnkhe->bdnqhe', p, vw.astype(f32)) / jnp.moveaxis(l, 3, 4)[..., None]
    lse = jnp.moveaxis(m[..., 0] + jnp.log(l), 3, 4)
    o = o.reshape(bsz, dilation, padded, nh, hd)[:, :, :sub_len]
    o = jnp.moveaxis(o, 1, 2).reshape(bsz, seq, nh, hd)
    lse = lse.reshape(bsz, dilation, padded, nh)[:, :, :sub_len]
    lse = jnp.moveaxis(lse, 1, 2).reshape(bsz, seq, nh)
    return o, lse


def _dilated_attention(q, k, v):
    slopes = jnp.exp2(-8.0 * jnp.arange(1, ATT_HEADS + 1, dtype=jnp.float32) / ATT_HEADS)
    outs, lses = [], []
    for window, dilation in DILATED_BRANCHES:
        o, lse = _dilated_branch(q, k, v, window, dilation, slopes)
        outs.append(o)
        lses.append(lse)
    w = jax.nn.softmax(jnp.stack(lses), axis=0)
    o = jnp.sum(w[..., None] * jnp.stack(outs), axis=0)
    return o.reshape(q.shape[0], q.shape[1], ATT_WIDTH).astype(q.dtype)


def _swiglu(x, w_gate, w_up, w_down):
    return (jax.nn.silu(x @ w_gate) * (x @ w_up)) @ w_down


def _moe_swiglu(x2, router_w, router_b, w_gate, w_up, w_down):
    n_tok, dim = x2.shape
    logits = (x2 @ router_w).astype(jnp.float32) + router_b.astype(jnp.float32)
    top_val, top_idx = lax.top_k(logits, TOP_K)
    gates = jax.nn.softmax(top_val, axis=-1)
    n_assign = n_tok * TOP_K
    exp_flat = top_idx.reshape(n_assign)
    tok_flat = jnp.broadcast_to(jnp.arange(n_tok, dtype=jnp.int32)[:, None], (n_tok, TOP_K)).reshape(n_assign)
    gate_flat = gates.reshape(n_assign)
    counts = jnp.bincount(exp_flat, length=N_EXPERTS)
    starts = jnp.cumsum(counts) - counts
    padded_counts = (counts + MOE_BLOCK - 1) // MOE_BLOCK * MOE_BLOCK
    pends = jnp.cumsum(padded_counts)
    pstarts = pends - padded_counts
    order = jnp.argsort(exp_flat)
    s_exp = exp_flat[order]
    dest = pstarts[s_exp] + jnp.arange(n_assign) - starts[s_exp]
    n_rows = (-(-n_assign // MOE_BLOCK) + N_EXPERTS) * MOE_BLOCK
    row_tok = jnp.zeros((n_rows,), jnp.int32).at[dest].set(tok_flat[order])
    row_gate = jnp.zeros((n_rows,), jnp.float32).at[dest].set(gate_flat[order])
    n_blocks = n_rows // MOE_BLOCK
    blk_exp = jnp.minimum(jnp.searchsorted(pends, jnp.arange(n_blocks) * MOE_BLOCK, side='right'), N_EXPERTS - 1)
    xb = x2[row_tok].reshape(n_blocks, MOE_BLOCK, dim)

    def expert_block(args):
        xblk, e = args
        return _swiglu(xblk, w_gate[e], w_up[e], w_down[e])

    yb = lax.map(expert_block, (xb, blk_exp)).reshape(n_rows, dim)
    return jnp.zeros_like(x2).at[row_tok].add(yb * row_gate[:, None].astype(yb.dtype))


def setup_inputs(seed: int = 0) -> dict:
    key = jax.random.key(seed)
    ks = iter(jax.random.split(key, 40))
    f32 = jnp.float32

    def nrm(shape, scale):
        return jax.random.normal(next(ks), shape, f32) * scale

    beta = (8.0 * DEPTH) ** -0.25
    ne, no = N_EVEN, N_ODD
    G, P, C = S5_GROUPS, S5_STATE, S5_GROUP
    inp = {}
    inp['x'] = nrm((BATCH, SEQ, D_MODEL), 1.0)
    inp['ev_w_in'] = nrm((ne, D_MODEL, EVEN_IN_WIDTH), D_MODEL ** -0.5)
    inp['ev_s5_a_re'] = -0.5 + nrm((ne, G, P), 0.01)
    inp['ev_s5_a_im'] = math.pi * jnp.arange(P, dtype=f32) + nrm((ne, G, P), 0.01)
    inp['ev_s5_log_dt'] = jax.random.uniform(next(ks), (ne, G), f32, math.log(1e-3), math.log(1e-1))
    inp['ev_s5_b_re'] = nrm((ne, G, P, C), (2.0 * C) ** -0.5)
    inp['ev_s5_b_im'] = nrm((ne, G, P, C), (2.0 * C) ** -0.5)
    inp['ev_s5_c_re'] = nrm((ne, G, C, P), (2.0 * P) ** -0.5)
    inp['ev_s5_c_im'] = nrm((ne, G, C, P), (2.0 * P) ** -0.5)
    inp['ev_s5_d'] = nrm((ne, S5_CHANNELS), 1.0)
    inp['ev_s5_glu_w'] = nrm((ne, S5_CHANNELS, S5_CHANNELS), S5_CHANNELS ** -0.5)
    inp['ev_s5_glu_b'] = nrm((ne, S5_CHANNELS), 0.01)
    inp['ev_w_out'] = nrm((ne, EVEN_MIX_WIDTH, D_MODEL), EVEN_MIX_WIDTH ** -0.5 * beta)
    inp['ev_ln1_g'] = 1.0 + nrm((ne, D_MODEL), 0.02)
    inp['ev_ln1_b'] = nrm((ne, D_MODEL), 0.02)
    inp['ev_ffn_w_gate'] = nrm((ne, D_MODEL, D_FF), D_MODEL ** -0.5)
    inp['ev_ffn_w_up'] = nrm((ne, D_MODEL, D_FF), D_MODEL ** -0.5)
    inp['ev_ffn_w_down'] = nrm((ne, D_FF, D_MODEL), D_FF ** -0.5 * beta)
    inp['ev_ln2_g'] = 1.0 + nrm((ne, D_MODEL), 0.02)
    inp['ev_ln2_b'] = nrm((ne, D_MODEL), 0.02)
    inp['od_w_qkv'] = nrm((no, D_MODEL, 3 * ATT_WIDTH), D_MODEL ** -0.5)
    inp['od_w_out'] = nrm((no, ATT_WIDTH, D_MODEL), ATT_WIDTH ** -0.5 * beta)
    inp['od_ln1_g'] = 1.0 + nrm((no, D_MODEL), 0.02)
    inp['od_ln1_b'] = nrm((no, D_MODEL), 0.02)
    inp['od_router_w'] = nrm((no, D_MODEL, N_EXPERTS), D_MODEL ** -0.5)
    inp['od_router_b'] = nrm((no, N_EXPERTS), 0.01)
    inp['od_moe_w_gate'] = nrm((no, N_EXPERTS, D_MODEL, D_FF), D_MODEL ** -0.5)
    inp['od_moe_w_up'] = nrm((no, N_EXPERTS, D_MODEL, D_FF), D_MODEL ** -0.5)
    inp['od_moe_w_down'] = nrm((no, N_EXPERTS, D_FF, D_MODEL), D_FF ** -0.5 * beta)
    inp['od_ln2_g'] = 1.0 + nrm((no, D_MODEL), 0.02)
    inp['od_ln2_b'] = nrm((no, D_MODEL), 0.02)
    return inp


def reference(x, ev_w_in, ev_s5_a_re, ev_s5_a_im, ev_s5_log_dt, ev_s5_b_re, ev_s5_b_im,
              ev_s5_c_re, ev_s5_c_im, ev_s5_d, ev_s5_glu_w, ev_s5_glu_b, ev_w_out,
              ev_ln1_g, ev_ln1_b, ev_ffn_w_gate, ev_ffn_w_up, ev_ffn_w_down, ev_ln2_g, ev_ln2_b,
              od_w_qkv, od_w_out, od_ln1_g, od_ln1_b, od_router_w, od_router_b,
              od_moe_w_gate, od_moe_w_up, od_moe_w_down, od_ln2_g, od_ln2_b):
    alpha = (2.0 * DEPTH) ** 0.25
    bsz, seq, dim = x.shape
    for layer in range(DEPTH):
        i = layer // 2
        if layer % 2 == 0:
            h = x @ ev_w_in[i]
            u, q, k, v, g = jnp.split(h, [S5_CHANNELS, S5_CHANNELS + RET_WIDTH,
                                          S5_CHANNELS + 2 * RET_WIDTH, S5_CHANNELS + 3 * RET_WIDTH], axis=-1)
            y_a = _s5_mixer(u, ev_s5_a_re[i], ev_s5_a_im[i], ev_s5_log_dt[i], ev_s5_b_re[i], ev_s5_b_im[i],
                            ev_s5_c_re[i], ev_s5_c_im[i], ev_s5_d[i], ev_s5_glu_w[i], ev_s5_glu_b[i])
            y_b = _retention(q, k, v, g)
            mix = jnp.concatenate([y_a, y_b], axis=-1) @ ev_w_out[i]
            x = _layernorm(alpha * x + mix, ev_ln1_g[i], ev_ln1_b[i])
            ffn = _swiglu(x, ev_ffn_w_gate[i], ev_ffn_w_up[i], ev_ffn_w_down[i])
            x = _layernorm(alpha * x + ffn, ev_ln2_g[i], ev_ln2_b[i])
        else:
            qkv = x @ od_w_qkv[i]
            q, k, v = jnp.split(qkv, 3, axis=-1)
            hs = (bsz, seq, ATT_HEADS, ATT_HEAD_DIM)
            att = _dilated_attention(q.reshape(hs), k.reshape(hs), v.reshape(hs)) @ od_w_out[i]
            x = _layernorm(alpha * x + att, od_ln1_g[i], od_ln1_b[i])
            moe = _moe_swiglu(x.reshape(bsz * seq, dim), od_router_w[i], od_router_b[i],
                              od_moe_w_gate[i], od_moe_w_up[i], od_moe_w_down[i]).reshape(bsz, seq, dim)
            x = _layernorm(alpha * x + moe, od_ln2_g[i], od_ln2_b[i])
    return x
```

```python
import functools
import math

import numpy as np
import jax
import jax.numpy as jnp
from jax import lax
from jax.experimental import pallas as pl
from jax.experimental.pallas import tpu as pltpu

F32 = jnp.float32
BF16 = jnp.bfloat16

LN_EPS = 1e-5
DEPTH = 2
ALPHA = (2.0 * DEPTH) ** 0.25

S5_CHANNELS = 512
S5_GROUPS = 32
S5_GROUP = 16
S5_STATE = 64
S5_NSTATE = S5_GROUPS * S5_STATE

RET_HEADS = 4
RET_HEAD_DIM = 128
RET_CHUNK = 128

ATT_HEADS = 16
ATT_HEAD_DIM = 64
ATT_PAIRS = ATT_HEADS // 2
ATT_BLOCK = 128
ATT_SPAN = 128
ATT_DILATIONS = (1, 4, 16)
NEG_BIG = -1e30

D_FF = 2816
N_EXPERTS = 8
TOP_K = 2
MOE_BLOCK = 512

VMEM_LIMIT = 56 * 1024 * 1024


def _cparams(sem):
    return pltpu.CompilerParams(dimension_semantics=sem, vmem_limit_bytes=VMEM_LIMIT)


def _layernorm(y, g, b):
    mu = jnp.mean(y, -1, keepdims=True)
    yc = y - mu
    var = jnp.mean(yc * yc, -1, keepdims=True)
    return yc * lax.rsqrt(var + LN_EPS) * g + b


def _const_spec(shape):
    nd = len(shape)
    return pl.BlockSpec(shape, lambda *_: (0,) * nd)


def _inproj_body(x_ref, w_ref, u_ref, h_ref):
    xb = x_ref[...].astype(BF16)
    r = jnp.dot(xb, w_ref[...], preferred_element_type=F32)
    u_ref[...] = r[:, :S5_CHANNELS].astype(BF16)
    h_ref[...] = r[:, S5_CHANNELS:].astype(BF16)


def _inproj(x, w):
    bsz, seq, dim = x.shape
    tm = 512
    nh = w.shape[1] - S5_CHANNELS
    return pl.pallas_call(
        _inproj_body,
        grid=(bsz, seq // tm),
        in_specs=[pl.BlockSpec((None, tm, dim), lambda b, s: (b, s, 0)),
                  _const_spec(w.shape)],
        out_specs=[pl.BlockSpec((tm, S5_CHANNELS), lambda b, s: (s, b)),
                   pl.BlockSpec((None, tm, nh), lambda b, s: (b, s, 0))],
        out_shape=[jax.ShapeDtypeStruct((seq, bsz * S5_CHANNELS), BF16),
                   jax.ShapeDtypeStruct((bsz, seq, nh), BF16)],
        compiler_params=_cparams(("parallel", "parallel")),
        name="inproj",
    )(x, w)


def _s5_discretize(a_re, a_im, log_dt, b_re, b_im, c_re, c_im, bsz):
    ar = a_re.astype(F32)
    ai = a_im.astype(F32)
    dt = jnp.exp(log_dt.astype(F32))[:, None]
    mag = jnp.exp(ar * dt)
    lr = mag * jnp.cos(ai * dt)
    li = mag * jnp.sin(ai * dt)
    den = ar * ar + ai * ai
    zr = ((lr - 1.0) * ar + li * ai) / den
    zi = (li * ar - (lr - 1.0) * ai) / den
    br = b_re.astype(F32)
    bi = b_im.astype(F32)
    bbr = zr[..., None] * br - zi[..., None] * bi
    bbi = zr[..., None] * bi + zi[..., None] * br
    eye = jnp.eye(S5_GROUPS, dtype=F32)
    nch = S5_CHANNELS
    wbr = jnp.einsum('gpc,gh->gchp', bbr, eye).reshape(nch, S5_NSTATE)
    wbi = jnp.einsum('gpc,gh->gchp', bbi, eye).reshape(nch, S5_NSTATE)
    wb = jnp.concatenate([wbr, wbi], axis=1).astype(BF16)
    wcr = jnp.einsum('gcp,gh->hpgc', c_re.astype(F32), eye).reshape(S5_NSTATE, nch)
    wci = jnp.einsum('gcp,gh->hpgc', c_im.astype(F32), eye).reshape(S5_NSTATE, nch)
    wc = jnp.concatenate([wcr, -wci], axis=0).astype(BF16)
    lam = jnp.stack([jnp.broadcast_to(lr.reshape(1, S5_NSTATE), (bsz, S5_NSTATE)),
                     jnp.broadcast_to(li.reshape(1, S5_NSTATE), (bsz, S5_NSTATE))])
    return wb, wc, lam


def _gelu_tanh(x):
    return 0.5 * x * (1.0 + jnp.tanh(math.sqrt(2.0 / math.pi) * (x + 0.044715 * (x * x * x))))


def _s5_body(u_ref, wb_ref, wc_ref, lam_ref, d_ref, gw_ref, gb_ref, o_ref, xs_ref, st_ref,
             *, lc, bsz, ncb):
    @pl.when(pl.program_id(0) == 0)
    def _():
        st_ref[...] = jnp.zeros_like(st_ref)

    u = u_ref[...]
    xs_ref[...] = jnp.dot(u, wb_ref[...], preferred_element_type=F32)
    width = S5_NSTATE // ncb
    for cb in range(ncb):
        re = slice(cb * width, (cb + 1) * width)
        im = slice(S5_NSTATE + cb * width, S5_NSTATE + (cb + 1) * width)
        lr = lam_ref[0, :, re]
        li = lam_ref[1, :, re]

        def step(t, carry, re=re, im=im, lr=lr, li=li):
            xr, xi = carry
            row = pl.multiple_of(t * bsz, bsz)
            nxr = lr * xr - li * xi + xs_ref[pl.ds(row, bsz), re]
            nxi = lr * xi + li * xr + xs_ref[pl.ds(row, bsz), im]
            xs_ref[pl.ds(row, bsz), re] = nxr
            xs_ref[pl.ds(row, bsz), im] = nxi
            return nxr, nxi

        xr, xi = lax.fori_loop(0, lc, step, (st_ref[:, re], st_ref[:, im]), unroll=8)
        st_ref[:, re] = xr
        st_ref[:, im] = xi

    y = jnp.dot(xs_ref[...].astype(BF16), wc_ref[...], preferred_element_type=F32)
    y = y + d_ref[...] * u.astype(F32)
    y = _gelu_tanh(y)
    z = jnp.dot(y.astype(BF16), gw_ref[...], preferred_element_type=F32) + gb_ref[...]
    o_ref[...] = (y * jax.nn.sigmoid(z)).astype(BF16)


def _s5(u_tm, wb, wc, lam, d_skip, glu_w, glu_b, bsz):
    rows = u_tm.shape[0]
    seq = rows // bsz
    lc = 64
    blk = lc * bsz
    body = functools.partial(_s5_body, lc=lc, bsz=bsz, ncb=4)
    return pl.pallas_call(
        body,
        grid=(seq // lc,),
        in_specs=[pl.BlockSpec((blk, S5_CHANNELS), lambda s: (s, 0)),
                  _const_spec(wb.shape), _const_spec(wc.shape), _const_spec(lam.shape),
                  _const_spec(d_skip.shape), _const_spec(glu_w.shape), _const_spec(glu_b.shape)],
        out_specs=pl.BlockSpec((blk, S5_CHANNELS), lambda s: (s, 0)),
        out_shape=jax.ShapeDtypeStruct((rows, S5_CHANNELS), BF16),
        scratch_shapes=[pltpu.VMEM((blk, 2 * S5_NSTATE), F32),
                        pltpu.VMEM((bsz, 2 * S5_NSTATE), F32)],
        compiler_params=_cparams(("arbitrary",)),
        name="s5_mixer",
    )(u_tm, wb, wc, lam, d_skip, glu_w, glu_b)


def _ret_constants():
    hh = np.arange(RET_HEADS, dtype=np.float64)
    log_gamma = np.log1p(-np.exp2(-5.0 - hh))
    pos = np.arange(RET_CHUNK, dtype=np.float64)
    diff = pos[:, None] - pos[None, :]
    scale = RET_HEAD_DIM ** -0.5
    decay = np.where(diff >= 0, np.exp(log_gamma[:, None, None] * np.maximum(diff, 0.0)), 0.0) * scale
    zeta = np.exp(log_gamma[:, None] * (RET_CHUNK - 1.0 - pos)) * scale
    xi = np.exp(log_gamma[:, None] * (pos + 1.0))
    chunk_decay = np.exp(log_gamma * RET_CHUNK)
    zeta_b = np.broadcast_to(zeta[:, :, None], (RET_HEADS, RET_CHUNK, RET_HEAD_DIM))
    xi_b = np.broadcast_to(xi[:, :, None], (RET_HEADS, RET_CHUNK, RET_HEAD_DIM))
    return (decay.astype(np.float32), np.ascontiguousarray(zeta_b).astype(np.float32),
            np.ascontiguousarray(xi_b).astype(np.float32), [float(c) for c in chunk_decay])


def _ret_body(q_ref, k_ref, v_ref, g_ref, dec_ref, zeta_ref, xi_ref, o_ref, st_ref, *, ts, chunk_decay):
    @pl.when(pl.program_id(1) == 0)
    def _():
        st_ref[...] = jnp.zeros_like(st_ref)

    for c in range(ts // RET_CHUNK):
        rows = slice(c * RET_CHUNK, (c + 1) * RET_CHUNK)
        for h in range(RET_HEADS):
            cols = slice(h * RET_HEAD_DIM, (h + 1) * RET_HEAD_DIM)
            q = q_ref[rows, cols]
            k = k_ref[rows, cols]
            v = v_ref[rows, cols]
            s = lax.dot_general(q, k, (((1,), (1,)), ((), ())), preferred_element_type=F32) * dec_ref[h]
            inner = jnp.dot(s.astype(BF16), v, preferred_element_type=F32)
            st = st_ref[h]
            cross = jnp.dot(q, st.astype(BF16), preferred_element_type=F32) * xi_ref[h]
            ret = inner + cross
            mu = jnp.mean(ret, -1, keepdims=True)
            rc = ret - mu
            var = jnp.mean(rc * rc, -1, keepdims=True)
            rn = rc * lax.rsqrt(var + LN_EPS)
            gg = g_ref[rows, cols].astype(F32)
            o_ref[rows, cols] = (gg * jax.nn.sigmoid(gg) * rn).astype(BF16)
            kz = (k.astype(F32) * zeta_ref[h]).astype(BF16)
            kv = lax.dot_general(kz, v, (((0,), (0,)), ((), ())), preferred_element_type=F32)
            st_ref[h] = st * chunk_decay[h] + kv


def _retention(h):
    bsz, seq, _ = h.shape
    ts = 512
    width = RET_HEADS * RET_HEAD_DIM
    decay, zeta_b, xi_b, chunk_decay = _ret_constants()
    body = functools.partial(_ret_body, ts=ts, chunk_decay=chunk_decay)

    def col(j):
        return pl.BlockSpec((None, ts, width), lambda b, s, j=j: (b, s, j))

    return pl.pallas_call(
        body,
        grid=(bsz, seq // ts),
        in_specs=[col(0), col(1), col(2), col(3),
                  _const_spec(decay.shape), _const_spec(zeta_b.shape), _const_spec(xi_b.shape)],
        out_specs=pl.BlockSpec((None, ts, width), lambda b, s: (b, s, 0)),
        out_shape=jax.ShapeDtypeStruct((bsz, seq, width), BF16),
        scratch_shapes=[pltpu.VMEM((RET_HEADS, RET_HEAD_DIM, RET_HEAD_DIM), F32)],
        compiler_params=_cparams(("parallel", "arbitrary")),
        name="retention",
    )(h, h, h, h, jnp.asarray(decay), jnp.asarray(zeta_b), jnp.asarray(xi_b))


def _outln_even_body(x_ref, ya_ref, yb_ref, w1_ref, w2_ref, g_ref, b_ref, o_ref):
    acc = jnp.dot(ya_ref[...], w1_ref[...], preferred_element_type=F32)
    acc = acc + jnp.dot(yb_ref[...], w2_ref[...], preferred_element_type=F32)
    y = ALPHA * x_ref[...] + acc
    o_ref[...] = _layernorm(y, g_ref[...], b_ref[...])


def _outln_even(x, ya_tm, yb, w1, w2, g, b):
    bsz, seq, dim = x.shape
    tm = 512
    wa = ya_tm.shape[1] // bsz
    wbw = yb.shape[2]
    return pl.pallas_call(
        _outln_even_body,
        grid=(bsz, seq // tm),
        in_specs=[pl.BlockSpec((None, tm, dim), lambda b_, s: (b_, s, 0)),
                  pl.BlockSpec((tm, wa), lambda b_, s: (s, b_)),
                  pl.BlockSpec((None, tm, wbw), lambda b_, s: (b_, s, 0)),
                  _const_spec(w1.shape), _const_spec(w2.shape),
                  _const_spec(g.shape), _const_spec(b.shape)],
        out_specs=pl.BlockSpec((None, tm, dim), lambda b_, s: (b_, s, 0)),
        out_shape=jax.ShapeDtypeStruct((bsz, seq, dim), F32),
        compiler_params=_cparams(("parallel", "parallel")),
        name="outproj_ln_even",
    )(x, ya_tm, yb, w1, w2, g, b)


FFN_CHUNK = 1408


def _ffn_body(x_ref, wg_ref, wu_ref, wd_ref, g_ref, b_ref, o_ref):
    x = x_ref[...]
    xb = x.astype(BF16)
    acc = None
    for c in range(D_FF // FFN_CHUNK):
        cs = slice(c * FFN_CHUNK, (c + 1) * FFN_CHUNK)
        gt = jnp.dot(xb, wg_ref[:, cs], preferred_element_type=F32)
        up = jnp.dot(xb, wu_ref[:, cs], preferred_element_type=F32)
        hh = (gt * jax.nn.sigmoid(gt) * up).astype(BF16)
        part = jnp.dot(hh, wd_ref[cs, :], preferred_element_type=F32)
        acc = part if acc is None else acc + part
    y = ALPHA * x + acc
    o_ref[...] = _layernorm(y, g_ref[...], b_ref[...])


def _ffn(x2d, wg, wu, wd, g, b):
    n_tok, dim = x2d.shape
    tm = 512
    single = pl.Buffered(1)

    def wspec(shape):
        return pl.BlockSpec(shape, lambda i: (0, 0), pipeline_mode=single)

    return pl.pallas_call(
        _ffn_body,
        grid=(n_tok // tm,),
        in_specs=[pl.BlockSpec((tm, dim), lambda i: (i, 0)),
                  wspec(wg.shape), wspec(wu.shape), wspec(wd.shape),
                  _const_spec(g.shape), _const_spec(b.shape)],
        out_specs=pl.BlockSpec((tm, dim), lambda i: (i, 0)),
        out_shape=jax.ShapeDtypeStruct((n_tok, dim), F32),
        compiler_params=_cparams(("parallel",)),
        name="ffn_swiglu_ln",
    )(x2d, wg, wu, wd, g, b)


def _qkv_body(x_ref, w_ref, o_ref):
    xb = x_ref[...].astype(BF16)
    r = jnp.dot(xb, w_ref[...], preferred_element_type=F32)
    nblk = o_ref.shape[0]
    for j in range(nblk):
        blk = r[:, j * 128:(j + 1) * 128]
        if j < ATT_PAIRS:
            blk = blk * (ATT_HEAD_DIM ** -0.5)
        o_ref[j] = blk.astype(BF16)


def _qkv(x, w):
    bsz, seq, dim = x.shape
    tm = 512
    nblk = w.shape[1] // 128
    return pl.pallas_call(
        _qkv_body,
        grid=(bsz, seq // tm),
        in_specs=[pl.BlockSpec((None, tm, dim), lambda b, s: (b, s, 0)),
                  _const_spec(w.shape)],
        out_specs=pl.BlockSpec((None, nblk, tm, 128), lambda b, s: (b, 0, s, 0)),
        out_shape=jax.ShapeDtypeStruct((bsz, nblk, seq, 128), BF16),
        compiler_params=_cparams(("parallel", "parallel")),
        name="qkv_proj",
    )(x, w)


def _att_constants():
    rows = np.arange(2 * ATT_BLOCK) % ATT_BLOCK
    keys = np.arange(2 * ATT_BLOCK)
    d0 = (rows[:, None] - keys[None, :]).astype(np.float32)
    slopes = np.exp2(-8.0 * np.arange(1, ATT_HEADS + 1, dtype=np.float64) / ATT_HEADS)
    sl = np.zeros((ATT_PAIRS, 2 * ATT_BLOCK, 2 * ATT_BLOCK), np.float32)
    for hp in range(ATT_PAIRS):
        sl[hp, :ATT_BLOCK, :] = slopes[2 * hp]
        sl[hp, ATT_BLOCK:, :] = slopes[2 * hp + 1]
    return d0, sl


def _att_body(q1, k1, v1, q4, k4, v4, q16, k16, v16, d0_ref, sl_ref, o_ref,
              acc_ref, m0_ref, m1_ref, l0_ref, l1_ref, *, seq):
    views = {1: (q1, k1, v1), 4: (q4, k4, v4), 16: (q16, k16, v16)}
    lane = lax.broadcasted_iota(jnp.int32, (ATT_BLOCK, 128), 1)
    low = lane < ATT_HEAD_DIM
    d0 = d0_ref[...]
    ones = jnp.ones((2 * ATT_BLOCK, 128), BF16)

    for bi, d in enumerate(ATT_DILATIONS):
        first = bi == 0
        last = bi == len(ATT_DILATIONS) - 1
        q_ref, k_ref, v_ref = views[d]
        sld = sl_ref[...] * float(d)
        nblk = seq // (d * ATT_BLOCK)
        for r in range(d):
            lanes = slice(r * 128, (r + 1) * 128)

            def unit(i, carry, q_ref=q_ref, k_ref=k_ref, v_ref=v_ref, lanes=lanes, d=d, r=r,
                     sld=sld, first=first, last=last):
                qrow = pl.multiple_of(i * ATT_BLOCK, ATT_BLOCK)
                wstart = pl.multiple_of(jnp.maximum(i - 1, 0) * ATT_BLOCK, ATT_BLOCK)
                off = (qrow - wstart).astype(F32)
                q = q_ref[pl.ds(qrow, ATT_BLOCK), lanes]
                kw = k_ref[pl.ds(wstart, 2 * ATT_BLOCK), lanes]
                vw = v_ref[pl.ds(wstart, 2 * ATT_BLOCK), lanes]
                zero = jnp.zeros_like(q)
                qs = jnp.concatenate([jnp.where(low, q, zero), jnp.where(low, zero, q)], axis=0)
                s = lax.dot_general(qs, kw, (((1,), (1,)), ((), ())), preferred_element_type=F32)
                steps = d0 + off
                s = s - sld * steps
                allowed = (steps >= 0.0) & (steps <= float(ATT_SPAN))
                s = jnp.where(allowed, s, NEG_BIG)
                mrow = jnp.max(s, axis=-1, keepdims=True)
                if d == 1:
                    srow = pl.ds(qrow, ATT_BLOCK)
                else:
                    srow = pl.ds(r + d * qrow, ATT_BLOCK, stride=d)
                if first:
                    m_new = mrow
                else:
                    m_old = jnp.concatenate([m0_ref[srow, :][:, :1], m1_ref[srow, :][:, :1]], axis=0)
                    m_new = jnp.maximum(m_old, mrow)
                p = jnp.exp(s - m_new)
                vaug = jnp.concatenate([vw, ones], axis=1)
                res = jnp.dot(p.astype(BF16), vaug, preferred_element_type=F32)
                o_pair = jnp.where(low, res[:ATT_BLOCK, :128], res[ATT_BLOCK:, :128])
                l0 = res[:ATT_BLOCK, 128:]
                l1 = res[ATT_BLOCK:, 128:]
                if first:
                    acc = o_pair
                else:
                    a = jnp.exp(m_old - m_new)
                    a0 = a[:ATT_BLOCK]
                    a1 = a[ATT_BLOCK:]
                    acc = jnp.where(low, a0, a1) * acc_ref[srow, :] + o_pair
                    l0 = a0 * l0_ref[srow, :] + l0
                    l1 = a1 * l1_ref[srow, :] + l1
                if last:
                    acc_ref[srow, :] = acc / jnp.where(low, l0, l1)
                else:
                    acc_ref[srow, :] = acc
                    l0_ref[srow, :] = l0
                    l1_ref[srow, :] = l1
                    m0_ref[srow, :] = jnp.broadcast_to(m_new[:ATT_BLOCK], (ATT_BLOCK, 128))
                    m1_ref[srow, :] = jnp.broadcast_to(m_new[ATT_BLOCK:], (ATT_BLOCK, 128))
                return carry

            lax.fori_loop(0, nblk, unit, 0)

    o_ref[...] = acc_ref[...].astype(BF16)


def _dilated_attention(qkv):
    bsz, nblk, seq, _ = qkv.shape
    d0, sl = _att_constants()
    args, specs = [], []
    for d in ATT_DILATIONS:
        view = qkv.reshape(bsz, nblk, seq // d, d * 128)
        for j in range(3):
            args.append(view)
            specs.append(pl.BlockSpec((None, None, seq // d, d * 128),
                                      lambda b, hp, j=j: (b, j * ATT_PAIRS + hp, 0, 0)))
    args += [jnp.asarray(d0), jnp.asarray(sl)]
    specs += [_const_spec(d0.shape),
              pl.BlockSpec((None,) + sl.shape[1:], lambda b, hp: (hp, 0, 0))]
    body = functools.partial(_att_body, seq=seq)
    return pl.pallas_call(
        body,
        grid=(bsz, ATT_PAIRS),
        in_specs=specs,
        out_specs=pl.BlockSpec((None, None, seq, 128), lambda b, hp: (b, hp, 0, 0)),
        out_shape=jax.ShapeDtypeStruct((bsz, ATT_PAIRS, seq, 128), BF16),
        scratch_shapes=[pltpu.VMEM((seq, 128), F32) for _ in range(5)],
        compiler_params=_cparams(("parallel", "parallel")),
        name="dilated_attention",
    )(*args)


def _outln_odd_body(x_ref, a_ref, w_ref, g_ref, b_ref, rw_ref, rb_ref, o_ref, lg_ref):
    a = jnp.concatenate([a_ref[j] for j in range(a_ref.shape[0])], axis=-1)
    acc = jnp.dot(a, w_ref[...], preferred_element_type=F32)
    y = ALPHA * x_ref[...] + acc
    o = _layernorm(y, g_ref[...], b_ref[...])
    o_ref[...] = o
    lg_ref[...] = jnp.dot(o, rw_ref[...], preferred_element_type=F32,
                          precision=lax.Precision.HIGHEST) + rb_ref[...]


def _outln_odd(x, att, w, g, b, rw, rb):
    bsz, seq, dim = x.shape
    tm = 512
    npair = att.shape[1]
    return pl.pallas_call(
        _outln_odd_body,
        grid=(bsz, seq // tm),
        in_specs=[pl.BlockSpec((None, tm, dim), lambda b_, s: (b_, s, 0)),
                  pl.BlockSpec((None, npair, tm, 128), lambda b_, s: (b_, 0, s, 0)),
                  _const_spec(w.shape), _const_spec(g.shape), _const_spec(b.shape),
                  _const_spec(rw.shape), _const_spec(rb.shape)],
        out_specs=[pl.BlockSpec((None, tm, dim), lambda b_, s: (b_, s, 0)),
                   pl.BlockSpec((None, tm, 128), lambda b_, s: (b_, s, 0))],
        out_shape=[jax.ShapeDtypeStruct((bsz, seq, dim), F32),
                   jax.ShapeDtypeStruct((bsz, seq, 128), F32)],
        compiler_params=_cparams(("parallel", "parallel")),
        name="outproj_ln_odd",
    )(x, att, w, g, b, rw, rb)


MOE_CHUNK = 1408
MOE_NCHUNK = D_FF // MOE_CHUNK


def _route(logits, n_tok):
    top_val, top_idx = lax.top_k(logits, TOP_K)
    gates = jax.nn.softmax(top_val, axis=-1)
    n_assign = n_tok * TOP_K
    exp_flat = top_idx.reshape(n_assign).astype(jnp.int32)
    tok_flat = jnp.broadcast_to(jnp.arange(n_tok, dtype=jnp.int32)[:, None], (n_tok, TOP_K)).reshape(n_assign)
    counts = jnp.bincount(exp_flat, length=N_EXPERTS).astype(jnp.int32)
    starts = jnp.cumsum(counts) - counts
    padded_counts = (counts + MOE_BLOCK - 1) // MOE_BLOCK * MOE_BLOCK
    pends = jnp.cumsum(padded_counts)
    pstarts = pends - padded_counts
    order = jnp.argsort(exp_flat)
    s_exp = exp_flat[order]
    dest = (pstarts[s_exp] + jnp.arange(n_assign, dtype=jnp.int32) - starts[s_exp]).astype(jnp.int32)
    n_blocks = -(-n_assign // MOE_BLOCK) + N_EXPERTS
    n_rows = n_blocks * MOE_BLOCK
    row_tok = jnp.zeros((n_rows,), jnp.int32).at[dest].set(tok_flat[order])
    pos = jnp.zeros((n_assign,), jnp.int32).at[order].set(dest)
    blk_exp = jnp.minimum(jnp.searchsorted(pends, jnp.arange(n_blocks, dtype=jnp.int32) * MOE_BLOCK,
                                           side='right'), N_EXPERTS - 1).astype(jnp.int32)
    n_used = (pends[-1] // MOE_BLOCK).astype(jnp.int32)
    return gates, row_tok, pos, blk_exp, n_used, n_blocks


def _moe_body(bexp_ref, nused_ref, rtok_ref, x_hbm, wg_ref, wu_ref, wd_ref, y_ref,
              xf_ref, xb_ref, acc_ref, sem):
    j = pl.program_id(0)
    c = pl.program_id(1)
    active = j < nused_ref[0]

    def row_copy(tok, n):
        return pltpu.make_async_copy(x_hbm.at[pl.ds(tok, 1), :], xf_ref.at[pl.ds(n, 1), :], sem)

    @pl.when(active & (c == 0))
    def _():
        base = j * MOE_BLOCK

        def issue(n, carry):
            row_copy(rtok_ref[base + n], n).start()
            return carry

        lax.fori_loop(0, MOE_BLOCK, issue, 0)
        pltpu.make_async_copy(x_hbm.at[pl.ds(0, MOE_BLOCK), :], xf_ref, sem).wait()
        xb_ref[...] = xf_ref[...].astype(BF16)
        acc_ref[...] = jnp.zeros_like(acc_ref)

    @pl.when(active)
    def _():
        xb = xb_ref[...]
        gt = jnp.dot(xb, wg_ref[...], preferred_element_type=F32)
        up = jnp.dot(xb, wu_ref[...], preferred_element_type=F32)
        hh = (gt * jax.nn.sigmoid(gt) * up).astype(BF16)
        acc_ref[...] += jnp.dot(hh, wd_ref[...], preferred_element_type=F32)

    @pl.when(active & (c == MOE_NCHUNK - 1))
    def _():
        y_ref[...] = acc_ref[...]

    @pl.when(jnp.logical_not(active) & (c == MOE_NCHUNK - 1))
    def _():
        y_ref[...] = jnp.zeros_like(y_ref)


def _moe(x2d, wg, wu, wd, row_tok, blk_exp, n_used, n_blocks):
    n_tok, dim = x2d.shape
    n_rows = n_blocks * MOE_BLOCK
    last = jnp.maximum(n_used - 1, 0)
    jj = jnp.arange(n_blocks, dtype=jnp.int32)
    bexp_eff = blk_exp[jnp.minimum(jj, last)]

    def wcol(j, c, bexp, nused, rtok):
        return (bexp[j], 0, jnp.where(j < nused[0], c, MOE_NCHUNK - 1))

    def wrow(j, c, bexp, nused, rtok):
        return (bexp[j], jnp.where(j < nused[0], c, MOE_NCHUNK - 1), 0)

    grid_spec = pltpu.PrefetchScalarGridSpec(
        num_scalar_prefetch=3,
        grid=(n_blocks, MOE_NCHUNK),
        in_specs=[pl.BlockSpec(memory_space=pl.ANY),
                  pl.BlockSpec((None, dim, MOE_CHUNK), wcol),
                  pl.BlockSpec((None, dim, MOE_CHUNK), wcol),
                  pl.BlockSpec((None, MOE_CHUNK, dim), wrow)],
        out_specs=pl.BlockSpec((MOE_BLOCK, dim), lambda j, c, *_: (j, 0)),
        scratch_shapes=[pltpu.VMEM((MOE_BLOCK, dim), F32),
                        pltpu.VMEM((MOE_BLOCK, dim), BF16),
                        pltpu.VMEM((MOE_BLOCK, dim), F32),
                        pltpu.SemaphoreType.DMA(())],
    )
    return pl.pallas_call(
        _moe_body,
        grid_spec=grid_spec,
        out_shape=jax.ShapeDtypeStruct((n_rows, dim), F32),
        compiler_params=_cparams(("arbitrary", "arbitrary")),
        name="moe_swiglu",
    )(bexp_eff, n_used.reshape(1), row_tok, x2d, wg, wu, wd)


def _combine_body(pos_ref, x_ref, gp_ref, y_hbm, g_ref, b_ref, o_ref, buf_ref, sem, *, tm):
    base = pl.program_id(0) * (tm * TOP_K)

    def row_copy(src, k, n):
        return pltpu.make_async_copy(y_hbm.at[pl.ds(src, 1), :], buf_ref.at[k, pl.ds(n, 1), :], sem)

    def issue(n, carry):
        for k in range(TOP_K):
            row_copy(pos_ref[base + TOP_K * n + k], k, n).start()
        return carry

    lax.fori_loop(0, tm, issue, 0)
    for k in range(TOP_K):
        pltpu.make_async_copy(y_hbm.at[pl.ds(0, tm), :], buf_ref.at[k], sem).wait()
    gp = gp_ref[...]
    moe = gp[:, 0:1] * buf_ref[0] + gp[:, 1:2] * buf_ref[1]
    y = ALPHA * x_ref[...] + moe
    o_ref[...] = _layernorm(y, g_ref[...], b_ref[...])


def _combine(x2d, gates_pad, yrows, pos, g, b):
    n_tok, dim = x2d.shape
    tm = 256
    body = functools.partial(_combine_body, tm=tm)
    grid_spec = pltpu.PrefetchScalarGridSpec(
        num_scalar_prefetch=1,
        grid=(n_tok // tm,),
        in_specs=[pl.BlockSpec((tm, dim), lambda i, p: (i, 0)),
                  pl.BlockSpec((tm, 128), lambda i, p: (i, 0)),
                  pl.BlockSpec(memory_space=pl.ANY),
                  pl.BlockSpec(g.shape, lambda i, p: (0, 0)),
                  pl.BlockSpec(b.shape, lambda i, p: (0, 0))],
        out_specs=pl.BlockSpec((tm, dim), lambda i, p: (i, 0)),
        scratch_shapes=[pltpu.VMEM((TOP_K, tm, dim), F32),
                        pltpu.SemaphoreType.DMA(())],
    )
    return pl.pallas_call(
        body,
        grid_spec=grid_spec,
        out_shape=jax.ShapeDtypeStruct((n_tok, dim), F32),
        compiler_params=_cparams(("arbitrary",)),
        name="moe_combine_ln",
    )(pos, x2d, gates_pad, yrows, g, b)


def _row(v):
    return v.astype(F32).reshape(1, -1)


def _even_layer(x, w_in, a_re, a_im, log_dt, b_re, b_im, c_re, c_im, d_skip, glu_w, glu_b,
                w_out, ln1_g, ln1_b, w_gate, w_up, w_down, ln2_g, ln2_b):
    bsz, seq, dim = x.shape
    u_tm, h = _inproj(x, w_in.astype(BF16))
    wb, wc, lam = _s5_discretize(a_re, a_im, log_dt, b_re, b_im, c_re, c_im, bsz)
    ya_tm = _s5(u_tm.reshape(seq * bsz, S5_CHANNELS), wb, wc, lam, _row(d_skip),
                glu_w.astype(BF16), _row(glu_b), bsz)
    yb = _retention(h)
    w_out = w_out.astype(BF16)
    x1 = _outln_even(x, ya_tm.reshape(seq, bsz * S5_CHANNELS), yb,
                     w_out[:S5_CHANNELS], w_out[S5_CHANNELS:], _row(ln1_g), _row(ln1_b))
    x2 = _ffn(x1.reshape(bsz * seq, dim), w_gate.astype(BF16), w_up.astype(BF16), w_down.astype(BF16),
              _row(ln2_g), _row(ln2_b))
    return x2.reshape(bsz, seq, dim)


def _odd_layer(x, w_qkv, w_out, ln1_g, ln1_b, router_w, router_b, w_gate, w_up, w_down, ln2_g, ln2_b):
    bsz, seq, dim = x.shape
    n_tok = bsz * seq
    qkv = _qkv(x, w_qkv.astype(BF16))
    att = _dilated_attention(qkv)
    rw = jnp.zeros((dim, 128), F32).at[:, :N_EXPERTS].set(router_w.astype(F32))
    rb = jnp.zeros((1, 128), F32).at[:, :N_EXPERTS].set(router_b.astype(F32)[None, :])
    x1, logits = _outln_odd(x, att, w_out.astype(BF16), _row(ln1_g), _row(ln1_b), rw, rb)
    x1 = x1.reshape(n_tok, dim)
    gates, row_tok, pos, blk_exp, n_used, n_blocks = _route(logits.reshape(n_tok, 128)[:, :N_EXPERTS], n_tok)
    yrows = _moe(x1, w_gate.astype(BF16), w_up.astype(BF16), w_down.astype(BF16),
                 row_tok, blk_exp, n_used, n_blocks)
    gates_pad = jnp.zeros((n_tok, 128), F32).at[:, :TOP_K].set(gates)
    out = _combine(x1, gates_pad, yrows, pos, _row(ln2_g), _row(ln2_b))
    return out.reshape(bsz, seq, dim)


def kernel(x, ev_w_in, ev_s5_a_re, ev_s5_a_im, ev_s5_log_dt, ev_s5_b_re, ev_s5_b_im, ev_s5_c_re, ev_s5_c_im, ev_s5_d, ev_s5_glu_w, ev_s5_glu_b, ev_w_out, ev_ln1_g, ev_ln1_b, ev_ffn_w_gate, ev_ffn_w_up, ev_ffn_w_down, ev_ln2_g, ev_ln2_b, od_w_qkv, od_w_out, od_ln1_g, od_ln1_b, od_router_w, od_router_b, od_moe_w_gate, od_moe_w_up, od_moe_w_down, od_ln2_g, od_ln2_b):
    for layer in range(DEPTH):
        i = layer // 2
        if layer % 2 == 0:
            x = _even_layer(x, ev_w_in[i], ev_s5_a_re[i], ev_s5_a_im[i], ev_s5_log_dt[i], ev_s5_b_re[i],
                            ev_s5_b_im[i], ev_s5_c_re[i], ev_s5_c_im[i], ev_s5_d[i], ev_s5_glu_w[i],
                            ev_s5_glu_b[i], ev_w_out[i], ev_ln1_g[i], ev_ln1_b[i], ev_ffn_w_gate[i],
                            ev_ffn_w_up[i], ev_ffn_w_down[i], ev_ln2_g[i], ev_ln2_b[i])
        else:
            x = _odd_layer(x, od_w_qkv[i], od_w_out[i], od_ln1_g[i], od_ln1_b[i], od_router_w[i],
                           od_router_b[i], od_moe_w_gate[i], od_moe_w_up[i], od_moe_w_down[i],
                           od_ln2_g[i], od_ln2_b[i])
    return x
```

```python
import functools
import math

import numpy as np
import jax
import jax.numpy as jnp
from jax import lax
from jax.experimental import pallas as pl
from jax.experimental.pallas import tpu as pltpu

F32 = jnp.float32
BF16 = jnp.bfloat16

LN_EPS = 1e-5
DEPTH = 2
ALPHA = (2.0 * DEPTH) ** 0.25

S5_CHANNELS = 512
S5_GROUPS = 32
S5_GROUP = 16
S5_STATE = 64
S5_NSTATE = S5_GROUPS * S5_STATE

RET_HEADS = 4
RET_HEAD_DIM = 128
RET_CHUNK = 128

ATT_HEADS = 16
ATT_HEAD_DIM = 64
ATT_PAIRS = ATT_HEADS // 2
ATT_BLOCK = 128
ATT_SPAN = 128
ATT_DILATIONS = (1, 4, 16)
NEG_BIG = -1e30

D_FF = 2816
N_EXPERTS = 8
TOP_K = 2
MOE_BLOCK = 512

VMEM_LIMIT = 56 * 1024 * 1024


def _cparams(sem):
    return pltpu.CompilerParams(dimension_semantics=sem, vmem_limit_bytes=VMEM_LIMIT)


def _layernorm(y, g, b):
    mu = jnp.mean(y, -1, keepdims=True)
    yc = y - mu
    var = jnp.mean(yc * yc, -1, keepdims=True)
    return yc * lax.rsqrt(var + LN_EPS) * g + b


def _const_spec(shape):
    nd = len(shape)
    return pl.BlockSpec(shape, lambda *_: (0,) * nd)


def _inproj_body(x_ref, w_ref, h_ref):
    xb = x_ref[...].astype(BF16)
    h_ref[...] = jnp.dot(xb, w_ref[...], preferred_element_type=F32).astype(BF16)


def _inproj(x, w):
    bsz, seq, dim = x.shape
    tm = 512
    nh = w.shape[1]
    return pl.pallas_call(
        _inproj_body,
        grid=(bsz, seq // tm),
        in_specs=[pl.BlockSpec((None, tm, dim), lambda b, s: (b, s, 0)),
                  _const_spec(w.shape)],
        out_specs=pl.BlockSpec((None, tm, nh), lambda b, s: (b, s, 0)),
        out_shape=jax.ShapeDtypeStruct((bsz, seq, nh), BF16),
        compiler_params=_cparams(("parallel", "parallel")),
        name="inproj",
    )(x, w)


def _s5_discretize(a_re, a_im, log_dt, b_re, b_im, c_re, c_im, bsz):
    ar = a_re.astype(F32)
    ai = a_im.astype(F32)
    dt = jnp.exp(log_dt.astype(F32))[:, None]
    mag = jnp.exp(ar * dt)
    lr = mag * jnp.cos(ai * dt)
    li = mag * jnp.sin(ai * dt)
    den = ar * ar + ai * ai
    zr = ((lr - 1.0) * ar + li * ai) / den
    zi = (li * ar - (lr - 1.0) * ai) / den
    br = b_re.astype(F32)
    bi = b_im.astype(F32)
    bbr = zr[..., None] * br - zi[..., None] * bi
    bbi = zr[..., None] * bi + zi[..., None] * br
    eye = jnp.eye(S5_GROUPS, dtype=F32)
    nch = S5_CHANNELS
    wbr = jnp.einsum('gpc,gh->gchp', bbr, eye).reshape(nch, S5_NSTATE)
    wbi = jnp.einsum('gpc,gh->gchp', bbi, eye).reshape(nch, S5_NSTATE)
    wb = jnp.concatenate([wbr, wbi], axis=1).astype(BF16)
    wcr = jnp.einsum('gcp,gh->hpgc', c_re.astype(F32), eye).reshape(S5_NSTATE, nch)
    wci = jnp.einsum('gcp,gh->hpgc', c_im.astype(F32), eye).reshape(S5_NSTATE, nch)
    wc = jnp.concatenate([wcr, -wci], axis=0).astype(BF16)
    lam = jnp.stack([jnp.broadcast_to(lr.reshape(1, S5_NSTATE), (bsz, S5_NSTATE)),
                     jnp.broadcast_to(li.reshape(1, S5_NSTATE), (bsz, S5_NSTATE))])
    return wb, wc, lam


def _gelu_tanh(x):
    return 0.5 * x * (1.0 + jnp.tanh(math.sqrt(2.0 / math.pi) * (x + 0.044715 * (x * x * x))))


def _s5_body(u_ref, wb_ref, wc_ref, lam_ref, d_ref, gw_ref, gb_ref, o_ref, xs_ref, st_ref, utb_ref, ys_ref,
             *, lc, bsz, ncb):
    @pl.when(pl.program_id(0) == 0)
    def _():
        st_ref[...] = jnp.zeros_like(st_ref)

    nlb = S5_CHANNELS // 128
    for b in range(bsz):
        ub = u_ref[b].astype(F32)
        for c in range(nlb):
            utb_ref[c, pl.ds(b, lc, stride=bsz), :] = ub[:, c * 128:(c + 1) * 128]
    uf = jnp.concatenate([utb_ref[c] for c in range(nlb)], axis=-1)
    xs_ref[...] = jnp.dot(uf.astype(BF16), wb_ref[...], preferred_element_type=F32)
    width = S5_NSTATE // ncb
    for cb in range(ncb):
        re = slice(cb * width, (cb + 1) * width)
        im = slice(S5_NSTATE + cb * width, S5_NSTATE + (cb + 1) * width)
        lr = lam_ref[0, :, re]
        li = lam_ref[1, :, re]

        def step(t, carry, re=re, im=im, lr=lr, li=li):
            xr, xi = carry
            row = pl.multiple_of(t * bsz, bsz)
            nxr = lr * xr - li * xi + xs_ref[pl.ds(row, bsz), re]
            nxi = lr * xi + li * xr + xs_ref[pl.ds(row, bsz), im]
            xs_ref[pl.ds(row, bsz), re] = nxr
            xs_ref[pl.ds(row, bsz), im] = nxi
            return nxr, nxi

        xr, xi = lax.fori_loop(0, lc, step, (st_ref[:, re], st_ref[:, im]), unroll=8)
        st_ref[:, re] = xr
        st_ref[:, im] = xi

    y = jnp.dot(xs_ref[...].astype(BF16), wc_ref[...], preferred_element_type=F32)
    y = y + d_ref[...] * uf
    y = _gelu_tanh(y)
    z = jnp.dot(y.astype(BF16), gw_ref[...], preferred_element_type=F32) + gb_ref[...]
    yo = y * jax.nn.sigmoid(z)
    for c in range(nlb):
        ys_ref[c] = yo[:, c * 128:(c + 1) * 128]
    for b in range(bsz):
        o_ref[b] = jnp.concatenate([ys_ref[c, pl.ds(b, lc, stride=bsz), :] for c in range(nlb)],
                                   axis=-1).astype(BF16)


def _s5(h, wb, wc, lam, d_skip, glu_w, glu_b):
    bsz, seq, _ = h.shape
    lc = 64
    blk = lc * bsz
    body = functools.partial(_s5_body, lc=lc, bsz=bsz, ncb=4)
    return pl.pallas_call(
        body,
        grid=(seq // lc,),
        in_specs=[pl.BlockSpec((bsz, lc, S5_CHANNELS), lambda s: (0, s, 0)),
                  _const_spec(wb.shape), _const_spec(wc.shape), _const_spec(lam.shape),
                  _const_spec(d_skip.shape), _const_spec(glu_w.shape), _const_spec(glu_b.shape)],
        out_specs=pl.BlockSpec((bsz, lc, S5_CHANNELS), lambda s: (0, s, 0)),
        out_shape=jax.ShapeDtypeStruct((bsz, seq, S5_CHANNELS), BF16),
        scratch_shapes=[pltpu.VMEM((blk, 2 * S5_NSTATE), F32),
                        pltpu.VMEM((bsz, 2 * S5_NSTATE), F32),
                        pltpu.VMEM((S5_CHANNELS // 128, blk, 128), F32),
                        pltpu.VMEM((S5_CHANNELS // 128, blk, 128), F32)],
        compiler_params=_cparams(("arbitrary",)),
        name="s5_mixer",
    )(h, wb, wc, lam, d_skip, glu_w, glu_b)


def _ret_constants():
    hh = np.arange(RET_HEADS, dtype=np.float64)
    log_gamma = np.log1p(-np.exp2(-5.0 - hh))
    pos = np.arange(RET_CHUNK, dtype=np.float64)
    diff = pos[:, None] - pos[None, :]
    scale = RET_HEAD_DIM ** -0.5
    decay = np.where(diff >= 0, np.exp(log_gamma[:, None, None] * np.maximum(diff, 0.0)), 0.0) * scale
    zeta = np.exp(log_gamma[:, None] * (RET_CHUNK - 1.0 - pos)) * scale
    xi = np.exp(log_gamma[:, None] * (pos + 1.0))
    chunk_decay = np.exp(log_gamma * RET_CHUNK)
    zeta_b = np.broadcast_to(zeta[:, :, None], (RET_HEADS, RET_CHUNK, RET_HEAD_DIM))
    xi_b = np.broadcast_to(xi[:, :, None], (RET_HEADS, RET_CHUNK, RET_HEAD_DIM))
    return (decay.astype(np.float32), np.ascontiguousarray(zeta_b).astype(np.float32),
            np.ascontiguousarray(xi_b).astype(np.float32), [float(c) for c in chunk_decay])


def _ret_body(q_ref, k_ref, v_ref, g_ref, dec_ref, zeta_ref, xi_ref, o_ref, st_ref, *, ts, chunk_decay):
    @pl.when(pl.program_id(1) == 0)
    def _():
        st_ref[...] = jnp.zeros_like(st_ref)

    for c in range(ts // RET_CHUNK):
        rows = slice(c * RET_CHUNK, (c + 1) * RET_CHUNK)
        for h in range(RET_HEADS):
            cols = slice(h * RET_HEAD_DIM, (h + 1) * RET_HEAD_DIM)
            q = q_ref[rows, cols]
            k = k_ref[rows, cols]
            v = v_ref[rows, cols]
            s = lax.dot_general(q, k, (((1,), (1,)), ((), ())), preferred_element_type=F32) * dec_ref[h]
            inner = jnp.dot(s.astype(BF16), v, preferred_element_type=F32)
            st = st_ref[h]
            cross = jnp.dot(q, st.astype(BF16), preferred_element_type=F32) * xi_ref[h]
            ret = inner + cross
            mu = jnp.mean(ret, -1, keepdims=True)
            rc = ret - mu
            var = jnp.mean(rc * rc, -1, keepdims=True)
            rn = rc * lax.rsqrt(var + LN_EPS)
            gg = g_ref[rows, cols].astype(F32)
            o_ref[rows, cols] = (gg * jax.nn.sigmoid(gg) * rn).astype(BF16)
            kz = (k.astype(F32) * zeta_ref[h]).astype(BF16)
            kv = lax.dot_general(kz, v, (((0,), (0,)), ((), ())), preferred_element_type=F32)
            st_ref[h] = st * chunk_decay[h] + kv


def _retention(h, first_block):
    bsz, seq, _ = h.shape
    ts = 512
    width = RET_HEADS * RET_HEAD_DIM
    decay, zeta_b, xi_b, chunk_decay = _ret_constants()
    body = functools.partial(_ret_body, ts=ts, chunk_decay=chunk_decay)

    def col(j):
        return pl.BlockSpec((None, ts, width), lambda b, s, j=j: (b, s, j))

    return pl.pallas_call(
        body,
        grid=(bsz, seq // ts),
        in_specs=[col(first_block), col(first_block + 1), col(first_block + 2), col(first_block + 3),
                  _const_spec(decay.shape), _const_spec(zeta_b.shape), _const_spec(xi_b.shape)],
        out_specs=pl.BlockSpec((None, ts, width), lambda b, s: (b, s, 0)),
        out_shape=jax.ShapeDtypeStruct((bsz, seq, width), BF16),
        scratch_shapes=[pltpu.VMEM((RET_HEADS, RET_HEAD_DIM, RET_HEAD_DIM), F32)],
        compiler_params=_cparams(("parallel", "arbitrary")),
        name="retention",
    )(h, h, h, h, jnp.asarray(decay), jnp.asarray(zeta_b), jnp.asarray(xi_b))


def _outln_even_body(x_ref, ya_ref, yb_ref, w1_ref, w2_ref, g_ref, b_ref, o_ref):
    acc = jnp.dot(ya_ref[...], w1_ref[...], preferred_element_type=F32)
    acc = acc + jnp.dot(yb_ref[...], w2_ref[...], preferred_element_type=F32)
    y = ALPHA * x_ref[...] + acc
    o_ref[...] = _layernorm(y, g_ref[...], b_ref[...])


def _outln_even(x, ya, yb, w1, w2, g, b):
    bsz, seq, dim = x.shape
    tm = 512
    wa = ya.shape[2]
    wbw = yb.shape[2]
    return pl.pallas_call(
        _outln_even_body,
        grid=(bsz, seq // tm),
        in_specs=[pl.BlockSpec((None, tm, dim), lambda b_, s: (b_, s, 0)),
                  pl.BlockSpec((None, tm, wa), lambda b_, s: (b_, s, 0)),
                  pl.BlockSpec((None, tm, wbw), lambda b_, s: (b_, s, 0)),
                  _const_spec(w1.shape), _const_spec(w2.shape),
                  _const_spec(g.shape), _const_spec(b.shape)],
        out_specs=pl.BlockSpec((None, tm, dim), lambda b_, s: (b_, s, 0)),
        out_shape=jax.ShapeDtypeStruct((bsz, seq, dim), F32),
        compiler_params=_cparams(("parallel", "parallel")),
        name="outproj_ln_even",
    )(x, ya, yb, w1, w2, g, b)


FFN_CHUNK = 1408


def _ffn_body(x_ref, wg_ref, wu_ref, wd_ref, g_ref, b_ref, o_ref):
    x = x_ref[...]
    xb = x.astype(BF16)
    acc = None
    for c in range(D_FF // FFN_CHUNK):
        cs = slice(c * FFN_CHUNK, (c + 1) * FFN_CHUNK)
        gt = jnp.dot(xb, wg_ref[:, cs], preferred_element_type=F32)
        up = jnp.dot(xb, wu_ref[:, cs], preferred_element_type=F32)
        hh = (gt * jax.nn.sigmoid(gt) * up).astype(BF16)
        part = jnp.dot(hh, wd_ref[cs, :], preferred_element_type=F32)
        acc = part if acc is None else acc + part
    y = ALPHA * x + acc
    o_ref[...] = _layernorm(y, g_ref[...], b_ref[...])


def _ffn(x2d, wg, wu, wd, g, b):
    n_tok, dim = x2d.shape
    tm = 512
    single = pl.Buffered(1)

    def wspec(shape):
        return pl.BlockSpec(shape, lambda i: (0, 0), pipeline_mode=single)

    return pl.pallas_call(
        _ffn_body,
        grid=(n_tok // tm,),
        in_specs=[pl.BlockSpec((tm, dim), lambda i: (i, 0)),
                  wspec(wg.shape), wspec(wu.shape), wspec(wd.shape),
                  _const_spec(g.shape), _const_spec(b.shape)],
        out_specs=pl.BlockSpec((tm, dim), lambda i: (i, 0)),
        out_shape=jax.ShapeDtypeStruct((n_tok, dim), F32),
        compiler_params=_cparams(("parallel",)),
        name="ffn_swiglu_ln",
    )(x2d, wg, wu, wd, g, b)


def _qkv_body(x_ref, w_ref, o_ref):
    xb = x_ref[...].astype(BF16)
    r = jnp.dot(xb, w_ref[...], preferred_element_type=F32)
    nblk = o_ref.shape[0]
    for j in range(nblk):
        blk = r[:, j * 128:(j + 1) * 128]
        if j < ATT_PAIRS:
            blk = blk * (ATT_HEAD_DIM ** -0.5)
        o_ref[j] = blk.astype(BF16)


def _qkv(x, w):
    bsz, seq, dim = x.shape
    tm = 512
    nblk = w.shape[1] // 128
    return pl.pallas_call(
        _qkv_body,
        grid=(bsz, seq // tm),
        in_specs=[pl.BlockSpec((None, tm, dim), lambda b, s: (b, s, 0)),
                  _const_spec(w.shape)],
        out_specs=pl.BlockSpec((None, nblk, tm, 128), lambda b, s: (b, 0, s, 0)),
        out_shape=jax.ShapeDtypeStruct((bsz, nblk, seq, 128), BF16),
        compiler_params=_cparams(("parallel", "parallel")),
        name="qkv_proj",
    )(x, w)


ATT_GROUP = 4


def _att_constants():
    slopes = np.exp2(-8.0 * np.arange(1, ATT_HEADS + 1, dtype=np.float64) / ATT_HEADS)
    sl = np.zeros((ATT_PAIRS, 2 * ATT_BLOCK, 2 * ATT_BLOCK), np.float32)
    for hp in range(ATT_PAIRS):
        sl[hp, :ATT_BLOCK, :] = slopes[2 * hp]
        sl[hp, ATT_BLOCK:, :] = slopes[2 * hp + 1]
    return sl


def _att_unit(q, kw, vw, bias, old, low, ones, last):
    zero = jnp.zeros_like(q)
    qs = jnp.concatenate([jnp.where(low, q, zero), jnp.where(low, zero, q)], axis=0)
    s = lax.dot_general(qs, kw, (((1,), (1,)), ((), ())), preferred_element_type=F32) + bias
    sa = s[:, :128]
    sb = s[:, 128:]
    mrow = jnp.max(jnp.maximum(sa, sb), axis=-1, keepdims=True)
    m_new = jnp.broadcast_to(mrow, (2 * ATT_BLOCK, 128))
    if old is not None:
        m_old, acc_old, l0_old, l1_old = old
        m_new = jnp.maximum(m_old, m_new)
    p = jnp.concatenate([jnp.exp(sa - m_new), jnp.exp(sb - m_new)], axis=1).astype(BF16)
    vaug = jnp.concatenate([vw, ones], axis=1)
    res = jnp.dot(p, vaug, preferred_element_type=F32)
    acc = jnp.where(low, res[:ATT_BLOCK, :128], res[ATT_BLOCK:, :128])
    l0 = res[:ATT_BLOCK, 128:]
    l1 = res[ATT_BLOCK:, 128:]
    if old is not None:
        a = jnp.exp(m_old - m_new)
        a0 = a[:ATT_BLOCK]
        a1 = a[ATT_BLOCK:]
        acc = jnp.where(low, a0, a1) * acc_old + acc
        l0 = a0 * l0_old + l0
        l1 = a1 * l1_old + l1
    if last:
        acc = acc / jnp.where(low, l0, l1)
    return m_new, acc, l0, l1


def _att_body(q_ref, k_ref, v_ref, sl_ref, o_ref,
              qf_ref, kf_ref, vf_ref, acc_ref, m0_ref, m1_ref, l0_ref, l1_ref, bias_ref, *, seq):
    qf_ref[...] = q_ref[...].astype(F32)
    kf_ref[...] = k_ref[...].astype(F32)
    vf_ref[...] = v_ref[...].astype(F32)
    lane = lax.broadcasted_iota(jnp.int32, (ATT_BLOCK, 128), 1)
    low = lane < ATT_HEAD_DIM
    ones = jnp.ones((2 * ATT_BLOCK, 128), BF16)

    qi = lax.broadcasted_iota(jnp.int32, (2 * ATT_BLOCK, 2 * ATT_BLOCK), 0) & (ATT_BLOCK - 1)
    ki = lax.broadcasted_iota(jnp.int32, (2 * ATT_BLOCK, 2 * ATT_BLOCK), 1)
    sl = sl_ref[...]
    for bi, d in enumerate(ATT_DILATIONS):
        for oi in range(2):
            steps = (qi - ki + oi * ATT_BLOCK).astype(F32)
            allowed = (steps >= 0.0) & (steps <= float(ATT_SPAN))
            bias_ref[2 * bi + oi] = jnp.where(allowed, -(sl * float(d)) * steps, NEG_BIG)

    n_units = seq // ATT_BLOCK
    for bi, d in enumerate(ATT_DILATIONS):
        first = bi == 0
        last = bi == len(ATT_DILATIONS) - 1
        shift = d.bit_length() - 1

        def group(it, carry, bi=bi, d=d, shift=shift, first=first, last=last):
            loaded = []
            for g in range(ATT_GROUP):
                n = it * ATT_GROUP + g
                i = n >> shift
                r = n & (d - 1)
                qstart = r + (d * ATT_BLOCK) * i
                wstart = r + (d * ATT_BLOCK) * jnp.maximum(i - 1, 0)
                if d == 1:
                    qsl = pl.ds(pl.multiple_of(qstart, ATT_BLOCK), ATT_BLOCK)
                    wsl = pl.ds(pl.multiple_of(wstart, ATT_BLOCK), 2 * ATT_BLOCK)
                else:
                    qsl = pl.ds(qstart, ATT_BLOCK, stride=d)
                    wsl = pl.ds(wstart, 2 * ATT_BLOCK, stride=d)
                old = None
                if not first:
                    old = (jnp.concatenate([m0_ref[qsl, :], m1_ref[qsl, :]], axis=0),
                           acc_ref[qsl, :], l0_ref[qsl, :], l1_ref[qsl, :])
                loaded.append((qsl, qf_ref[qsl, :].astype(BF16), kf_ref[wsl, :].astype(BF16),
                               vf_ref[wsl, :].astype(BF16), bias_ref[2 * bi + jnp.minimum(i, 1)], old))
            results = [_att_unit(q, kw, vw, bias, old, low, ones, last)
                       for (_, q, kw, vw, bias, old) in loaded]
            for (qsl, *_), (m_new, acc, l0, l1) in zip(loaded, results):
                acc_ref[qsl, :] = acc
                if not last:
                    l0_ref[qsl, :] = l0
                    l1_ref[qsl, :] = l1
                    m0_ref[qsl, :] = m_new[:ATT_BLOCK]
                    m1_ref[qsl, :] = m_new[ATT_BLOCK:]
            return carry

        lax.fori_loop(0, n_units // ATT_GROUP, group, 0)

    o_ref[...] = acc_ref[...].astype(BF16)


def _dilated_attention(qkv):
    bsz, nblk, seq, _ = qkv.shape
    sl = jnp.asarray(_att_constants())
    specs = [pl.BlockSpec((None, None, seq, 128), lambda b, hp, j=j: (b, j * ATT_PAIRS + hp, 0, 0))
             for j in range(3)]
    specs.append(pl.BlockSpec((None,) + sl.shape[1:], lambda b, hp: (hp, 0, 0)))
    body = functools.partial(_att_body, seq=seq)
    return pl.pallas_call(
        body,
        grid=(bsz, ATT_PAIRS),
        in_specs=specs,
        out_specs=pl.BlockSpec((None, None, seq, 128), lambda b, hp: (b, hp, 0, 0)),
        out_shape=jax.ShapeDtypeStruct((bsz, ATT_PAIRS, seq, 128), BF16),
        scratch_shapes=[pltpu.VMEM((seq, 128), F32) for _ in range(8)]
        + [pltpu.VMEM((2 * len(ATT_DILATIONS), 2 * ATT_BLOCK, 2 * ATT_BLOCK), F32)],
        compiler_params=_cparams(("parallel", "parallel")),
        name="dilated_attention",
    )(qkv, qkv, qkv, sl)


def _outln_odd_body(x_ref, a_ref, w_ref, g_ref, b_ref, rwh_ref, rwl_ref, rb_ref, o_ref, lg_ref):
    a = jnp.concatenate([a_ref[j] for j in range(a_ref.shape[0])], axis=-1)
    acc = jnp.dot(a, w_ref[...], preferred_element_type=F32)
    y = ALPHA * x_ref[...] + acc
    o = _layernorm(y, g_ref[...], b_ref[...])
    o_ref[...] = o
    oh = o.astype(BF16)
    ol = (o - oh.astype(F32)).astype(BF16)
    lg = jnp.dot(oh, rwh_ref[...], preferred_element_type=F32)
    lg = lg + jnp.dot(ol, rwh_ref[...], preferred_element_type=F32)
    lg = lg + jnp.dot(oh, rwl_ref[...], preferred_element_type=F32)
    lg_ref[...] = lg + rb_ref[...]


def _outln_odd(x, att, w, g, b, rw, rb):
    bsz, seq, dim = x.shape
    tm = 512
    npair = att.shape[1]
    rwh = rw.astype(BF16)
    rwl = (rw - rwh.astype(F32)).astype(BF16)
    return pl.pallas_call(
        _outln_odd_body,
        grid=(bsz, seq // tm),
        in_specs=[pl.BlockSpec((None, tm, dim), lambda b_, s: (b_, s, 0)),
                  pl.BlockSpec((None, npair, tm, 128), lambda b_, s: (b_, 0, s, 0)),
                  _const_spec(w.shape), _const_spec(g.shape), _const_spec(b.shape),
                  _const_spec(rw.shape), _const_spec(rw.shape), _const_spec(rb.shape)],
        out_specs=[pl.BlockSpec((None, tm, dim), lambda b_, s: (b_, s, 0)),
                   pl.BlockSpec((None, tm, 128), lambda b_, s: (b_, s, 0))],
        out_shape=[jax.ShapeDtypeStruct((bsz, seq, dim), F32),
                   jax.ShapeDtypeStruct((bsz, seq, 128), F32)],
        compiler_params=_cparams(("parallel", "parallel")),
        name="outproj_ln_odd",
    )(x, att, w, g, b, rwh, rwl, rb)


MOE_CHUNK = 1408
MOE_NCHUNK = D_FF // MOE_CHUNK


def _route(logits, n_tok):
    top_val, top_idx = lax.top_k(logits, TOP_K)
    gates = jax.nn.softmax(top_val, axis=-1)
    n_assign = n_tok * TOP_K
    exp_flat = top_idx.reshape(n_assign).astype(jnp.int32)
    experts = jnp.arange(N_EXPERTS, dtype=jnp.int32)

    def lookup(table, idx):
        return jnp.sum(jnp.where(idx[:, None] == experts[None, :], table[None, :], 0), axis=1)

    counts = jnp.sum((exp_flat[:, None] == experts[None, :]).astype(jnp.int32), axis=0)
    starts = jnp.cumsum(counts) - counts
    padded_counts = (counts + MOE_BLOCK - 1) // MOE_BLOCK * MOE_BLOCK
    pends = jnp.cumsum(padded_counts)
    pstarts = pends - padded_counts
    order = jnp.argsort(exp_flat).astype(jnp.int32)
    rank = jnp.argsort(order).astype(jnp.int32)
    pos = lookup(pstarts - starts, exp_flat) + rank
    n_blocks = -(-n_assign // MOE_BLOCK) + N_EXPERTS
    n_rows = n_blocks * MOE_BLOCK
    blk_exp = jnp.minimum(jnp.sum((pends[None, :] <= (jnp.arange(n_blocks, dtype=jnp.int32) * MOE_BLOCK)[:, None])
                                  .astype(jnp.int32), axis=1), N_EXPERTS - 1)
    rows = jnp.arange(n_rows, dtype=jnp.int32)
    row_exp = jnp.repeat(blk_exp, MOE_BLOCK)
    local = rows - lookup(pstarts, row_exp)
    valid = local < lookup(counts, row_exp)
    spos = jnp.clip(lookup(starts, row_exp) + local, 0, n_assign - 1)
    row_tok = jnp.where(valid, order[spos] // TOP_K, 0)
    n_used = (pends[-1] // MOE_BLOCK).astype(jnp.int32)
    return gates, row_tok, pos, blk_exp, n_used, n_blocks


def _moe_body(bexp_ref, nused_ref, rtok_ref, x_hbm, wg_ref, wu_ref, wd_ref, y_ref,
              xf_ref, xb_ref, acc_ref, sem):
    j = pl.program_id(0)
    c = pl.program_id(1)
    active = j < nused_ref[0]

    def row_copy(tok, n):
        return pltpu.make_async_copy(x_hbm.at[pl.ds(tok, 1), :], xf_ref.at[pl.ds(n, 1), :], sem)

    @pl.when(active & (c == 0))
    def _():
        base = j * MOE_BLOCK

        def issue(n, carry):
            row_copy(rtok_ref[base + n], n).start()
            return carry

        lax.fori_loop(0, MOE_BLOCK, issue, 0)
        pltpu.make_async_copy(x_hbm.at[pl.ds(0, MOE_BLOCK), :], xf_ref, sem).wait()
        xb_ref[...] = xf_ref[...].astype(BF16)
        acc_ref[...] = jnp.zeros_like(acc_ref)

    @pl.when(active)
    def _():
        xb = xb_ref[...]
        gt = jnp.dot(xb, wg_ref[...], preferred_element_type=F32)
        up = jnp.dot(xb, wu_ref[...], preferred_element_type=F32)
        hh = (gt * jax.nn.sigmoid(gt) * up).astype(BF16)
        acc_ref[...] += jnp.dot(hh, wd_ref[...], preferred_element_type=F32)

    @pl.when(active & (c == MOE_NCHUNK - 1))
    def _():
        y_ref[...] = acc_ref[...]

    @pl.when(jnp.logical_not(active) & (c == MOE_NCHUNK - 1))
    def _():
        y_ref[...] = jnp.zeros_like(y_ref)


def _moe(x2d, wg, wu, wd, row_tok, blk_exp, n_used, n_blocks):
    n_tok, dim = x2d.shape
    n_rows = n_blocks * MOE_BLOCK
    last = jnp.maximum(n_used - 1, 0)
    jj = jnp.arange(n_blocks, dtype=jnp.int32)
    bexp_eff = blk_exp[jnp.minimum(jj, last)]

    def wcol(j, c, bexp, nused, rtok):
        return (bexp[j], 0, jnp.where(j < nused[0], c, MOE_NCHUNK - 1))

    def wrow(j, c, bexp, nused, rtok):
        return (bexp[j], jnp.where(j < nused[0], c, MOE_NCHUNK - 1), 0)

    grid_spec = pltpu.PrefetchScalarGridSpec(
        num_scalar_prefetch=3,
        grid=(n_blocks, MOE_NCHUNK),
        in_specs=[pl.BlockSpec(memory_space=pl.ANY),
                  pl.BlockSpec((None, dim, MOE_CHUNK), wcol),
                  pl.BlockSpec((None, dim, MOE_CHUNK), wcol),
                  pl.BlockSpec((None, MOE_CHUNK, dim), wrow)],
        out_specs=pl.BlockSpec((MOE_BLOCK, dim), lambda j, c, *_: (j, 0)),
        scratch_shapes=[pltpu.VMEM((MOE_BLOCK, dim), F32),
                        pltpu.VMEM((MOE_BLOCK, dim), BF16),
                        pltpu.VMEM((MOE_BLOCK, dim), F32),
                        pltpu.SemaphoreType.DMA(())],
    )
    return pl.pallas_call(
        _moe_body,
        grid_spec=grid_spec,
        out_shape=jax.ShapeDtypeStruct((n_rows, dim), F32),
        compiler_params=_cparams(("arbitrary", "arbitrary")),
        name="moe_swiglu",
    )(bexp_eff, n_used.reshape(1), row_tok, x2d, wg, wu, wd)


def _combine_body(pos_ref, x_ref, gp_ref, y_hbm, g_ref, b_ref, o_ref, buf_ref, sem, *, tm):
    base = pl.program_id(0) * (tm * TOP_K)

    def row_copy(src, k, n):
        return pltpu.make_async_copy(y_hbm.at[pl.ds(src, 1), :], buf_ref.at[k, pl.ds(n, 1), :], sem)

    def issue(n, carry):
        for k in range(TOP_K):
            row_copy(pos_ref[base + TOP_K * n + k], k, n).start()
        return carry

    lax.fori_loop(0, tm, issue, 0)
    for k in range(TOP_K):
        pltpu.make_async_copy(y_hbm.at[pl.ds(0, tm), :], buf_ref.at[k], sem).wait()
    gp = gp_ref[...]
    moe = gp[:, 0:1] * buf_ref[0] + gp[:, 1:2] * buf_ref[1]
    y = ALPHA * x_ref[...] + moe
    o_ref[...] = _layernorm(y, g_ref[...], b_ref[...])


def _combine(x2d, gates_pad, yrows, pos, g, b):
    n_tok, dim = x2d.shape
    tm = 256
    body = functools.partial(_combine_body, tm=tm)
    grid_spec = pltpu.PrefetchScalarGridSpec(
        num_scalar_prefetch=1,
        grid=(n_tok // tm,),
        in_specs=[pl.BlockSpec((tm, dim), lambda i, p: (i, 0)),
                  pl.BlockSpec((tm, 128), lambda i, p: (i, 0)),
                  pl.BlockSpec(memory_space=pl.ANY),
                  pl.BlockSpec(g.shape, lambda i, p: (0, 0)),
                  pl.BlockSpec(b.shape, lambda i, p: (0, 0))],
        out_specs=pl.BlockSpec((tm, dim), lambda i, p: (i, 0)),
        scratch_shapes=[pltpu.VMEM((TOP_K, tm, dim), F32),
                        pltpu.SemaphoreType.DMA(())],
    )
    return pl.pallas_call(
        body,
        grid_spec=grid_spec,
        out_shape=jax.ShapeDtypeStruct((n_tok, dim), F32),
        compiler_params=_cparams(("arbitrary",)),
        name="moe_combine_ln",
    )(pos, x2d, gates_pad, yrows, g, b)


def _row(v):
    return v.astype(F32).reshape(1, -1)


def _even_layer(x, w_in, a_re, a_im, log_dt, b_re, b_im, c_re, c_im, d_skip, glu_w, glu_b,
                w_out, ln1_g, ln1_b, w_gate, w_up, w_down, ln2_g, ln2_b):
    bsz, seq, dim = x.shape
    h = _inproj(x, w_in.astype(BF16))
    wb, wc, lam = _s5_discretize(a_re, a_im, log_dt, b_re, b_im, c_re, c_im, bsz)
    ya = _s5(h, wb, wc, lam, _row(d_skip), glu_w.astype(BF16), _row(glu_b))
    yb = _retention(h, 1)
    w_out = w_out.astype(BF16)
    x1 = _outln_even(x, ya, yb, w_out[:S5_CHANNELS], w_out[S5_CHANNELS:], _row(ln1_g), _row(ln1_b))
    x2 = _ffn(x1.reshape(bsz * seq, dim), w_gate.astype(BF16), w_up.astype(BF16), w_down.astype(BF16),
              _row(ln2_g), _row(ln2_b))
    return x2.reshape(bsz, seq, dim)


def _odd_layer(x, w_qkv, w_out, ln1_g, ln1_b, router_w, router_b, w_gate, w_up, w_down, ln2_g, ln2_b):
    bsz, seq, dim = x.shape
    n_tok = bsz * seq
    qkv = _qkv(x, w_qkv.astype(BF16))
    att = _dilated_attention(qkv)
    rw = jnp.zeros((dim, 128), F32).at[:, :N_EXPERTS].set(router_w.astype(F32))
    rb = jnp.zeros((1, 128), F32).at[:, :N_EXPERTS].set(router_b.astype(F32)[None, :])
    x1, logits = _outln_odd(x, att, w_out.astype(BF16), _row(ln1_g), _row(ln1_b), rw, rb)
    x1 = x1.reshape(n_tok, dim)
    gates, row_tok, pos, blk_exp, n_used, n_blocks = _route(logits.reshape(n_tok, 128)[:, :N_EXPERTS], n_tok)
    yrows = _moe(x1, w_gate.astype(BF16), w_up.astype(BF16), w_down.astype(BF16),
                 row_tok, blk_exp, n_used, n_blocks)
    gates_pad = jnp.zeros((n_tok, 128), F32).at[:, :TOP_K].set(gates)
    out = _combine(x1, gates_pad, yrows, pos, _row(ln2_g), _row(ln2_b))
    return out.reshape(bsz, seq, dim)


def kernel(x, ev_w_in, ev_s5_a_re, ev_s5_a_im, ev_s5_log_dt, ev_s5_b_re, ev_s5_b_im, ev_s5_c_re, ev_s5_c_im, ev_s5_d, ev_s5_glu_w, ev_s5_glu_b, ev_w_out, ev_ln1_g, ev_ln1_b, ev_ffn_w_gate, ev_ffn_w_up, ev_ffn_w_down, ev_ln2_g, ev_ln2_b, od_w_qkv, od_w_out, od_ln1_g, od_ln1_b, od_router_w, od_router_b, od_moe_w_gate, od_moe_w_up, od_moe_w_down, od_ln2_g, od_ln2_b):
    for layer in range(DEPTH):
        i = layer // 2
        if layer % 2 == 0:
            x = _even_layer(x, ev_w_in[i], ev_s5_a_re[i], ev_s5_a_im[i], ev_s5_log_dt[i], ev_s5_b_re[i],
                            ev_s5_b_im[i], ev_s5_c_re[i], ev_s5_c_im[i], ev_s5_d[i], ev_s5_glu_w[i],
                            ev_s5_glu_b[i], ev_w_out[i], ev_ln1_g[i], ev_ln1_b[i], ev_ffn_w_gate[i],
                            ev_ffn_w_up[i], ev_ffn_w_down[i], ev_ln2_g[i], ev_ln2_b[i])
        else:
            x = _odd_layer(x, od_w_qkv[i], od_w_out[i], od_ln1_g[i], od_ln1_b[i], od_router_w[i],
                           od_router_b[i], od_moe_w_gate[i], od_moe_w_up[i], od_moe_w_down[i],
                           od_ln2_g[i], od_ln2_b[i])
    return x
```

```python
import functools
import math

import numpy as np
import jax
import jax.numpy as jnp
from jax import lax
from jax.experimental import pallas as pl
from jax.experimental.pallas import tpu as pltpu

F32 = jnp.float32
BF16 = jnp.bfloat16

LN_EPS = 1e-5
DEPTH = 2
ALPHA = (2.0 * DEPTH) ** 0.25

S5_CHANNELS = 512
S5_GROUPS = 32
S5_GROUP = 16
S5_STATE = 64
S5_NSTATE = S5_GROUPS * S5_STATE

RET_HEADS = 4
RET_HEAD_DIM = 128
RET_CHUNK = 128

ATT_HEADS = 16
ATT_HEAD_DIM = 64
ATT_PAIRS = ATT_HEADS // 2
ATT_BLOCK = 128
ATT_SPAN = 128
ATT_DILATIONS = (1, 4, 16)
NEG_BIG = -1e30

D_FF = 2816
N_EXPERTS = 8
TOP_K = 2
MOE_BLOCK = 512

VMEM_LIMIT = 56 * 1024 * 1024


def _cparams(sem):
    return pltpu.CompilerParams(dimension_semantics=sem, vmem_limit_bytes=VMEM_LIMIT)


def _layernorm(y, g, b):
    mu = jnp.mean(y, -1, keepdims=True)
    yc = y - mu
    var = jnp.mean(yc * yc, -1, keepdims=True)
    return yc * lax.rsqrt(var + LN_EPS) * g + b


def _const_spec(shape):
    nd = len(shape)
    return pl.BlockSpec(shape, lambda *_: (0,) * nd)


def _inproj_body(x_ref, w_ref, h_ref):
    xb = x_ref[...].astype(BF16)
    h_ref[...] = jnp.dot(xb, w_ref[...], preferred_element_type=F32).astype(BF16)


def _inproj(x, w):
    bsz, seq, dim = x.shape
    tm = 512
    nh = w.shape[1]
    return pl.pallas_call(
        _inproj_body,
        grid=(bsz, seq // tm),
        in_specs=[pl.BlockSpec((None, tm, dim), lambda b, s: (b, s, 0)),
                  _const_spec(w.shape)],
        out_specs=pl.BlockSpec((None, tm, nh), lambda b, s: (b, s, 0)),
        out_shape=jax.ShapeDtypeStruct((bsz, seq, nh), BF16),
        compiler_params=_cparams(("parallel", "parallel")),
        name="inproj",
    )(x, w)


def _s5_discretize(a_re, a_im, log_dt, b_re, b_im, c_re, c_im, bsz):
    ar = a_re.astype(F32)
    ai = a_im.astype(F32)
    dt = jnp.exp(log_dt.astype(F32))[:, None]
    mag = jnp.exp(ar * dt)
    lr = mag * jnp.cos(ai * dt)
    li = mag * jnp.sin(ai * dt)
    den = ar * ar + ai * ai
    zr = ((lr - 1.0) * ar + li * ai) / den
    zi = (li * ar - (lr - 1.0) * ai) / den
    br = b_re.astype(F32)
    bi = b_im.astype(F32)
    bbr = zr[..., None] * br - zi[..., None] * bi
    bbi = zr[..., None] * bi + zi[..., None] * br
    eye = jnp.eye(S5_GROUPS, dtype=F32)
    nch = S5_CHANNELS
    wbr = jnp.einsum('gpc,gh->gchp', bbr, eye).reshape(nch, S5_NSTATE)
    wbi = jnp.einsum('gpc,gh->gchp', bbi, eye).reshape(nch, S5_NSTATE)
    wb = jnp.concatenate([wbr, wbi], axis=1).astype(BF16)
    wcr = jnp.einsum('gcp,gh->hpgc', c_re.astype(F32), eye).reshape(S5_NSTATE, nch)
    wci = jnp.einsum('gcp,gh->hpgc', c_im.astype(F32), eye).reshape(S5_NSTATE, nch)
    wc = jnp.concatenate([wcr, -wci], axis=0).astype(BF16)
    lam = jnp.stack([jnp.broadcast_to(lr.reshape(1, S5_NSTATE), (bsz, S5_NSTATE)),
                     jnp.broadcast_to(li.reshape(1, S5_NSTATE), (bsz, S5_NSTATE))])
    return wb, wc, lam


def _gelu_tanh(x):
    return 0.5 * x * (1.0 + jnp.tanh(math.sqrt(2.0 / math.pi) * (x + 0.044715 * (x * x * x))))


def _s5_body(u_ref, wb_ref, wc_ref, lam_ref, d_ref, gw_ref, gb_ref, o_ref, xs_ref, st_ref, utb_ref, ys_ref,
             *, lc, bsz, ncb):
    @pl.when(pl.program_id(0) == 0)
    def _():
        st_ref[...] = jnp.zeros_like(st_ref)

    nlb = S5_CHANNELS // 128
    for b in range(bsz):
        ub = u_ref[b].astype(F32)
        for c in range(nlb):
            utb_ref[c, pl.ds(b, lc, stride=bsz), :] = ub[:, c * 128:(c + 1) * 128]
    uf = jnp.concatenate([utb_ref[c] for c in range(nlb)], axis=-1)
    xs_ref[...] = jnp.dot(uf.astype(BF16), wb_ref[...], preferred_element_type=F32)
    width = S5_NSTATE // ncb
    for cb in range(ncb):
        re = slice(cb * width, (cb + 1) * width)
        im = slice(S5_NSTATE + cb * width, S5_NSTATE + (cb + 1) * width)
        lr = lam_ref[0, :, re]
        li = lam_ref[1, :, re]

        def step(t, carry, re=re, im=im, lr=lr, li=li):
            xr, xi = carry
            row = pl.multiple_of(t * bsz, bsz)
            nxr = lr * xr - li * xi + xs_ref[pl.ds(row, bsz), re]
            nxi = lr * xi + li * xr + xs_ref[pl.ds(row, bsz), im]
            xs_ref[pl.ds(row, bsz), re] = nxr
            xs_ref[pl.ds(row, bsz), im] = nxi
            return nxr, nxi

        xr, xi = lax.fori_loop(0, lc, step, (st_ref[:, re], st_ref[:, im]), unroll=8)
        st_ref[:, re] = xr
        st_ref[:, im] = xi

    y = jnp.dot(xs_ref[...].astype(BF16), wc_ref[...], preferred_element_type=F32)
    y = y + d_ref[...] * uf
    y = _gelu_tanh(y)
    z = jnp.dot(y.astype(BF16), gw_ref[...], preferred_element_type=F32) + gb_ref[...]
    yo = y * jax.nn.sigmoid(z)
    for c in range(nlb):
        ys_ref[c] = yo[:, c * 128:(c + 1) * 128]
    for b in range(bsz):
        o_ref[b] = jnp.concatenate([ys_ref[c, pl.ds(b, lc, stride=bsz), :] for c in range(nlb)],
                                   axis=-1).astype(BF16)


def _s5(h, wb, wc, lam, d_skip, glu_w, glu_b):
    bsz, seq, _ = h.shape
    lc = 64
    blk = lc * bsz
    body = functools.partial(_s5_body, lc=lc, bsz=bsz, ncb=4)
    return pl.pallas_call(
        body,
        grid=(seq // lc,),
        in_specs=[pl.BlockSpec((bsz, lc, S5_CHANNELS), lambda s: (0, s, 0)),
                  _const_spec(wb.shape), _const_spec(wc.shape), _const_spec(lam.shape),
                  _const_spec(d_skip.shape), _const_spec(glu_w.shape), _const_spec(glu_b.shape)],
        out_specs=pl.BlockSpec((bsz, lc, S5_CHANNELS), lambda s: (0, s, 0)),
        out_shape=jax.ShapeDtypeStruct((bsz, seq, S5_CHANNELS), BF16),
        scratch_shapes=[pltpu.VMEM((blk, 2 * S5_NSTATE), F32),
                        pltpu.VMEM((bsz, 2 * S5_NSTATE), F32),
                        pltpu.VMEM((S5_CHANNELS // 128, blk, 128), F32),
                        pltpu.VMEM((S5_CHANNELS // 128, blk, 128), F32)],
        compiler_params=_cparams(("arbitrary",)),
        name="s5_mixer",
    )(h, wb, wc, lam, d_skip, glu_w, glu_b)


def _ret_constants():
    hh = np.arange(RET_HEADS, dtype=np.float64)
    log_gamma = np.log1p(-np.exp2(-5.0 - hh))
    pos = np.arange(RET_CHUNK, dtype=np.float64)
    diff = pos[:, None] - pos[None, :]
    scale = RET_HEAD_DIM ** -0.5
    decay = np.where(diff >= 0, np.exp(log_gamma[:, None, None] * np.maximum(diff, 0.0)), 0.0) * scale
    zeta = np.exp(log_gamma[:, None] * (RET_CHUNK - 1.0 - pos)) * scale
    xi = np.exp(log_gamma[:, None] * (pos + 1.0))
    chunk_decay = np.exp(log_gamma * RET_CHUNK)
    zeta_b = np.broadcast_to(zeta[:, :, None], (RET_HEADS, RET_CHUNK, RET_HEAD_DIM))
    xi_b = np.broadcast_to(xi[:, :, None], (RET_HEADS, RET_CHUNK, RET_HEAD_DIM))
    return (decay.astype(np.float32), np.ascontiguousarray(zeta_b).astype(np.float32),
            np.ascontiguousarray(xi_b).astype(np.float32), [float(c) for c in chunk_decay])


def _ret_body(q_ref, k_ref, v_ref, g_ref, dec_ref, zeta_ref, xi_ref, o_ref, st_ref, *, ts, chunk_decay):
    @pl.when(pl.program_id(1) == 0)
    def _():
        st_ref[...] = jnp.zeros_like(st_ref)

    for c in range(ts // RET_CHUNK):
        rows = slice(c * RET_CHUNK, (c + 1) * RET_CHUNK)
        for h in range(RET_HEADS):
            cols = slice(h * RET_HEAD_DIM, (h + 1) * RET_HEAD_DIM)
            q = q_ref[rows, cols]
            k = k_ref[rows, cols]
            v = v_ref[rows, cols]
            s = lax.dot_general(q, k, (((1,), (1,)), ((), ())), preferred_element_type=F32) * dec_ref[h]
            inner = jnp.dot(s.astype(BF16), v, preferred_element_type=F32)
            st = st_ref[h]
            cross = jnp.dot(q, st.astype(BF16), preferred_element_type=F32) * xi_ref[h]
            ret = inner + cross
            mu = jnp.mean(ret, -1, keepdims=True)
            rc = ret - mu
            var = jnp.mean(rc * rc, -1, keepdims=True)
            rn = rc * lax.rsqrt(var + LN_EPS)
            gg = g_ref[rows, cols].astype(F32)
            o_ref[rows, cols] = (gg * jax.nn.sigmoid(gg) * rn).astype(BF16)
            kz = (k.astype(F32) * zeta_ref[h]).astype(BF16)
            kv = lax.dot_general(kz, v, (((0,), (0,)), ((), ())), preferred_element_type=F32)
            st_ref[h] = st * chunk_decay[h] + kv


def _retention(h, first_block):
    bsz, seq, _ = h.shape
    ts = 512
    width = RET_HEADS * RET_HEAD_DIM
    decay, zeta_b, xi_b, chunk_decay = _ret_constants()
    body = functools.partial(_ret_body, ts=ts, chunk_decay=chunk_decay)

    def col(j):
        return pl.BlockSpec((None, ts, width), lambda b, s, j=j: (b, s, j))

    return pl.pallas_call(
        body,
        grid=(bsz, seq // ts),
        in_specs=[col(first_block), col(first_block + 1), col(first_block + 2), col(first_block + 3),
                  _const_spec(decay.shape), _const_spec(zeta_b.shape), _const_spec(xi_b.shape)],
        out_specs=pl.BlockSpec((None, ts, width), lambda b, s: (b, s, 0)),
        out_shape=jax.ShapeDtypeStruct((bsz, seq, width), BF16),
        scratch_shapes=[pltpu.VMEM((RET_HEADS, RET_HEAD_DIM, RET_HEAD_DIM), F32)],
        compiler_params=_cparams(("parallel", "arbitrary")),
        name="retention",
    )(h, h, h, h, jnp.asarray(decay), jnp.asarray(zeta_b), jnp.asarray(xi_b))


def _outln_even_body(x_ref, ya_ref, yb_ref, w1_ref, w2_ref, g_ref, b_ref, o_ref):
    acc = jnp.dot(ya_ref[...], w1_ref[...], preferred_element_type=F32)
    acc = acc + jnp.dot(yb_ref[...], w2_ref[...], preferred_element_type=F32)
    y = ALPHA * x_ref[...] + acc
    o_ref[...] = _layernorm(y, g_ref[...], b_ref[...])


def _outln_even(x, ya, yb, w1, w2, g, b):
    bsz, seq, dim = x.shape
    tm = 512
    wa = ya.shape[2]
    wbw = yb.shape[2]
    return pl.pallas_call(
        _outln_even_body,
        grid=(bsz, seq // tm),
        in_specs=[pl.BlockSpec((None, tm, dim), lambda b_, s: (b_, s, 0)),
                  pl.BlockSpec((None, tm, wa), lambda b_, s: (b_, s, 0)),
                  pl.BlockSpec((None, tm, wbw), lambda b_, s: (b_, s, 0)),
                  _const_spec(w1.shape), _const_spec(w2.shape),
                  _const_spec(g.shape), _const_spec(b.shape)],
        out_specs=pl.BlockSpec((None, tm, dim), lambda b_, s: (b_, s, 0)),
        out_shape=jax.ShapeDtypeStruct((bsz, seq, dim), F32),
        compiler_params=_cparams(("parallel", "parallel")),
        name="outproj_ln_even",
    )(x, ya, yb, w1, w2, g, b)


FFN_CHUNK = 1408


def _ffn_body(x_ref, wg_ref, wu_ref, wd_ref, g_ref, b_ref, o_ref):
    x = x_ref[...]
    xb = x.astype(BF16)
    acc = None
    for c in range(D_FF // FFN_CHUNK):
        cs = slice(c * FFN_CHUNK, (c + 1) * FFN_CHUNK)
        gt = jnp.dot(xb, wg_ref[:, cs], preferred_element_type=F32)
        up = jnp.dot(xb, wu_ref[:, cs], preferred_element_type=F32)
        hh = (gt * jax.nn.sigmoid(gt) * up).astype(BF16)
        part = jnp.dot(hh, wd_ref[cs, :], preferred_element_type=F32)
        acc = part if acc is None else acc + part
    y = ALPHA * x + acc
    o_ref[...] = _layernorm(y, g_ref[...], b_ref[...])


def _ffn(x2d, wg, wu, wd, g, b):
    n_tok, dim = x2d.shape
    tm = 512
    single = pl.Buffered(1)

    def wspec(shape):
        return pl.BlockSpec(shape, lambda i: (0, 0), pipeline_mode=single)

    return pl.pallas_call(
        _ffn_body,
        grid=(n_tok // tm,),
        in_specs=[pl.BlockSpec((tm, dim), lambda i: (i, 0)),
                  wspec(wg.shape), wspec(wu.shape), wspec(wd.shape),
                  _const_spec(g.shape), _const_spec(b.shape)],
        out_specs=pl.BlockSpec((tm, dim), lambda i: (i, 0)),
        out_shape=jax.ShapeDtypeStruct((n_tok, dim), F32),
        compiler_params=_cparams(("parallel",)),
        name="ffn_swiglu_ln",
    )(x2d, wg, wu, wd, g, b)


def _qkv_body(x_ref, w_ref, o_ref):
    xb = x_ref[...].astype(BF16)
    r = jnp.dot(xb, w_ref[...], preferred_element_type=F32)
    nblk = o_ref.shape[0]
    for j in range(nblk):
        blk = r[:, j * 128:(j + 1) * 128]
        if j < ATT_PAIRS:
            blk = blk * (ATT_HEAD_DIM ** -0.5)
        o_ref[j] = blk.astype(BF16)


def _qkv(x, w):
    bsz, seq, dim = x.shape
    tm = 512
    nblk = w.shape[1] // 128
    return pl.pallas_call(
        _qkv_body,
        grid=(bsz, seq // tm),
        in_specs=[pl.BlockSpec((None, tm, dim), lambda b, s: (b, s, 0)),
                  _const_spec(w.shape)],
        out_specs=pl.BlockSpec((None, nblk, tm, 128), lambda b, s: (b, 0, s, 0)),
        out_shape=jax.ShapeDtypeStruct((bsz, nblk, seq, 128), BF16),
        compiler_params=_cparams(("parallel", "parallel")),
        name="qkv_proj",
    )(x, w)


ATT_GROUP = 4


def _att_constants():
    slopes = np.exp2(-8.0 * np.arange(1, ATT_HEADS + 1, dtype=np.float64) / ATT_HEADS)
    sl = np.zeros((ATT_PAIRS, 2 * ATT_BLOCK, 2 * ATT_BLOCK), np.float32)
    for hp in range(ATT_PAIRS):
        sl[hp, :ATT_BLOCK, :] = slopes[2 * hp]
        sl[hp, ATT_BLOCK:, :] = slopes[2 * hp + 1]
    return sl


def _att_unit(q, kw, vw, bias, old, low, ones, last):
    zero = jnp.zeros_like(q)
    qs = jnp.concatenate([jnp.where(low, q, zero), jnp.where(low, zero, q)], axis=0)
    s = lax.dot_general(qs, kw, (((1,), (1,)), ((), ())), preferred_element_type=F32) + bias
    sa = s[:, :128]
    sb = s[:, 128:]
    mrow = jnp.max(jnp.maximum(sa, sb), axis=-1, keepdims=True)
    m_new = jnp.broadcast_to(mrow, (2 * ATT_BLOCK, 128))
    if old is not None:
        m_old, acc_old, l0_old, l1_old = old
        m_new = jnp.maximum(m_old, m_new)
    p = jnp.concatenate([jnp.exp(sa - m_new), jnp.exp(sb - m_new)], axis=1).astype(BF16)
    vaug = jnp.concatenate([vw, ones], axis=1)
    res = jnp.dot(p, vaug, preferred_element_type=F32)
    acc = jnp.where(low, res[:ATT_BLOCK, :128], res[ATT_BLOCK:, :128])
    l0 = res[:ATT_BLOCK, 128:]
    l1 = res[ATT_BLOCK:, 128:]
    if old is not None:
        a = jnp.exp(m_old - m_new)
        a0 = a[:ATT_BLOCK]
        a1 = a[ATT_BLOCK:]
        acc = jnp.where(low, a0, a1) * acc_old + acc
        l0 = a0 * l0_old + l0
        l1 = a1 * l1_old + l1
    if last:
        acc = acc / jnp.where(low, l0, l1)
    return m_new, acc, l0, l1


def _att_body(q_ref, k_ref, v_ref, sl_ref, o_ref,
              tmp_ref, q4_ref, k4_ref, v4_ref, nat_ref, de4_ref, bias_ref, *, seq):
    quarter = seq // 4
    for src, dst in ((q_ref, q4_ref), (k_ref, k4_ref), (v_ref, v4_ref)):
        tmp_ref[...] = src[...].astype(F32)
        for r in range(4):
            dst[r] = tmp_ref[pl.ds(r, quarter, stride=4), :]
    lane = lax.broadcasted_iota(jnp.int32, (ATT_BLOCK, 128), 1)
    low = lane < ATT_HEAD_DIM
    ones = jnp.ones((2 * ATT_BLOCK, 128), BF16)

    qi = lax.broadcasted_iota(jnp.int32, (2 * ATT_BLOCK, 2 * ATT_BLOCK), 0) & (ATT_BLOCK - 1)
    ki = lax.broadcasted_iota(jnp.int32, (2 * ATT_BLOCK, 2 * ATT_BLOCK), 1)
    sl = sl_ref[...]
    for bi, d in enumerate(ATT_DILATIONS):
        for oi in range(2):
            steps = (qi - ki + oi * ATT_BLOCK).astype(F32)
            allowed = (steps >= 0.0) & (steps <= float(ATT_SPAN))
            bias_ref[2 * bi + oi] = jnp.where(allowed, -(sl * float(d)) * steps, NEG_BIG)

    n_units = seq // ATT_BLOCK
    for bi, d in enumerate(ATT_DILATIONS):
        first = bi == 0
        last = bi == len(ATT_DILATIONS) - 1
        shift = d.bit_length() - 1

        def group(it, carry, bi=bi, d=d, shift=shift, first=first, last=last):
            loaded = []
            for g in range(ATT_GROUP):
                n = it * ATT_GROUP + g
                i = n >> shift
                r = n & (d - 1)
                iw = jnp.maximum(i - 1, 0)
                if d == 1:
                    qsl = pl.ds(pl.multiple_of(i * ATT_BLOCK, ATT_BLOCK), ATT_BLOCK)
                    wsl = pl.ds(pl.multiple_of(iw * ATT_BLOCK, ATT_BLOCK), 2 * ATT_BLOCK)
                    data = (q_ref[qsl, :], k_ref[wsl, :], v_ref[wsl, :])
                    old_at = None
                    new_at = lambda s, qsl=qsl: nat_ref.at[s, qsl, :]
                else:
                    if d == 4:
                        r4 = r
                        qsl = pl.ds(pl.multiple_of(i * ATT_BLOCK, ATT_BLOCK), ATT_BLOCK)
                        wsl = pl.ds(pl.multiple_of(iw * ATT_BLOCK, ATT_BLOCK), 2 * ATT_BLOCK)
                        osl = pl.ds(r + (4 * ATT_BLOCK) * i, ATT_BLOCK, stride=4)
                        old_at = lambda s, osl=osl: nat_ref.at[s, osl, :]
                    else:
                        r4 = r & 3
                        qsl = pl.ds((r >> 2) + (4 * ATT_BLOCK) * i, ATT_BLOCK, stride=4)
                        wsl = pl.ds((r >> 2) + (4 * ATT_BLOCK) * iw, 2 * ATT_BLOCK, stride=4)
                        old_at = lambda s, r4=r4, qsl=qsl: de4_ref.at[s, r4, qsl, :]
                    data = (q4_ref[r4, qsl, :].astype(BF16), k4_ref[r4, wsl, :].astype(BF16),
                            v4_ref[r4, wsl, :].astype(BF16))
                    new_at = lambda s, r4=r4, qsl=qsl: de4_ref.at[s, r4, qsl, :]
                old = None
                if old_at is not None:
                    old = (jnp.concatenate([old_at(1)[...], old_at(2)[...]], axis=0),
                           old_at(0)[...], old_at(3)[...], old_at(4)[...])
                loaded.append((new_at, data, bias_ref[2 * bi + jnp.minimum(i, 1)], old))
            results = [_att_unit(q, kw, vw, bias, old, low, ones, last)
                       for (_, (q, kw, vw), bias, old) in loaded]
            for (new_at, *_), (m_new, acc, l0, l1) in zip(loaded, results):
                new_at(0)[...] = acc
                if not last:
                    new_at(1)[...] = m_new[:ATT_BLOCK]
                    new_at(2)[...] = m_new[ATT_BLOCK:]
                    new_at(3)[...] = l0
                    new_at(4)[...] = l1
            return carry

        lax.fori_loop(0, n_units // ATT_GROUP, group, 0)

    for r in range(4):
        nat_ref[0, pl.ds(r, quarter, stride=4), :] = de4_ref[0, r]
    o_ref[...] = nat_ref[0].astype(BF16)


def _dilated_attention(qkv):
    bsz, nblk, seq, _ = qkv.shape
    sl = jnp.asarray(_att_constants())
    specs = [pl.BlockSpec((None, None, seq, 128), lambda b, hp, j=j: (b, j * ATT_PAIRS + hp, 0, 0))
             for j in range(3)]
    specs.append(pl.BlockSpec((None,) + sl.shape[1:], lambda b, hp: (hp, 0, 0)))
    body = functools.partial(_att_body, seq=seq)
    return pl.pallas_call(
        body,
        grid=(bsz, ATT_PAIRS),
        in_specs=specs,
        out_specs=pl.BlockSpec((None, None, seq, 128), lambda b, hp: (b, hp, 0, 0)),
        out_shape=jax.ShapeDtypeStruct((bsz, ATT_PAIRS, seq, 128), BF16),
        scratch_shapes=[pltpu.VMEM((seq, 128), F32)]
        + [pltpu.VMEM((4, seq // 4, 128), F32) for _ in range(3)]
        + [pltpu.VMEM((5, seq, 128), F32), pltpu.VMEM((5, 4, seq // 4, 128), F32),
           pltpu.VMEM((2 * len(ATT_DILATIONS), 2 * ATT_BLOCK, 2 * ATT_BLOCK), F32)],
        compiler_params=_cparams(("parallel", "parallel")),
        name="dilated_attention",
    )(qkv, qkv, qkv, sl)


def _outln_odd_body(x_ref, a_ref, w_ref, g_ref, b_ref, rwh_ref, rwl_ref, rb_ref, o_ref, lg_ref):
    a = jnp.concatenate([a_ref[j] for j in range(a_ref.shape[0])], axis=-1)
    acc = jnp.dot(a, w_ref[...], preferred_element_type=F32)
    y = ALPHA * x_ref[...] + acc
    o = _layernorm(y, g_ref[...], b_ref[...])
    o_ref[...] = o
    oh = o.astype(BF16)
    ol = (o - oh.astype(F32)).astype(BF16)
    lg = jnp.dot(oh, rwh_ref[...], preferred_element_type=F32)
    lg = lg + jnp.dot(ol, rwh_ref[...], preferred_element_type=F32)
    lg = lg + jnp.dot(oh, rwl_ref[...], preferred_element_type=F32)
    lg_ref[...] = lg + rb_ref[...]


def _outln_odd(x, att, w, g, b, rw, rb):
    bsz, seq, dim = x.shape
    tm = 512
    npair = att.shape[1]
    rwh = rw.astype(BF16)
    rwl = (rw - rwh.astype(F32)).astype(BF16)
    return pl.pallas_call(
        _outln_odd_body,
        grid=(bsz, seq // tm),
        in_specs=[pl.BlockSpec((None, tm, dim), lambda b_, s: (b_, s, 0)),
                  pl.BlockSpec((None, npair, tm, 128), lambda b_, s: (b_, 0, s, 0)),
                  _const_spec(w.shape), _const_spec(g.shape), _const_spec(b.shape),
                  _const_spec(rw.shape), _const_spec(rw.shape), _const_spec(rb.shape)],
        out_specs=[pl.BlockSpec((None, tm, dim), lambda b_, s: (b_, s, 0)),
                   pl.BlockSpec((None, tm, 128), lambda b_, s: (b_, s, 0))],
        out_shape=[jax.ShapeDtypeStruct((bsz, seq, dim), F32),
                   jax.ShapeDtypeStruct((bsz, seq, 128), F32)],
        compiler_params=_cparams(("parallel", "parallel")),
        name="outproj_ln_odd",
    )(x, att, w, g, b, rwh, rwl, rb)


MOE_FF_BOUNDS = (0, 1536, 2816)
MOE_NCHUNK = len(MOE_FF_BOUNDS) - 1
MOE_TRASH_BLOCKS = 2 + 2 * N_EXPERTS


def _route(logits, n_tok):
    top_val, top_idx = lax.top_k(logits, TOP_K)
    gates = jax.nn.softmax(top_val, axis=-1)
    n_assign = n_tok * TOP_K
    exp_flat = top_idx.reshape(n_assign).astype(jnp.int32)
    experts = jnp.arange(N_EXPERTS, dtype=jnp.int32)

    def lookup(table, idx):
        return jnp.sum(jnp.where(idx[:, None] == experts[None, :], table[None, :], 0), axis=1)

    counts = jnp.sum((exp_flat[:, None] == experts[None, :]).astype(jnp.int32), axis=0)
    starts = jnp.cumsum(counts) - counts
    padded_counts = (counts + MOE_BLOCK - 1) // MOE_BLOCK * MOE_BLOCK
    pends = jnp.cumsum(padded_counts)
    pstarts = pends - padded_counts
    order = jnp.argsort(exp_flat).astype(jnp.int32)
    n_blocks = -(-n_assign // MOE_BLOCK) + N_EXPERTS
    n_rows = n_blocks * MOE_BLOCK
    blk_exp = jnp.minimum(jnp.sum((pends[None, :] <= (jnp.arange(n_blocks, dtype=jnp.int32) * MOE_BLOCK)[:, None])
                                  .astype(jnp.int32), axis=1), N_EXPERTS - 1)
    rows = jnp.arange(n_rows, dtype=jnp.int32)
    row_exp = jnp.repeat(blk_exp, MOE_BLOCK)
    local = rows - lookup(pstarts, row_exp)
    valid = local < lookup(counts, row_exp)
    spos = jnp.clip(lookup(starts, row_exp) + local, 0, n_assign - 1)
    n_used = (pends[-1] // MOE_BLOCK).astype(jnp.int32)
    assign = jnp.where(valid, order[spos], -1)
    real_dst = (assign % TOP_K) * n_tok + assign // TOP_K
    in_blk = rows % MOE_BLOCK
    blk = rows // MOE_BLOCK
    trash_blk = jnp.where(blk < n_used, 1 + row_exp, 1 + N_EXPERTS + blk - n_used)
    dst_mid = jnp.where(valid, real_dst, n_assign + trash_blk * MOE_BLOCK + in_blk)
    lane_ids = jnp.arange(MOE_BLOCK, dtype=jnp.int32)
    dst_front = n_assign + lane_ids
    dst_back = n_assign + (1 + N_EXPERTS + n_blocks - n_used) * MOE_BLOCK + lane_ids
    dst_row = jnp.concatenate([dst_front, dst_mid, dst_back]).reshape(n_blocks + 2, MOE_BLOCK)
    zeros = jnp.zeros((MOE_BLOCK,), jnp.int32)
    src_row = jnp.concatenate([zeros, jnp.where(valid, assign // TOP_K, 0), zeros]).reshape(n_blocks + 2, MOE_BLOCK)
    return gates, src_row, dst_row, blk_exp, n_used, n_blocks


def _moe_body(bexp_ref, nused_ref, src_ref, dst_ref, x_hbm, wg_ref, wu_ref, wd_ref, o_hbm,
              xf_ref, xb_ref, acc_ref, yb_ref, gsem, ssem, *, n_blocks, n_real):
    j = pl.program_id(0)
    slot = lax.rem(j, 2)
    nslot = 1 - slot
    active = j < nused_ref[0]

    def gather_start(blk, n, s):
        tok = src_ref[blk + 1, n]
        pltpu.make_async_copy(x_hbm.at[pl.ds(tok, 1), :], xf_ref.at[s, pl.ds(n, 1), :], gsem.at[s]).start()

    def scatter_start(blk, n, s):
        dst = dst_ref[blk + 1, n]
        pltpu.make_async_copy(yb_ref.at[s, pl.ds(n, 1), :], o_hbm.at[pl.ds(dst, 1), :], ssem.at[s]).start()

    def gather_wait(s):
        pltpu.make_async_copy(x_hbm.at[pl.ds(0, MOE_BLOCK), :], xf_ref.at[s], gsem.at[s]).wait()

    def scatter_wait(s):
        pltpu.make_async_copy(yb_ref.at[s], o_hbm.at[pl.ds(0, MOE_BLOCK), :], ssem.at[s]).wait()

    @pl.when(j == 0)
    def _():
        yb_ref[...] = jnp.zeros_like(yb_ref)
        for t in range(MOE_TRASH_BLOCKS):
            fill = pltpu.make_async_copy(
                yb_ref.at[0], o_hbm.at[pl.ds(n_real + t * MOE_BLOCK, MOE_BLOCK), :], ssem.at[0])
            fill.start()
            fill.wait()

        def issue(n, carry):
            gather_start(0, n, 0)
            return carry

        lax.fori_loop(0, MOE_BLOCK, issue, 0)

    gather_wait(slot)

    @pl.when(j >= 1)
    def _():
        scatter_wait(slot)

    @pl.when(active)
    def _():
        xb_ref[...] = xf_ref[slot].astype(BF16)

    per = -(-MOE_BLOCK // MOE_NCHUNK)
    for c in range(MOE_NCHUNK):
        @pl.when(j + c < nused_ref[0] + c)
        def _(c=c):
            for n in range(c * per, min((c + 1) * per, MOE_BLOCK)):
                gather_start(j + 1, n, nslot)
                scatter_start(j - 1, n, nslot)
            cs = slice(MOE_FF_BOUNDS[c], MOE_FF_BOUNDS[c + 1])
            xb = xb_ref[...]
            gt = jnp.dot(xb, wg_ref[:, cs], preferred_element_type=F32)
            up = jnp.dot(xb, wu_ref[:, cs], preferred_element_type=F32)
            hh = (gt * jax.nn.sigmoid(gt) * up).astype(BF16)
            part = jnp.dot(hh, wd_ref[cs, :], preferred_element_type=F32)
            if c == 0:
                acc_ref[...] = part
            elif c < MOE_NCHUNK - 1:
                acc_ref[...] += part
            else:
                yb_ref[slot] = acc_ref[...] + part

    @pl.when(jnp.logical_not(active))
    def _():
        def issue(n, carry):
            gather_start(j + 1, n, nslot)
            scatter_start(j - 1, n, nslot)
            return carry

        lax.fori_loop(0, MOE_BLOCK, issue, 0)

    @pl.when(j == n_blocks - 1)
    def _():
        def issue(n, carry):
            scatter_start(j, n, slot)
            return carry

        lax.fori_loop(0, MOE_BLOCK, issue, 0)
        scatter_wait(nslot)
        scatter_wait(slot)
        gather_wait(nslot)


def _moe(x2d, wg, wu, wd, src_row, dst_row, blk_exp, n_used, n_blocks):
    n_tok, dim = x2d.shape
    last = jnp.maximum(n_used - 1, 0)
    jj = jnp.arange(n_blocks, dtype=jnp.int32)
    bexp_eff = blk_exp[jnp.minimum(jj, last)]
    n_real = TOP_K * n_tok
    out_rows = n_real + MOE_TRASH_BLOCKS * MOE_BLOCK

    def wmap(j, bexp, nused, src, dst):
        return (bexp[j], 0, 0)

    grid_spec = pltpu.PrefetchScalarGridSpec(
        num_scalar_prefetch=4,
        grid=(n_blocks,),
        in_specs=[pl.BlockSpec(memory_space=pl.ANY),
                  pl.BlockSpec((None, dim, D_FF), wmap),
                  pl.BlockSpec((None, dim, D_FF), wmap),
                  pl.BlockSpec((None, D_FF, dim), wmap)],
        out_specs=pl.BlockSpec(memory_space=pl.ANY),
        scratch_shapes=[pltpu.VMEM((2, MOE_BLOCK, dim), F32),
                        pltpu.VMEM((MOE_BLOCK, dim), BF16),
                        pltpu.VMEM((MOE_BLOCK, dim), F32),
                        pltpu.VMEM((2, MOE_BLOCK, dim), F32),
                        pltpu.SemaphoreType.DMA((2,)),
                        pltpu.SemaphoreType.DMA((2,))],
    )
    body = functools.partial(_moe_body, n_blocks=n_blocks, n_real=n_real)
    return pl.pallas_call(
        body,
        grid_spec=grid_spec,
        out_shape=jax.ShapeDtypeStruct((out_rows, dim), F32),
        compiler_params=pltpu.CompilerParams(dimension_semantics=("arbitrary",),
                                             vmem_limit_bytes=60 * 1024 * 1024),
        name="moe_swiglu",
    )(bexp_eff, n_used.reshape(1), src_row, dst_row, x2d, wg, wu, wd)


def _combine_body(x_ref, gp_ref, y0_ref, y1_ref, g_ref, b_ref, o_ref):
    gp = gp_ref[...]
    moe = gp[:, 0:1] * y0_ref[...] + gp[:, 1:2] * y1_ref[...]
    y = ALPHA * x_ref[...] + moe
    o_ref[...] = _layernorm(y, g_ref[...], b_ref[...])


def _combine(x2d, gates_pad, yrows, g, b):
    n_tok, dim = x2d.shape
    tm = 512
    nt = n_tok // tm
    return pl.pallas_call(
        _combine_body,
        grid=(nt,),
        in_specs=[pl.BlockSpec((tm, dim), lambda i: (i, 0)),
                  pl.BlockSpec((tm, 128), lambda i: (i, 0)),
                  pl.BlockSpec((tm, dim), lambda i: (i, 0)),
                  pl.BlockSpec((tm, dim), lambda i: (nt + i, 0)),
                  _const_spec(g.shape), _const_spec(b.shape)],
        out_specs=pl.BlockSpec((tm, dim), lambda i: (i, 0)),
        out_shape=jax.ShapeDtypeStruct((n_tok, dim), F32),
        compiler_params=_cparams(("parallel",)),
        name="moe_combine_ln",
    )(x2d, gates_pad, yrows, yrows, g, b)


def _row(v):
    return v.astype(F32).reshape(1, -1)


def _even_layer(x, w_in, a_re, a_im, log_dt, b_re, b_im, c_re, c_im, d_skip, glu_w, glu_b,
                w_out, ln1_g, ln1_b, w_gate, w_up, w_down, ln2_g, ln2_b):
    bsz, seq, dim = x.shape
    h = _inproj(x, w_in.astype(BF16))
    wb, wc, lam = _s5_discretize(a_re, a_im, log_dt, b_re, b_im, c_re, c_im, bsz)
    ya = _s5(h, wb, wc, lam, _row(d_skip), glu_w.astype(BF16), _row(glu_b))
    yb = _retention(h, 1)
    w_out = w_out.astype(BF16)
    x1 = _outln_even(x, ya, yb, w_out[:S5_CHANNELS], w_out[S5_CHANNELS:], _row(ln1_g), _row(ln1_b))
    x2 = _ffn(x1.reshape(bsz * seq, dim), w_gate.astype(BF16), w_up.astype(BF16), w_down.astype(BF16),
              _row(ln2_g), _row(ln2_b))
    return x2.reshape(bsz, seq, dim)


def _odd_layer(x, w_qkv, w_out, ln1_g, ln1_b, router_w, router_b, w_gate, w_up, w_down, ln2_g, ln2_b):
    bsz, seq, dim = x.shape
    n_tok = bsz * seq
    qkv = _qkv(x, w_qkv.astype(BF16))
    att = _dilated_attention(qkv)
    rw = jnp.zeros((dim, 128), F32).at[:, :N_EXPERTS].set(router_w.astype(F32))
    rb = jnp.zeros((1, 128), F32).at[:, :N_EXPERTS].set(router_b.astype(F32)[None, :])
    x1, logits = _outln_odd(x, att, w_out.astype(BF16), _row(ln1_g), _row(ln1_b), rw, rb)
    x1 = x1.reshape(n_tok, dim)
    gates, src_row, dst_row, blk_exp, n_used, n_blocks = _route(logits.reshape(n_tok, 128)[:, :N_EXPERTS], n_tok)
    yrows = _moe(x1, w_gate.astype(BF16), w_up.astype(BF16), w_down.astype(BF16),
                 src_row, dst_row, blk_exp, n_used, n_blocks)
    gates_pad = jnp.zeros((n_tok, 128), F32).at[:, :TOP_K].set(gates)
    out = _combine(x1, gates_pad, yrows, _row(ln2_g), _row(ln2_b))
    return out.reshape(bsz, seq, dim)


def kernel(x, ev_w_in, ev_s5_a_re, ev_s5_a_im, ev_s5_log_dt, ev_s5_b_re, ev_s5_b_im, ev_s5_c_re, ev_s5_c_im, ev_s5_d, ev_s5_glu_w, ev_s5_glu_b, ev_w_out, ev_ln1_g, ev_ln1_b, ev_ffn_w_gate, ev_ffn_w_up, ev_ffn_w_down, ev_ln2_g, ev_ln2_b, od_w_qkv, od_w_out, od_ln1_g, od_ln1_b, od_router_w, od_router_b, od_moe_w_gate, od_moe_w_up, od_moe_w_down, od_ln2_g, od_ln2_b):
    for layer in range(DEPTH):
        i = layer // 2
        if layer % 2 == 0:
            x = _even_layer(x, ev_w_in[i], ev_s5_a_re[i], ev_s5_a_im[i], ev_s5_log_dt[i], ev_s5_b_re[i],
                            ev_s5_b_im[i], ev_s5_c_re[i], ev_s5_c_im[i], ev_s5_d[i], ev_s5_glu_w[i],
                            ev_s5_glu_b[i], ev_w_out[i], ev_ln1_g[i], ev_ln1_b[i], ev_ffn_w_gate[i],
                            ev_ffn_w_up[i], ev_ffn_w_down[i], ev_ln2_g[i], ev_ln2_b[i])
        else:
            x = _odd_layer(x, od_w_qkv[i], od_w_out[i], od_ln1_g[i], od_ln1_b[i], od_router_w[i],
                           od_router_b[i], od_moe_w_gate[i], od_moe_w_up[i], od_moe_w_down[i],
                           od_ln2_g[i], od_ln2_b[i])
    return x
```

```python
import functools
import math

import numpy as np
import jax
import jax.numpy as jnp
from jax import lax
from jax.experimental import pallas as pl
from jax.experimental.pallas import tpu as pltpu

F32 = jnp.float32
BF16 = jnp.bfloat16

LN_EPS = 1e-5
DEPTH = 2
ALPHA = (2.0 * DEPTH) ** 0.25

S5_CHANNELS = 512
S5_GROUPS = 32
S5_GROUP = 16
S5_STATE = 64
S5_NSTATE = S5_GROUPS * S5_STATE

RET_HEADS = 4
RET_HEAD_DIM = 128
RET_CHUNK = 128

ATT_HEADS = 16
ATT_HEAD_DIM = 64
ATT_PAIRS = ATT_HEADS // 2
ATT_BLOCK = 128
ATT_SPAN = 128
ATT_DILATIONS = (1, 4, 16)
NEG_BIG = -1e30

D_FF = 2816
N_EXPERTS = 8
TOP_K = 2
MOE_BLOCK = 512

VMEM_LIMIT = 56 * 1024 * 1024


def _cparams(sem):
    return pltpu.CompilerParams(dimension_semantics=sem, vmem_limit_bytes=VMEM_LIMIT)


def _layernorm(y, g, b):
    mu = jnp.mean(y, -1, keepdims=True)
    yc = y - mu
    var = jnp.mean(yc * yc, -1, keepdims=True)
    return yc * lax.rsqrt(var + LN_EPS) * g + b


def _const_spec(shape):
    nd = len(shape)
    return pl.BlockSpec(shape, lambda *_: (0,) * nd)


def _inproj_body(x_ref, w_ref, h_ref):
    xb = x_ref[...].astype(BF16)
    h_ref[...] = jnp.dot(xb, w_ref[...], preferred_element_type=F32).astype(BF16)


def _inproj(x, w):
    bsz, seq, dim = x.shape
    tm = 512
    nh = w.shape[1]
    return pl.pallas_call(
        _inproj_body,
        grid=(bsz, seq // tm),
        in_specs=[pl.BlockSpec((None, tm, dim), lambda b, s: (b, s, 0)),
                  _const_spec(w.shape)],
        out_specs=pl.BlockSpec((None, tm, nh), lambda b, s: (b, s, 0)),
        out_shape=jax.ShapeDtypeStruct((bsz, seq, nh), BF16),
        compiler_params=_cparams(("parallel", "parallel")),
        name="inproj",
    )(x, w)


def _s5_discretize(a_re, a_im, log_dt, b_re, b_im, c_re, c_im, bsz):
    ar = a_re.astype(F32)
    ai = a_im.astype(F32)
    dt = jnp.exp(log_dt.astype(F32))[:, None]
    mag = jnp.exp(ar * dt)
    lr = mag * jnp.cos(ai * dt)
    li = mag * jnp.sin(ai * dt)
    den = ar * ar + ai * ai
    zr = ((lr - 1.0) * ar + li * ai) / den
    zi = (li * ar - (lr - 1.0) * ai) / den
    br = b_re.astype(F32)
    bi = b_im.astype(F32)
    bbr = zr[..., None] * br - zi[..., None] * bi
    bbi = zr[..., None] * bi + zi[..., None] * br
    eye = jnp.eye(S5_GROUPS, dtype=F32)
    nch = S5_CHANNELS
    wbr = jnp.einsum('gpc,gh->gchp', bbr, eye).reshape(nch, S5_NSTATE)
    wbi = jnp.einsum('gpc,gh->gchp', bbi, eye).reshape(nch, S5_NSTATE)
    wb = jnp.concatenate([wbr, wbi], axis=1).astype(BF16)
    wcr = jnp.einsum('gcp,gh->hpgc', c_re.astype(F32), eye).reshape(S5_NSTATE, nch)
    wci = jnp.einsum('gcp,gh->hpgc', c_im.astype(F32), eye).reshape(S5_NSTATE, nch)
    wc = jnp.concatenate([wcr, -wci], axis=0).astype(BF16)
    lam = jnp.stack([jnp.broadcast_to(lr.reshape(1, S5_NSTATE), (bsz, S5_NSTATE)),
                     jnp.broadcast_to(li.reshape(1, S5_NSTATE), (bsz, S5_NSTATE))])
    return wb, wc, lam


def _gelu_tanh(x):
    return 0.5 * x * (1.0 + jnp.tanh(math.sqrt(2.0 / math.pi) * (x + 0.044715 * (x * x * x))))


def _s5_body(u_ref, wb_ref, wc_ref, lam_ref, d_ref, gw_ref, gb_ref, o_ref, xs_ref, st_ref, utb_ref, ys_ref,
             *, lc, bsz, ncb):
    @pl.when(pl.program_id(0) == 0)
    def _():
        st_ref[...] = jnp.zeros_like(st_ref)

    nlb = S5_CHANNELS // 128
    for b in range(bsz):
        ub = u_ref[b].astype(F32)
        for c in range(nlb):
            utb_ref[c, pl.ds(b, lc, stride=bsz), :] = ub[:, c * 128:(c + 1) * 128]
    uf = jnp.concatenate([utb_ref[c] for c in range(nlb)], axis=-1)
    ub = uf.astype(BF16)
    half_c = S5_CHANNELS // 2
    half_s = S5_NSTATE // 2
    for q in range(4):
        hq = q % 2
        xs_ref[:, q * half_s:(q + 1) * half_s] = jnp.dot(
            ub[:, hq * half_c:(hq + 1) * half_c],
            wb_ref[hq * half_c:(hq + 1) * half_c, q * half_s:(q + 1) * half_s],
            preferred_element_type=F32)
    width = S5_NSTATE // ncb
    for cb in range(ncb):
        re = slice(cb * width, (cb + 1) * width)
        im = slice(S5_NSTATE + cb * width, S5_NSTATE + (cb + 1) * width)
        lr = lam_ref[0, :, re]
        li = lam_ref[1, :, re]

        def step(t, carry, re=re, im=im, lr=lr, li=li):
            xr, xi = carry
            row = pl.multiple_of(t * bsz, bsz)
            nxr = lr * xr - li * xi + xs_ref[pl.ds(row, bsz), re]
            nxi = lr * xi + li * xr + xs_ref[pl.ds(row, bsz), im]
            xs_ref[pl.ds(row, bsz), re] = nxr
            xs_ref[pl.ds(row, bsz), im] = nxi
            return nxr, nxi

        xr, xi = lax.fori_loop(0, lc, step, (st_ref[:, re], st_ref[:, im]), unroll=8)
        st_ref[:, re] = xr
        st_ref[:, im] = xi

    ys = []
    for hq in range(2):
        cols = slice(hq * half_c, (hq + 1) * half_c)
        part = None
        for ri in range(2):
            rows = slice((2 * ri + hq) * half_s, (2 * ri + hq + 1) * half_s)
            t = jnp.dot(xs_ref[:, rows].astype(BF16), wc_ref[rows, cols], preferred_element_type=F32)
            part = t if part is None else part + t
        ys.append(part)
    y = jnp.concatenate(ys, axis=-1)
    y = y + d_ref[...] * uf
    y = _gelu_tanh(y)
    z = jnp.dot(y.astype(BF16), gw_ref[...], preferred_element_type=F32) + gb_ref[...]
    yo = y * jax.nn.sigmoid(z)
    for c in range(nlb):
        ys_ref[c] = yo[:, c * 128:(c + 1) * 128]
    for b in range(bsz):
        o_ref[b] = jnp.concatenate([ys_ref[c, pl.ds(b, lc, stride=bsz), :] for c in range(nlb)],
                                   axis=-1).astype(BF16)


def _s5(h, wb, wc, lam, d_skip, glu_w, glu_b):
    bsz, seq, _ = h.shape
    lc = 64
    blk = lc * bsz
    body = functools.partial(_s5_body, lc=lc, bsz=bsz, ncb=4)
    return pl.pallas_call(
        body,
        grid=(seq // lc,),
        in_specs=[pl.BlockSpec((bsz, lc, S5_CHANNELS), lambda s: (0, s, 0)),
                  _const_spec(wb.shape), _const_spec(wc.shape), _const_spec(lam.shape),
                  _const_spec(d_skip.shape), _const_spec(glu_w.shape), _const_spec(glu_b.shape)],
        out_specs=pl.BlockSpec((bsz, lc, S5_CHANNELS), lambda s: (0, s, 0)),
        out_shape=jax.ShapeDtypeStruct((bsz, seq, S5_CHANNELS), BF16),
        scratch_shapes=[pltpu.VMEM((blk, 2 * S5_NSTATE), F32),
                        pltpu.VMEM((bsz, 2 * S5_NSTATE), F32),
                        pltpu.VMEM((S5_CHANNELS // 128, blk, 128), F32),
                        pltpu.VMEM((S5_CHANNELS // 128, blk, 128), F32)],
        compiler_params=_cparams(("arbitrary",)),
        name="s5_mixer",
    )(h, wb, wc, lam, d_skip, glu_w, glu_b)


def _ret_constants():
    hh = np.arange(RET_HEADS, dtype=np.float64)
    log_gamma = np.log1p(-np.exp2(-5.0 - hh))
    pos = np.arange(RET_CHUNK, dtype=np.float64)
    diff = pos[:, None] - pos[None, :]
    scale = RET_HEAD_DIM ** -0.5
    decay = np.where(diff >= 0, np.exp(log_gamma[:, None, None] * np.maximum(diff, 0.0)), 0.0) * scale
    zeta = np.exp(log_gamma[:, None] * (RET_CHUNK - 1.0 - pos)) * scale
    xi = np.exp(log_gamma[:, None] * (pos + 1.0))
    chunk_decay = np.exp(log_gamma * RET_CHUNK)
    zeta_b = np.broadcast_to(zeta[:, :, None], (RET_HEADS, RET_CHUNK, RET_HEAD_DIM))
    xi_b = np.broadcast_to(xi[:, :, None], (RET_HEADS, RET_CHUNK, RET_HEAD_DIM))
    return (decay.astype(np.float32), np.ascontiguousarray(zeta_b).astype(np.float32),
            np.ascontiguousarray(xi_b).astype(np.float32), [float(c) for c in chunk_decay])


def _ret_body(q_ref, k_ref, v_ref, g_ref, dec_ref, zeta_ref, xi_ref, o_ref, st_ref, *, ts, chunk_decay):
    @pl.when(pl.program_id(1) == 0)
    def _():
        st_ref[...] = jnp.zeros_like(st_ref)

    for c in range(ts // RET_CHUNK):
        rows = slice(c * RET_CHUNK, (c + 1) * RET_CHUNK)
        for h in range(RET_HEADS):
            cols = slice(h * RET_HEAD_DIM, (h + 1) * RET_HEAD_DIM)
            q = q_ref[rows, cols]
            k = k_ref[rows, cols]
            v = v_ref[rows, cols]
            s = lax.dot_general(q, k, (((1,), (1,)), ((), ())), preferred_element_type=F32) * dec_ref[h]
            inner = jnp.dot(s.astype(BF16), v, preferred_element_type=F32)
            st = st_ref[h]
            cross = jnp.dot(q, st.astype(BF16), preferred_element_type=F32) * xi_ref[h]
            ret = inner + cross
            mu = jnp.mean(ret, -1, keepdims=True)
            rc = ret - mu
            var = jnp.mean(rc * rc, -1, keepdims=True)
            rn = rc * lax.rsqrt(var + LN_EPS)
            gg = g_ref[rows, cols].astype(F32)
            o_ref[rows, cols] = (gg * jax.nn.sigmoid(gg) * rn).astype(BF16)
            kz = (k.astype(F32) * zeta_ref[h]).astype(BF16)
            kv = lax.dot_general(kz, v, (((0,), (0,)), ((), ())), preferred_element_type=F32)
            st_ref[h] = st * chunk_decay[h] + kv


def _retention(h, first_block):
    bsz, seq, _ = h.shape
    ts = 512
    width = RET_HEADS * RET_HEAD_DIM
    decay, zeta_b, xi_b, chunk_decay = _ret_constants()
    body = functools.partial(_ret_body, ts=ts, chunk_decay=chunk_decay)

    def col(j):
        return pl.BlockSpec((None, ts, width), lambda b, s, j=j: (b, s, j))

    return pl.pallas_call(
        body,
        grid=(bsz, seq // ts),
        in_specs=[col(first_block), col(first_block + 1), col(first_block + 2), col(first_block + 3),
                  _const_spec(decay.shape), _const_spec(zeta_b.shape), _const_spec(xi_b.shape)],
        out_specs=pl.BlockSpec((None, ts, width), lambda b, s: (b, s, 0)),
        out_shape=jax.ShapeDtypeStruct((bsz, seq, width), BF16),
        scratch_shapes=[pltpu.VMEM((RET_HEADS, RET_HEAD_DIM, RET_HEAD_DIM), F32)],
        compiler_params=_cparams(("parallel", "arbitrary")),
        name="retention",
    )(h, h, h, h, jnp.asarray(decay), jnp.asarray(zeta_b), jnp.asarray(xi_b))


def _outln_even_body(x_ref, ya_ref, yb_ref, w1_ref, w2_ref, g_ref, b_ref, o_ref):
    acc = jnp.dot(ya_ref[...], w1_ref[...], preferred_element_type=F32)
    acc = acc + jnp.dot(yb_ref[...], w2_ref[...], preferred_element_type=F32)
    y = ALPHA * x_ref[...] + acc
    o_ref[...] = _layernorm(y, g_ref[...], b_ref[...])


def _outln_even(x, ya, yb, w1, w2, g, b):
    bsz, seq, dim = x.shape
    tm = 512
    wa = ya.shape[2]
    wbw = yb.shape[2]
    return pl.pallas_call(
        _outln_even_body,
        grid=(bsz, seq // tm),
        in_specs=[pl.BlockSpec((None, tm, dim), lambda b_, s: (b_, s, 0)),
                  pl.BlockSpec((None, tm, wa), lambda b_, s: (b_, s, 0)),
                  pl.BlockSpec((None, tm, wbw), lambda b_, s: (b_, s, 0)),
                  _const_spec(w1.shape), _const_spec(w2.shape),
                  _const_spec(g.shape), _const_spec(b.shape)],
        out_specs=pl.BlockSpec((None, tm, dim), lambda b_, s: (b_, s, 0)),
        out_shape=jax.ShapeDtypeStruct((bsz, seq, dim), F32),
        compiler_params=_cparams(("parallel", "parallel")),
        name="outproj_ln_even",
    )(x, ya, yb, w1, w2, g, b)


FFN_BOUNDS = (0, 1536, 2816)


def _ffn_body(x_ref, wg_ref, wu_ref, wd_ref, g_ref, b_ref, o_ref):
    x = x_ref[...]
    xb = x.astype(BF16)
    acc = None
    for c in range(len(FFN_BOUNDS) - 1):
        cs = slice(FFN_BOUNDS[c], FFN_BOUNDS[c + 1])
        gt = jnp.dot(xb, wg_ref[:, cs], preferred_element_type=F32)
        up = jnp.dot(xb, wu_ref[:, cs], preferred_element_type=F32)
        hh = (gt * jax.nn.sigmoid(gt) * up).astype(BF16)
        part = jnp.dot(hh, wd_ref[cs, :], preferred_element_type=F32)
        acc = part if acc is None else acc + part
    y = ALPHA * x + acc
    o_ref[...] = _layernorm(y, g_ref[...], b_ref[...])


def _ffn(x2d, wg, wu, wd, g, b):
    n_tok, dim = x2d.shape
    tm = 512
    single = pl.Buffered(1)

    def wspec(shape):
        return pl.BlockSpec(shape, lambda i: (0, 0), pipeline_mode=single)

    return pl.pallas_call(
        _ffn_body,
        grid=(n_tok // tm,),
        in_specs=[pl.BlockSpec((tm, dim), lambda i: (i, 0)),
                  wspec(wg.shape), wspec(wu.shape), wspec(wd.shape),
                  _const_spec(g.shape), _const_spec(b.shape)],
        out_specs=pl.BlockSpec((tm, dim), lambda i: (i, 0)),
        out_shape=jax.ShapeDtypeStruct((n_tok, dim), F32),
        compiler_params=_cparams(("parallel",)),
        name="ffn_swiglu_ln",
    )(x2d, wg, wu, wd, g, b)


def _qkv_body(x_ref, w_ref, o_ref):
    xb = x_ref[...].astype(BF16)
    r = jnp.dot(xb, w_ref[...], preferred_element_type=F32)
    nblk = o_ref.shape[0]
    for j in range(nblk):
        blk = r[:, j * 128:(j + 1) * 128]
        if j < ATT_PAIRS:
            blk = blk * (ATT_HEAD_DIM ** -0.5)
        o_ref[j] = blk.astype(BF16)


def _qkv(x, w):
    bsz, seq, dim = x.shape
    tm = 512
    nblk = w.shape[1] // 128
    return pl.pallas_call(
        _qkv_body,
        grid=(bsz, seq // tm),
        in_specs=[pl.BlockSpec((None, tm, dim), lambda b, s: (b, s, 0)),
                  _const_spec(w.shape)],
        out_specs=pl.BlockSpec((None, nblk, tm, 128), lambda b, s: (b, 0, s, 0)),
        out_shape=jax.ShapeDtypeStruct((bsz, nblk, seq, 128), BF16),
        compiler_params=_cparams(("parallel", "parallel")),
        name="qkv_proj",
    )(x, w)


ATT_GROUP = 4


def _att_constants():
    slopes = np.exp2(-8.0 * np.arange(1, ATT_HEADS + 1, dtype=np.float64) / ATT_HEADS)
    sl = np.zeros((ATT_PAIRS, 2 * ATT_BLOCK, 2 * ATT_BLOCK), np.float32)
    for hp in range(ATT_PAIRS):
        sl[hp, :ATT_BLOCK, :] = slopes[2 * hp]
        sl[hp, ATT_BLOCK:, :] = slopes[2 * hp + 1]
    return sl


def _att_unit(q, kw, vw, bias, old, low, ones, last):
    zero = jnp.zeros_like(q)
    qs = jnp.concatenate([jnp.where(low, q, zero), jnp.where(low, zero, q)], axis=0)
    s = lax.dot_general(qs, kw, (((1,), (1,)), ((), ())), preferred_element_type=F32) + bias
    sa = s[:, :128]
    sb = s[:, 128:]
    mrow = jnp.max(jnp.maximum(sa, sb), axis=-1, keepdims=True)
    m_new = jnp.broadcast_to(mrow, (2 * ATT_BLOCK, 128))
    if old is not None:
        m_old, acc_old, l0_old, l1_old = old
        m_new = jnp.maximum(m_old, m_new)
    p = jnp.concatenate([jnp.exp(sa - m_new), jnp.exp(sb - m_new)], axis=1).astype(BF16)
    vaug = jnp.concatenate([vw, ones], axis=1)
    res = jnp.dot(p, vaug, preferred_element_type=F32)
    acc = jnp.where(low, res[:ATT_BLOCK, :128], res[ATT_BLOCK:, :128])
    l0 = res[:ATT_BLOCK, 128:]
    l1 = res[ATT_BLOCK:, 128:]
    if old is not None:
        a = jnp.exp(m_old - m_new)
        a0 = a[:ATT_BLOCK]
        a1 = a[ATT_BLOCK:]
        acc = jnp.where(low, a0, a1) * acc_old + acc
        l0 = a0 * l0_old + l0
        l1 = a1 * l1_old + l1
    if last:
        acc = acc / jnp.where(low, l0, l1)
    return m_new, acc, l0, l1


def _att_body(q_ref, k_ref, v_ref, sl_ref, o_ref,
              tmp_ref, q4_ref, k4_ref, v4_ref, nat_ref, de4_ref, bias_ref, *, seq):
    quarter = seq // 4
    for src, dst in ((q_ref, q4_ref), (k_ref, k4_ref), (v_ref, v4_ref)):
        tmp_ref[...] = src[...].astype(F32)
        for r in range(4):
            dst[r] = tmp_ref[pl.ds(r, quarter, stride=4), :]
    lane = lax.broadcasted_iota(jnp.int32, (ATT_BLOCK, 128), 1)
    low = lane < ATT_HEAD_DIM
    ones = jnp.ones((2 * ATT_BLOCK, 128), BF16)

    qi = lax.broadcasted_iota(jnp.int32, (2 * ATT_BLOCK, 2 * ATT_BLOCK), 0) & (ATT_BLOCK - 1)
    ki = lax.broadcasted_iota(jnp.int32, (2 * ATT_BLOCK, 2 * ATT_BLOCK), 1)
    sl = sl_ref[...]
    for bi, d in enumerate(ATT_DILATIONS):
        for oi in range(2):
            steps = (qi - ki + oi * ATT_BLOCK).astype(F32)
            allowed = (steps >= 0.0) & (steps <= float(ATT_SPAN))
            bias_ref[2 * bi + oi] = jnp.where(allowed, -(sl * float(d)) * steps, NEG_BIG)

    n_units = seq // ATT_BLOCK
    for bi, d in enumerate(ATT_DILATIONS):
        first = bi == 0
        last = bi == len(ATT_DILATIONS) - 1
        shift = d.bit_length() - 1

        def group(it, carry, bi=bi, d=d, shift=shift, first=first, last=last):
            loaded = []
            for g in range(ATT_GROUP):
                n = it * ATT_GROUP + g
                i = n >> shift
                r = n & (d - 1)
                iw = jnp.maximum(i - 1, 0)
                if d == 1:
                    qsl = pl.ds(pl.multiple_of(i * ATT_BLOCK, ATT_BLOCK), ATT_BLOCK)
                    wsl = pl.ds(pl.multiple_of(iw * ATT_BLOCK, ATT_BLOCK), 2 * ATT_BLOCK)
                    data = (q_ref[qsl, :], k_ref[wsl, :], v_ref[wsl, :])
                    old_at = None
                    new_at = lambda s, qsl=qsl: nat_ref.at[s, qsl, :]
                else:
                    if d == 4:
                        r4 = r
                        qsl = pl.ds(pl.multiple_of(i * ATT_BLOCK, ATT_BLOCK), ATT_BLOCK)
                        wsl = pl.ds(pl.multiple_of(iw * ATT_BLOCK, ATT_BLOCK), 2 * ATT_BLOCK)
                        osl = pl.ds(r + (4 * ATT_BLOCK) * i, ATT_BLOCK, stride=4)
                        old_at = lambda s, osl=osl: nat_ref.at[s, osl, :]
                    else:
                        r4 = r & 3
                        qsl = pl.ds((r >> 2) + (4 * ATT_BLOCK) * i, ATT_BLOCK, stride=4)
                        wsl = pl.ds((r >> 2) + (4 * ATT_BLOCK) * iw, 2 * ATT_BLOCK, stride=4)
                        old_at = lambda s, r4=r4, qsl=qsl: de4_ref.at[s, r4, qsl, :]
                    data = (q4_ref[r4, qsl, :].astype(BF16), k4_ref[r4, wsl, :].astype(BF16),
                            v4_ref[r4, wsl, :].astype(BF16))
                    new_at = lambda s, r4=r4, qsl=qsl: de4_ref.at[s, r4, qsl, :]
                old = None
                if old_at is not None:
                    old = (jnp.concatenate([old_at(1)[...], old_at(2)[...]], axis=0),
                           old_at(0)[...], old_at(3)[...], old_at(4)[...])
                loaded.append((new_at, data, bias_ref[2 * bi + jnp.minimum(i, 1)], old))
            results = [_att_unit(q, kw, vw, bias, old, low, ones, last)
                       for (_, (q, kw, vw), bias, old) in loaded]
            for (new_at, *_), (m_new, acc, l0, l1) in zip(loaded, results):
                new_at(0)[...] = acc
                if not last:
                    new_at(1)[...] = m_new[:ATT_BLOCK]
                    new_at(2)[...] = m_new[ATT_BLOCK:]
                    new_at(3)[...] = l0
                    new_at(4)[...] = l1
            return carry

        lax.fori_loop(0, n_units // ATT_GROUP, group, 0)

    for r in range(4):
        nat_ref[0, pl.ds(r, quarter, stride=4), :] = de4_ref[0, r]
    o_ref[...] = nat_ref[0].astype(BF16)


def _dilated_attention(qkv):
    bsz, nblk, seq, _ = qkv.shape
    sl = jnp.asarray(_att_constants())
    specs = [pl.BlockSpec((None, None, seq, 128), lambda b, hp, j=j: (b, j * ATT_PAIRS + hp, 0, 0))
             for j in range(3)]
    specs.append(pl.BlockSpec((None,) + sl.shape[1:], lambda b, hp: (hp, 0, 0)))
    body = functools.partial(_att_body, seq=seq)
    return pl.pallas_call(
        body,
        grid=(bsz, ATT_PAIRS),
        in_specs=specs,
        out_specs=pl.BlockSpec((None, None, seq, 128), lambda b, hp: (b, hp, 0, 0)),
        out_shape=jax.ShapeDtypeStruct((bsz, ATT_PAIRS, seq, 128), BF16),
        scratch_shapes=[pltpu.VMEM((seq, 128), F32)]
        + [pltpu.VMEM((4, seq // 4, 128), F32) for _ in range(3)]
        + [pltpu.VMEM((5, seq, 128), F32), pltpu.VMEM((5, 4, seq // 4, 128), F32),
           pltpu.VMEM((2 * len(ATT_DILATIONS), 2 * ATT_BLOCK, 2 * ATT_BLOCK), F32)],
        compiler_params=_cparams(("parallel", "parallel")),
        name="dilated_attention",
    )(qkv, qkv, qkv, sl)


def _outln_odd_body(x_ref, a_ref, w_ref, g_ref, b_ref, rwh_ref, rwl_ref, rb_ref, o_ref, lg_ref):
    a = jnp.concatenate([a_ref[j] for j in range(a_ref.shape[0])], axis=-1)
    acc = jnp.dot(a, w_ref[...], preferred_element_type=F32)
    y = ALPHA * x_ref[...] + acc
    o = _layernorm(y, g_ref[...], b_ref[...])
    o_ref[...] = o
    oh = o.astype(BF16)
    ol = (o - oh.astype(F32)).astype(BF16)
    lg = jnp.dot(oh, rwh_ref[...], preferred_element_type=F32)
    lg = lg + jnp.dot(ol, rwh_ref[...], preferred_element_type=F32)
    lg = lg + jnp.dot(oh, rwl_ref[...], preferred_element_type=F32)
    lg_ref[...] = lg + rb_ref[...]


def _outln_odd(x, att, w, g, b, rw, rb):
    bsz, seq, dim = x.shape
    tm = 512
    npair = att.shape[1]
    rwh = rw.astype(BF16)
    rwl = (rw - rwh.astype(F32)).astype(BF16)
    return pl.pallas_call(
        _outln_odd_body,
        grid=(bsz, seq // tm),
        in_specs=[pl.BlockSpec((None, tm, dim), lambda b_, s: (b_, s, 0)),
                  pl.BlockSpec((None, npair, tm, 128), lambda b_, s: (b_, 0, s, 0)),
                  _const_spec(w.shape), _const_spec(g.shape), _const_spec(b.shape),
                  _const_spec(rw.shape), _const_spec(rw.shape), _const_spec(rb.shape)],
        out_specs=[pl.BlockSpec((None, tm, dim), lambda b_, s: (b_, s, 0)),
                   pl.BlockSpec((None, tm, 128), lambda b_, s: (b_, s, 0))],
        out_shape=[jax.ShapeDtypeStruct((bsz, seq, dim), F32),
                   jax.ShapeDtypeStruct((bsz, seq, 128), F32)],
        compiler_params=_cparams(("parallel", "parallel")),
        name="outproj_ln_odd",
    )(x, att, w, g, b, rwh, rwl, rb)


MOE_FF_BOUNDS = (0, 1536, 2816)
MOE_NCHUNK = len(MOE_FF_BOUNDS) - 1
MOE_TRASH_BLOCKS = 2 + 2 * N_EXPERTS


def _route(logits, n_tok):
    top_val, top_idx = lax.top_k(logits, TOP_K)
    gates = jax.nn.softmax(top_val, axis=-1)
    n_assign = n_tok * TOP_K
    exp_flat = top_idx.reshape(n_assign).astype(jnp.int32)
    experts = jnp.arange(N_EXPERTS, dtype=jnp.int32)

    def lookup(table, idx):
        return jnp.sum(jnp.where(idx[:, None] == experts[None, :], table[None, :], 0), axis=1)

    counts = jnp.sum((exp_flat[:, None] == experts[None, :]).astype(jnp.int32), axis=0)
    starts = jnp.cumsum(counts) - counts
    padded_counts = (counts + MOE_BLOCK - 1) // MOE_BLOCK * MOE_BLOCK
    pends = jnp.cumsum(padded_counts)
    pstarts = pends - padded_counts
    order = jnp.argsort(exp_flat).astype(jnp.int32)
    n_blocks = -(-n_assign // MOE_BLOCK) + N_EXPERTS
    n_rows = n_blocks * MOE_BLOCK
    blk_exp = jnp.minimum(jnp.sum((pends[None, :] <= (jnp.arange(n_blocks, dtype=jnp.int32) * MOE_BLOCK)[:, None])
                                  .astype(jnp.int32), axis=1), N_EXPERTS - 1)
    rows = jnp.arange(n_rows, dtype=jnp.int32)
    row_exp = jnp.repeat(blk_exp, MOE_BLOCK)
    local = rows - lookup(pstarts, row_exp)
    valid = local < lookup(counts, row_exp)
    spos = jnp.clip(lookup(starts, row_exp) + local, 0, n_assign - 1)
    n_used = (pends[-1] // MOE_BLOCK).astype(jnp.int32)
    assign = jnp.where(valid, order[spos], -1)
    real_dst = (assign % TOP_K) * n_tok + assign // TOP_K
    in_blk = rows % MOE_BLOCK
    blk = rows // MOE_BLOCK
    trash_blk = jnp.where(blk < n_used, 1 + row_exp, 1 + N_EXPERTS + blk - n_used)
    dst_mid = jnp.where(valid, real_dst, n_assign + trash_blk * MOE_BLOCK + in_blk)
    lane_ids = jnp.arange(MOE_BLOCK, dtype=jnp.int32)
    dst_front = n_assign + lane_ids
    dst_back = n_assign + (1 + N_EXPERTS + n_blocks - n_used) * MOE_BLOCK + lane_ids
    dst_row = jnp.concatenate([dst_front, dst_mid, dst_back]).reshape(n_blocks + 2, MOE_BLOCK)
    zeros = jnp.zeros((MOE_BLOCK,), jnp.int32)
    src_row = jnp.concatenate([zeros, jnp.where(valid, assign // TOP_K, 0), zeros]).reshape(n_blocks + 2, MOE_BLOCK)
    return gates, src_row, dst_row, blk_exp, n_used, n_blocks


def _moe_body(bexp_ref, nused_ref, src_ref, dst_ref, x_hbm, wg_ref, wu_ref, wd_ref, o_hbm,
              xf_ref, xb_ref, yb_ref, gsem, ssem, *, n_blocks, n_real):
    j = pl.program_id(0)
    slot = lax.rem(j, 2)
    nslot = 1 - slot
    active = j < nused_ref[0]

    def gather_start(blk, n, s):
        tok = src_ref[blk + 1, n]
        pltpu.make_async_copy(x_hbm.at[pl.ds(tok, 1), :], xf_ref.at[s, pl.ds(n, 1), :], gsem.at[s]).start()

    def scatter_start(blk, n, s):
        dst = dst_ref[blk + 1, n]
        pltpu.make_async_copy(yb_ref.at[s, pl.ds(n, 1), :], o_hbm.at[pl.ds(dst, 1), :], ssem.at[s]).start()

    def gather_wait(s):
        pltpu.make_async_copy(x_hbm.at[pl.ds(0, MOE_BLOCK), :], xf_ref.at[s], gsem.at[s]).wait()

    def scatter_wait(s):
        pltpu.make_async_copy(yb_ref.at[s], o_hbm.at[pl.ds(0, MOE_BLOCK), :], ssem.at[s]).wait()

    @pl.when(j == 0)
    def _():
        yb_ref[...] = jnp.zeros_like(yb_ref)
        for t in range(MOE_TRASH_BLOCKS):
            fill = pltpu.make_async_copy(
                yb_ref.at[0], o_hbm.at[pl.ds(n_real + t * MOE_BLOCK, MOE_BLOCK), :], ssem.at[0])
            fill.start()
            fill.wait()

        def issue(n, carry):
            gather_start(0, n, 0)
            return carry

        lax.fori_loop(0, MOE_BLOCK, issue, 0)

    gather_wait(slot)

    @pl.when(j >= 1)
    def _():
        scatter_wait(slot)

    @pl.when(active)
    def _():
        xb_ref[...] = xf_ref[slot].astype(BF16)

    per = -(-MOE_BLOCK // MOE_NCHUNK)
    for c in range(MOE_NCHUNK):
        @pl.when(j + c < nused_ref[0] + c)
        def _(c=c):
            for n in range(c * per, min((c + 1) * per, MOE_BLOCK)):
                gather_start(j + 1, n, nslot)
                scatter_start(j - 1, n, nslot)
            cs = slice(MOE_FF_BOUNDS[c], MOE_FF_BOUNDS[c + 1])
            xb = xb_ref[...]
            gt = jnp.dot(xb, wg_ref[:, cs], preferred_element_type=F32)
            up = jnp.dot(xb, wu_ref[:, cs], preferred_element_type=F32)
            hh = (gt * jax.nn.sigmoid(gt) * up).astype(BF16)
            part = jnp.dot(hh, wd_ref[cs, :], preferred_element_type=F32)
            if c == 0:
                yb_ref[slot] = part
            else:
                yb_ref[slot] += part

    @pl.when(jnp.logical_not(active))
    def _():
        def issue(n, carry):
            gather_start(j + 1, n, nslot)
            scatter_start(j - 1, n, nslot)
            return carry

        lax.fori_loop(0, MOE_BLOCK, issue, 0)

    @pl.when(j == n_blocks - 1)
    def _():
        def issue(n, carry):
            scatter_start(j, n, slot)
            return carry

        lax.fori_loop(0, MOE_BLOCK, issue, 0)
        scatter_wait(nslot)
        scatter_wait(slot)
        gather_wait(nslot)


def _moe(x2d, wg, wu, wd, src_row, dst_row, blk_exp, n_used, n_blocks):
    n_tok, dim = x2d.shape
    last = jnp.maximum(n_used - 1, 0)
    jj = jnp.arange(n_blocks, dtype=jnp.int32)
    bexp_eff = blk_exp[jnp.minimum(jj, last)]
    n_real = TOP_K * n_tok
    out_rows = n_real + MOE_TRASH_BLOCKS * MOE_BLOCK

    def wmap(j, bexp, nused, src, dst):
        return (bexp[j], 0, 0)

    grid_spec = pltpu.PrefetchScalarGridSpec(
        num_scalar_prefetch=4,
        grid=(n_blocks,),
        in_specs=[pl.BlockSpec(memory_space=pl.ANY),
                  pl.BlockSpec((None, dim, D_FF), wmap),
                  pl.BlockSpec((None, dim, D_FF), wmap),
                  pl.BlockSpec((None, D_FF, dim), wmap)],
        out_specs=pl.BlockSpec(memory_space=pl.ANY),
        scratch_shapes=[pltpu.VMEM((2, MOE_BLOCK, dim), F32),
                        pltpu.VMEM((MOE_BLOCK, dim), BF16),
                        pltpu.VMEM((2, MOE_BLOCK, dim), F32),
                        pltpu.SemaphoreType.DMA((2,)),
                        pltpu.SemaphoreType.DMA((2,))],
    )
    body = functools.partial(_moe_body, n_blocks=n_blocks, n_real=n_real)
    return pl.pallas_call(
        body,
        grid_spec=grid_spec,
        out_shape=jax.ShapeDtypeStruct((out_rows, dim), F32),
        compiler_params=pltpu.CompilerParams(dimension_semantics=("arbitrary",),
                                             vmem_limit_bytes=60 * 1024 * 1024),
        name="moe_swiglu",
    )(bexp_eff, n_used.reshape(1), src_row, dst_row, x2d, wg, wu, wd)


def _combine_body(x_ref, gp_ref, y0_ref, y1_ref, g_ref, b_ref, o_ref):
    gp = gp_ref[...]
    moe = gp[:, 0:1] * y0_ref[...] + gp[:, 1:2] * y1_ref[...]
    y = ALPHA * x_ref[...] + moe
    o_ref[...] = _layernorm(y, g_ref[...], b_ref[...])


def _combine(x2d, gates_pad, yrows, g, b):
    n_tok, dim = x2d.shape
    tm = 512
    nt = n_tok // tm
    return pl.pallas_call(
        _combine_body,
        grid=(nt,),
        in_specs=[pl.BlockSpec((tm, dim), lambda i: (i, 0)),
                  pl.BlockSpec((tm, 128), lambda i: (i, 0)),
                  pl.BlockSpec((tm, dim), lambda i: (i, 0)),
                  pl.BlockSpec((tm, dim), lambda i: (nt + i, 0)),
                  _const_spec(g.shape), _const_spec(b.shape)],
        out_specs=pl.BlockSpec((tm, dim), lambda i: (i, 0)),
        out_shape=jax.ShapeDtypeStruct((n_tok, dim), F32),
        compiler_params=_cparams(("parallel",)),
        name="moe_combine_ln",
    )(x2d, gates_pad, yrows, yrows, g, b)


def _row(v):
    return v.astype(F32).reshape(1, -1)


def _even_layer(x, w_in, a_re, a_im, log_dt, b_re, b_im, c_re, c_im, d_skip, glu_w, glu_b,
                w_out, ln1_g, ln1_b, w_gate, w_up, w_down, ln2_g, ln2_b):
    bsz, seq, dim = x.shape
    h = _inproj(x, w_in.astype(BF16))
    wb, wc, lam = _s5_discretize(a_re, a_im, log_dt, b_re, b_im, c_re, c_im, bsz)
    ya = _s5(h, wb, wc, lam, _row(d_skip), glu_w.astype(BF16), _row(glu_b))
    yb = _retention(h, 1)
    w_out = w_out.astype(BF16)
    x1 = _outln_even(x, ya, yb, w_out[:S5_CHANNELS], w_out[S5_CHANNELS:], _row(ln1_g), _row(ln1_b))
    x2 = _ffn(x1.reshape(bsz * seq, dim), w_gate.astype(BF16), w_up.astype(BF16), w_down.astype(BF16),
              _row(ln2_g), _row(ln2_b))
    return x2.reshape(bsz, seq, dim)


def _odd_layer(x, w_qkv, w_out, ln1_g, ln1_b, router_w, router_b, w_gate, w_up, w_down, ln2_g, ln2_b):
    bsz, seq, dim = x.shape
    n_tok = bsz * seq
    qkv = _qkv(x, w_qkv.astype(BF16))
    att = _dilated_attention(qkv)
    rw = jnp.zeros((dim, 128), F32).at[:, :N_EXPERTS].set(router_w.astype(F32))
    rb = jnp.zeros((1, 128), F32).at[:, :N_EXPERTS].set(router_b.astype(F32)[None, :])
    x1, logits = _outln_odd(x, att, w_out.astype(BF16), _row(ln1_g), _row(ln1_b), rw, rb)
    x1 = x1.reshape(n_tok, dim)
    gates, src_row, dst_row, blk_exp, n_used, n_blocks = _route(logits.reshape(n_tok, 128)[:, :N_EXPERTS], n_tok)
    yrows = _moe(x1, w_gate.astype(BF16), w_up.astype(BF16), w_down.astype(BF16),
                 src_row, dst_row, blk_exp, n_used, n_blocks)
    gates_pad = jnp.zeros((n_tok, 128), F32).at[:, :TOP_K].set(gates)
    out = _combine(x1, gates_pad, yrows, _row(ln2_g), _row(ln2_b))
    return out.reshape(bsz, seq, dim)


def kernel(x, ev_w_in, ev_s5_a_re, ev_s5_a_im, ev_s5_log_dt, ev_s5_b_re, ev_s5_b_im, ev_s5_c_re, ev_s5_c_im, ev_s5_d, ev_s5_glu_w, ev_s5_glu_b, ev_w_out, ev_ln1_g, ev_ln1_b, ev_ffn_w_gate, ev_ffn_w_up, ev_ffn_w_down, ev_ln2_g, ev_ln2_b, od_w_qkv, od_w_out, od_ln1_g, od_ln1_b, od_router_w, od_router_b, od_moe_w_gate, od_moe_w_up, od_moe_w_down, od_ln2_g, od_ln2_b):
    for layer in range(DEPTH):
        i = layer // 2
        if layer % 2 == 0:
            x = _even_layer(x, ev_w_in[i], ev_s5_a_re[i], ev_s5_a_im[i], ev_s5_log_dt[i], ev_s5_b_re[i],
                            ev_s5_b_im[i], ev_s5_c_re[i], ev_s5_c_im[i], ev_s5_d[i], ev_s5_glu_w[i],
                            ev_s5_glu_b[i], ev_w_out[i], ev_ln1_g[i], ev_ln1_b[i], ev_ffn_w_gate[i],
                            ev_ffn_w_up[i], ev_ffn_w_down[i], ev_ln2_g[i], ev_ln2_b[i])
        else:
            x = _odd_layer(x, od_w_qkv[i], od_w_out[i], od_ln1_g[i], od_ln1_b[i], od_router_w[i],
                           od_router_b[i], od_moe_w_gate[i], od_moe_w_up[i], od_moe_w_down[i],
                           od_ln2_g[i], od_ln2_b[i])
    return x
```

```python
import functools
import math

import numpy as np
import jax
import jax.numpy as jnp
from jax import lax
from jax.experimental import pallas as pl
from jax.experimental.pallas import tpu as pltpu

F32 = jnp.float32
BF16 = jnp.bfloat16

LN_EPS = 1e-5
DEPTH = 2
ALPHA = (2.0 * DEPTH) ** 0.25

S5_CHANNELS = 512
S5_GROUPS = 32
S5_GROUP = 16
S5_STATE = 64
S5_NSTATE = S5_GROUPS * S5_STATE

RET_HEADS = 4
RET_HEAD_DIM = 128
RET_CHUNK = 128

ATT_HEADS = 16
ATT_HEAD_DIM = 64
ATT_PAIRS = ATT_HEADS // 2
ATT_BLOCK = 128
ATT_SPAN = 128
ATT_DILATIONS = (1, 4, 16)
NEG_BIG = -1e30
LOG2E = math.log2(math.e)
ATT_Q_SCALE = ATT_HEAD_DIM ** -0.5 * LOG2E

D_FF = 2816
N_EXPERTS = 8
TOP_K = 2
MOE_BLOCK = 512

VMEM_LIMIT = 56 * 1024 * 1024


def _cparams(sem):
    return pltpu.CompilerParams(dimension_semantics=sem, vmem_limit_bytes=VMEM_LIMIT)


def _layernorm(y, g, b):
    mu = jnp.mean(y, -1, keepdims=True)
    yc = y - mu
    var = jnp.mean(yc * yc, -1, keepdims=True)
    return yc * lax.rsqrt(var + LN_EPS) * g + b


def _const_spec(shape):
    nd = len(shape)
    return pl.BlockSpec(shape, lambda *_: (0,) * nd)


def _inproj_body(x_ref, w_ref, h_ref):
    xb = x_ref[...].astype(BF16)
    h_ref[...] = jnp.dot(xb, w_ref[...], preferred_element_type=F32).astype(BF16)


def _inproj(x, w):
    bsz, seq, dim = x.shape
    tm = 512
    nh = w.shape[1]
    return pl.pallas_call(
        _inproj_body,
        grid=(bsz, seq // tm),
        in_specs=[pl.BlockSpec((None, tm, dim), lambda b, s: (b, s, 0)),
                  _const_spec(w.shape)],
        out_specs=pl.BlockSpec((None, tm, nh), lambda b, s: (b, s, 0)),
        out_shape=jax.ShapeDtypeStruct((bsz, seq, nh), BF16),
        compiler_params=_cparams(("parallel", "parallel")),
        name="inproj",
    )(x, w)


def _s5_discretize(a_re, a_im, log_dt, b_re, b_im, c_re, c_im, bsz):
    ar = a_re.astype(F32)
    ai = a_im.astype(F32)
    dt = jnp.exp(log_dt.astype(F32))[:, None]
    mag = jnp.exp(ar * dt)
    lr = mag * jnp.cos(ai * dt)
    li = mag * jnp.sin(ai * dt)
    den = ar * ar + ai * ai
    zr = ((lr - 1.0) * ar + li * ai) / den
    zi = (li * ar - (lr - 1.0) * ai) / den
    br = b_re.astype(F32)
    bi = b_im.astype(F32)
    bbr = zr[..., None] * br - zi[..., None] * bi
    bbi = zr[..., None] * bi + zi[..., None] * br
    eye = jnp.eye(S5_GROUPS, dtype=F32)
    nch = S5_CHANNELS
    wbr = jnp.einsum('gpc,gh->gchp', bbr, eye).reshape(nch, S5_NSTATE)
    wbi = jnp.einsum('gpc,gh->gchp', bbi, eye).reshape(nch, S5_NSTATE)
    wb = jnp.concatenate([wbr, wbi], axis=1).astype(BF16)
    wcr = jnp.einsum('gcp,gh->hpgc', c_re.astype(F32), eye).reshape(S5_NSTATE, nch)
    wci = jnp.einsum('gcp,gh->hpgc', c_im.astype(F32), eye).reshape(S5_NSTATE, nch)
    wc = jnp.concatenate([wcr, -wci], axis=0).astype(BF16)
    lam = jnp.stack([jnp.broadcast_to(lr.reshape(1, S5_NSTATE), (bsz, S5_NSTATE)),
                     jnp.broadcast_to(li.reshape(1, S5_NSTATE), (bsz, S5_NSTATE))])
    return wb, wc, lam


def _gelu_tanh(x):
    return 0.5 * x * (1.0 + jnp.tanh(math.sqrt(2.0 / math.pi) * (x + 0.044715 * (x * x * x))))


def _s5_body(u_ref, wb_ref, wc_ref, lam_ref, d_ref, gw_ref, gb_ref, o_ref, xs_ref, st_ref, utb_ref, ys_ref,
             *, lc, bsz, ncb):
    @pl.when(pl.program_id(0) == 0)
    def _():
        st_ref[...] = jnp.zeros_like(st_ref)

    nlb = S5_CHANNELS // 128
    for b in range(bsz):
        ub = u_ref[b].astype(F32)
        for c in range(nlb):
            utb_ref[c, pl.ds(b, lc, stride=bsz), :] = ub[:, c * 128:(c + 1) * 128]
    uf = jnp.concatenate([utb_ref[c] for c in range(nlb)], axis=-1)
    ub = uf.astype(BF16)
    half_c = S5_CHANNELS // 2
    half_s = S5_NSTATE // 2
    for q in range(4):
        hq = q % 2
        xs_ref[:, q * half_s:(q + 1) * half_s] = jnp.dot(
            ub[:, hq * half_c:(hq + 1) * half_c],
            wb_ref[hq * half_c:(hq + 1) * half_c, q * half_s:(q + 1) * half_s],
            preferred_element_type=F32)
    width = S5_NSTATE // ncb
    for cb in range(ncb):
        re = slice(cb * width, (cb + 1) * width)
        im = slice(S5_NSTATE + cb * width, S5_NSTATE + (cb + 1) * width)
        lr = lam_ref[0, :, re]
        li = lam_ref[1, :, re]

        def step(t, carry, re=re, im=im, lr=lr, li=li):
            xr, xi = carry
            row = pl.multiple_of(t * bsz, bsz)
            nxr = lr * xr - li * xi + xs_ref[pl.ds(row, bsz), re]
            nxi = lr * xi + li * xr + xs_ref[pl.ds(row, bsz), im]
            xs_ref[pl.ds(row, bsz), re] = nxr
            xs_ref[pl.ds(row, bsz), im] = nxi
            return nxr, nxi

        xr, xi = lax.fori_loop(0, lc, step, (st_ref[:, re], st_ref[:, im]), unroll=8)
        st_ref[:, re] = xr
        st_ref[:, im] = xi

    ys = []
    for hq in range(2):
        cols = slice(hq * half_c, (hq + 1) * half_c)
        part = None
        for ri in range(2):
            rows = slice((2 * ri + hq) * half_s, (2 * ri + hq + 1) * half_s)
            t = jnp.dot(xs_ref[:, rows].astype(BF16), wc_ref[rows, cols], preferred_element_type=F32)
            part = t if part is None else part + t
        ys.append(part)
    y = jnp.concatenate(ys, axis=-1)
    y = y + d_ref[...] * uf
    y = _gelu_tanh(y)
    z = jnp.dot(y.astype(BF16), gw_ref[...], preferred_element_type=F32) + gb_ref[...]
    yo = y * jax.nn.sigmoid(z)
    for c in range(nlb):
        ys_ref[c] = yo[:, c * 128:(c + 1) * 128]
    for b in range(bsz):
        o_ref[b] = jnp.concatenate([ys_ref[c, pl.ds(b, lc, stride=bsz), :] for c in range(nlb)],
                                   axis=-1).astype(BF16)


def _s5(h, wb, wc, lam, d_skip, glu_w, glu_b):
    bsz, seq, _ = h.shape
    lc = 64
    blk = lc * bsz
    body = functools.partial(_s5_body, lc=lc, bsz=bsz, ncb=4)
    return pl.pallas_call(
        body,
        grid=(seq // lc,),
        in_specs=[pl.BlockSpec((bsz, lc, S5_CHANNELS), lambda s: (0, s, 0)),
                  _const_spec(wb.shape), _const_spec(wc.shape), _const_spec(lam.shape),
                  _const_spec(d_skip.shape), _const_spec(glu_w.shape), _const_spec(glu_b.shape)],
        out_specs=pl.BlockSpec((bsz, lc, S5_CHANNELS), lambda s: (0, s, 0)),
        out_shape=jax.ShapeDtypeStruct((bsz, seq, S5_CHANNELS), BF16),
        scratch_shapes=[pltpu.VMEM((blk, 2 * S5_NSTATE), F32),
                        pltpu.VMEM((bsz, 2 * S5_NSTATE), F32),
                        pltpu.VMEM((S5_CHANNELS // 128, blk, 128), F32),
                        pltpu.VMEM((S5_CHANNELS // 128, blk, 128), F32)],
        compiler_params=_cparams(("arbitrary",)),
        name="s5_mixer",
    )(h, wb, wc, lam, d_skip, glu_w, glu_b)


def _ret_constants():
    hh = np.arange(RET_HEADS, dtype=np.float64)
    log_gamma = np.log1p(-np.exp2(-5.0 - hh))
    pos = np.arange(RET_CHUNK, dtype=np.float64)
    diff = pos[:, None] - pos[None, :]
    scale = RET_HEAD_DIM ** -0.5
    decay = np.where(diff >= 0, np.exp(log_gamma[:, None, None] * np.maximum(diff, 0.0)), 0.0) * scale
    zeta = np.exp(log_gamma[:, None] * (RET_CHUNK - 1.0 - pos)) * scale
    xi = np.exp(log_gamma[:, None] * (pos + 1.0))
    chunk_decay = np.exp(log_gamma * RET_CHUNK)
    zeta_b = np.broadcast_to(zeta[:, :, None], (RET_HEADS, RET_CHUNK, RET_HEAD_DIM))
    xi_b = np.broadcast_to(xi[:, :, None], (RET_HEADS, RET_CHUNK, RET_HEAD_DIM))
    return (decay.astype(np.float32), np.ascontiguousarray(zeta_b).astype(np.float32),
            np.ascontiguousarray(xi_b).astype(np.float32), [float(c) for c in chunk_decay])


def _ret_body(q_ref, k_ref, v_ref, g_ref, dec_ref, zeta_ref, xi_ref, o_ref, st_ref, *, ts, chunk_decay):
    @pl.when(pl.program_id(1) == 0)
    def _():
        st_ref[...] = jnp.zeros_like(st_ref)

    for c in range(ts // RET_CHUNK):
        rows = slice(c * RET_CHUNK, (c + 1) * RET_CHUNK)
        for h in range(RET_HEADS):
            cols = slice(h * RET_HEAD_DIM, (h + 1) * RET_HEAD_DIM)
            q = q_ref[rows, cols]
            k = k_ref[rows, cols]
            v = v_ref[rows, cols]
            s = lax.dot_general(q, k, (((1,), (1,)), ((), ())), preferred_element_type=F32) * dec_ref[h]
            inner = jnp.dot(s.astype(BF16), v, preferred_element_type=F32)
            st = st_ref[h]
            cross = jnp.dot(q, st.astype(BF16), preferred_element_type=F32) * xi_ref[h]
            ret = inner + cross
            mu = jnp.mean(ret, -1, keepdims=True)
            rc = ret - mu
            var = jnp.mean(rc * rc, -1, keepdims=True)
            rn = rc * lax.rsqrt(var + LN_EPS)
            gg = g_ref[rows, cols].astype(F32)
            o_ref[rows, cols] = (gg * jax.nn.sigmoid(gg) * rn).astype(BF16)
            kz = (k.astype(F32) * zeta_ref[h]).astype(BF16)
            kv = lax.dot_general(kz, v, (((0,), (0,)), ((), ())), preferred_element_type=F32)
            st_ref[h] = st * chunk_decay[h] + kv


def _retention(h, first_block):
    bsz, seq, _ = h.shape
    ts = 512
    width = RET_HEADS * RET_HEAD_DIM
    decay, zeta_b, xi_b, chunk_decay = _ret_constants()
    body = functools.partial(_ret_body, ts=ts, chunk_decay=chunk_decay)

    def col(j):
        return pl.BlockSpec((None, ts, width), lambda b, s, j=j: (b, s, j))

    return pl.pallas_call(
        body,
        grid=(bsz, seq // ts),
        in_specs=[col(first_block), col(first_block + 1), col(first_block + 2), col(first_block + 3),
                  _const_spec(decay.shape), _const_spec(zeta_b.shape), _const_spec(xi_b.shape)],
        out_specs=pl.BlockSpec((None, ts, width), lambda b, s: (b, s, 0)),
        out_shape=jax.ShapeDtypeStruct((bsz, seq, width), BF16),
        scratch_shapes=[pltpu.VMEM((RET_HEADS, RET_HEAD_DIM, RET_HEAD_DIM), F32)],
        compiler_params=_cparams(("parallel", "arbitrary")),
        name="retention",
    )(h, h, h, h, jnp.asarray(decay), jnp.asarray(zeta_b), jnp.asarray(xi_b))


def _outln_even_body(x_ref, ya_ref, yb_ref, w1_ref, w2_ref, g_ref, b_ref, o_ref):
    acc = jnp.dot(ya_ref[...], w1_ref[...], preferred_element_type=F32)
    acc = acc + jnp.dot(yb_ref[...], w2_ref[...], preferred_element_type=F32)
    y = ALPHA * x_ref[...] + acc
    o_ref[...] = _layernorm(y, g_ref[...], b_ref[...])


def _outln_even(x, ya, yb, w1, w2, g, b):
    bsz, seq, dim = x.shape
    tm = 512
    wa = ya.shape[2]
    wbw = yb.shape[2]
    return pl.pallas_call(
        _outln_even_body,
        grid=(bsz, seq // tm),
        in_specs=[pl.BlockSpec((None, tm, dim), lambda b_, s: (b_, s, 0)),
                  pl.BlockSpec((None, tm, wa), lambda b_, s: (b_, s, 0)),
                  pl.BlockSpec((None, tm, wbw), lambda b_, s: (b_, s, 0)),
                  _const_spec(w1.shape), _const_spec(w2.shape),
                  _const_spec(g.shape), _const_spec(b.shape)],
        out_specs=pl.BlockSpec((None, tm, dim), lambda b_, s: (b_, s, 0)),
        out_shape=jax.ShapeDtypeStruct((bsz, seq, dim), F32),
        compiler_params=_cparams(("parallel", "parallel")),
        name="outproj_ln_even",
    )(x, ya, yb, w1, w2, g, b)


FFN_BOUNDS = (0, 1536, 2816)


def _ffn_body(x_ref, wg_ref, wu_ref, wd_ref, g_ref, b_ref, o_ref):
    x = x_ref[...]
    xb = x.astype(BF16)
    acc = None
    for c in range(len(FFN_BOUNDS) - 1):
        cs = slice(FFN_BOUNDS[c], FFN_BOUNDS[c + 1])
        gt = jnp.dot(xb, wg_ref[:, cs], preferred_element_type=F32)
        up = jnp.dot(xb, wu_ref[:, cs], preferred_element_type=F32)
        hh = (gt * jax.nn.sigmoid(gt) * up).astype(BF16)
        part = jnp.dot(hh, wd_ref[cs, :], preferred_element_type=F32)
        acc = part if acc is None else acc + part
    y = ALPHA * x + acc
    o_ref[...] = _layernorm(y, g_ref[...], b_ref[...])


def _ffn(x2d, wg, wu, wd, g, b):
    n_tok, dim = x2d.shape
    tm = 512
    single = pl.Buffered(1)

    def wspec(shape):
        return pl.BlockSpec(shape, lambda i: (0, 0), pipeline_mode=single)

    return pl.pallas_call(
        _ffn_body,
        grid=(n_tok // tm,),
        in_specs=[pl.BlockSpec((tm, dim), lambda i: (i, 0)),
                  wspec(wg.shape), wspec(wu.shape), wspec(wd.shape),
                  _const_spec(g.shape), _const_spec(b.shape)],
        out_specs=pl.BlockSpec((tm, dim), lambda i: (i, 0)),
        out_shape=jax.ShapeDtypeStruct((n_tok, dim), F32),
        compiler_params=_cparams(("parallel",)),
        name="ffn_swiglu_ln",
    )(x2d, wg, wu, wd, g, b)


def _qkv_body(x_ref, w_ref, o_ref):
    xb = x_ref[...].astype(BF16)
    r = jnp.dot(xb, w_ref[...], preferred_element_type=F32)
    nblk = o_ref.shape[0]
    for j in range(nblk):
        blk = r[:, j * 128:(j + 1) * 128]
        if j < ATT_PAIRS:
            blk = blk * ATT_Q_SCALE
        o_ref[j] = blk.astype(BF16)


def _qkv(x, w):
    bsz, seq, dim = x.shape
    tm = 512
    nblk = w.shape[1] // 128
    return pl.pallas_call(
        _qkv_body,
        grid=(bsz, seq // tm),
        in_specs=[pl.BlockSpec((None, tm, dim), lambda b, s: (b, s, 0)),
                  _const_spec(w.shape)],
        out_specs=pl.BlockSpec((None, nblk, tm, 128), lambda b, s: (b, 0, s, 0)),
        out_shape=jax.ShapeDtypeStruct((bsz, nblk, seq, 128), BF16),
        compiler_params=_cparams(("parallel", "parallel")),
        name="qkv_proj",
    )(x, w)


ATT_GROUP = 8


def _att_constants():
    slopes = np.exp2(-8.0 * np.arange(1, ATT_HEADS + 1, dtype=np.float64) / ATT_HEADS)
    sl = np.zeros((ATT_PAIRS, 2 * ATT_BLOCK, 2 * ATT_BLOCK), np.float32)
    for hp in range(ATT_PAIRS):
        sl[hp, :ATT_BLOCK, :] = slopes[2 * hp]
        sl[hp, ATT_BLOCK:, :] = slopes[2 * hp + 1]
    return sl


def _att_unit(q, kw, vw, bias, old, low, ones, last):
    zero = jnp.zeros_like(q)
    qs = jnp.concatenate([jnp.where(low, q, zero), jnp.where(low, zero, q)], axis=0)
    s = lax.dot_general(qs, kw, (((1,), (1,)), ((), ())), preferred_element_type=F32) + bias
    sa = s[:, :128]
    sb = s[:, 128:]
    mrow = jnp.max(jnp.maximum(sa, sb), axis=-1, keepdims=True)
    m_new = jnp.broadcast_to(mrow, (2 * ATT_BLOCK, 128))
    if old is not None:
        m_old, acc_old, l0_old, l1_old = old
        m_new = jnp.maximum(m_old, m_new)
    p = jnp.concatenate([jnp.exp2(sa - m_new), jnp.exp2(sb - m_new)], axis=1).astype(BF16)
    vaug = jnp.concatenate([vw, ones], axis=1)
    res = jnp.dot(p, vaug, preferred_element_type=F32)
    acc = jnp.where(low, res[:ATT_BLOCK, :128], res[ATT_BLOCK:, :128])
    l0 = res[:ATT_BLOCK, 128:]
    l1 = res[ATT_BLOCK:, 128:]
    if old is not None:
        a = jnp.exp2(m_old - m_new)
        a0 = a[:ATT_BLOCK]
        a1 = a[ATT_BLOCK:]
        acc = jnp.where(low, a0, a1) * acc_old + acc
        l0 = a0 * l0_old + l0
        l1 = a1 * l1_old + l1
    if last:
        acc = acc / jnp.where(low, l0, l1)
    return m_new, acc, l0, l1


def _att_body(q_ref, k_ref, v_ref, sl_ref, o_ref,
              tmp_ref, q4_ref, k4_ref, v4_ref, nat_ref, de4_ref, bias_ref, *, seq):
    quarter = seq // 4
    for src, dst in ((q_ref, q4_ref), (k_ref, k4_ref), (v_ref, v4_ref)):
        tmp_ref[...] = src[...].astype(F32)
        for r in range(4):
            dst[r] = tmp_ref[pl.ds(r, quarter, stride=4), :]
    lane = lax.broadcasted_iota(jnp.int32, (ATT_BLOCK, 128), 1)
    low = lane < ATT_HEAD_DIM
    ones = jnp.ones((2 * ATT_BLOCK, 128), BF16)

    qi = lax.broadcasted_iota(jnp.int32, (2 * ATT_BLOCK, 2 * ATT_BLOCK), 0) & (ATT_BLOCK - 1)
    ki = lax.broadcasted_iota(jnp.int32, (2 * ATT_BLOCK, 2 * ATT_BLOCK), 1)
    sl = sl_ref[...]
    for bi, d in enumerate(ATT_DILATIONS):
        for oi in range(2):
            steps = (qi - ki + oi * ATT_BLOCK).astype(F32)
            allowed = (steps >= 0.0) & (steps <= float(ATT_SPAN))
            bias_ref[2 * bi + oi] = jnp.where(allowed, -(sl * (float(d) * LOG2E)) * steps, NEG_BIG)

    n_units = seq // ATT_BLOCK
    for bi, d in enumerate(ATT_DILATIONS):
        first = bi == 0
        last = bi == len(ATT_DILATIONS) - 1
        shift = d.bit_length() - 1

        def group(it, carry, bi=bi, d=d, shift=shift, first=first, last=last):
            loaded = []
            for g in range(ATT_GROUP):
                n = it * ATT_GROUP + g
                i = n >> shift
                r = n & (d - 1)
                iw = jnp.maximum(i - 1, 0)
                if d == 1:
                    qsl = pl.ds(pl.multiple_of(i * ATT_BLOCK, ATT_BLOCK), ATT_BLOCK)
                    wsl = pl.ds(pl.multiple_of(iw * ATT_BLOCK, ATT_BLOCK), 2 * ATT_BLOCK)
                    data = (q_ref[qsl, :], k_ref[wsl, :], v_ref[wsl, :])
                    old_at = None
                    new_at = lambda s, qsl=qsl: nat_ref.at[s, qsl, :]
                else:
                    if d == 4:
                        r4 = r
                        qsl = pl.ds(pl.multiple_of(i * ATT_BLOCK, ATT_BLOCK), ATT_BLOCK)
                        wsl = pl.ds(pl.multiple_of(iw * ATT_BLOCK, ATT_BLOCK), 2 * ATT_BLOCK)
                        osl = pl.ds(r + (4 * ATT_BLOCK) * i, ATT_BLOCK, stride=4)
                        old_at = lambda s, osl=osl: nat_ref.at[s, osl, :]
                    else:
                        r4 = r & 3
                        qsl = pl.ds((r >> 2) + (4 * ATT_BLOCK) * i, ATT_BLOCK, stride=4)
                        wsl = pl.ds((r >> 2) + (4 * ATT_BLOCK) * iw, 2 * ATT_BLOCK, stride=4)
                        old_at = lambda s, r4=r4, qsl=qsl: de4_ref.at[s, r4, qsl, :]
                    data = (q4_ref[r4, qsl, :].astype(BF16), k4_ref[r4, wsl, :].astype(BF16),
                            v4_ref[r4, wsl, :].astype(BF16))
                    new_at = lambda s, r4=r4, qsl=qsl: de4_ref.at[s, r4, qsl, :]
                old = None
                if old_at is not None:
                    old = (jnp.concatenate([old_at(1)[...], old_at(2)[...]], axis=0),
                           old_at(0)[...], old_at(3)[...], old_at(4)[...])
                loaded.append((new_at, data, bias_ref[2 * bi + jnp.minimum(i, 1)], old))
            results = [_att_unit(q, kw, vw, bias, old, low, ones, last)
                       for (_, (q, kw, vw), bias, old) in loaded]
            for (new_at, *_), (m_new, acc, l0, l1) in zip(loaded, results):
                new_at(0)[...] = acc
                if not last:
                    new_at(1)[...] = m_new[:ATT_BLOCK]
                    new_at(2)[...] = m_new[ATT_BLOCK:]
                    new_at(3)[...] = l0
                    new_at(4)[...] = l1
            return carry

        lax.fori_loop(0, n_units // ATT_GROUP, group, 0)

    for r in range(4):
        nat_ref[0, pl.ds(r, quarter, stride=4), :] = de4_ref[0, r]
    o_ref[...] = nat_ref[0].astype(BF16)


def _dilated_attention(qkv):
    bsz, nblk, seq, _ = qkv.shape
    sl = jnp.asarray(_att_constants())
    specs = [pl.BlockSpec((None, None, seq, 128), lambda b, hp, j=j: (b, j * ATT_PAIRS + hp, 0, 0))
             for j in range(3)]
    specs.append(pl.BlockSpec((None,) + sl.shape[1:], lambda b, hp: (hp, 0, 0)))
    body = functools.partial(_att_body, seq=seq)
    return pl.pallas_call(
        body,
        grid=(bsz, ATT_PAIRS),
        in_specs=specs,
        out_specs=pl.BlockSpec((None, None, seq, 128), lambda b, hp: (b, hp, 0, 0)),
        out_shape=jax.ShapeDtypeStruct((bsz, ATT_PAIRS, seq, 128), BF16),
        scratch_shapes=[pltpu.VMEM((seq, 128), F32)]
        + [pltpu.VMEM((4, seq // 4, 128), F32) for _ in range(3)]
        + [pltpu.VMEM((5, seq, 128), F32), pltpu.VMEM((5, 4, seq // 4, 128), F32),
           pltpu.VMEM((2 * len(ATT_DILATIONS), 2 * ATT_BLOCK, 2 * ATT_BLOCK), F32)],
        compiler_params=_cparams(("parallel", "parallel")),
        name="dilated_attention",
    )(qkv, qkv, qkv, sl)


def _outln_odd_body(x_ref, a_ref, w_ref, g_ref, b_ref, rwh_ref, rwl_ref, rb_ref, o_ref, lg_ref):
    a = jnp.concatenate([a_ref[j] for j in range(a_ref.shape[0])], axis=-1)
    acc = jnp.dot(a, w_ref[...], preferred_element_type=F32)
    y = ALPHA * x_ref[...] + acc
    o = _layernorm(y, g_ref[...], b_ref[...])
    o_ref[...] = o
    oh = o.astype(BF16)
    ol = (o - oh.astype(F32)).astype(BF16)
    lg = jnp.dot(oh, rwh_ref[...], preferred_element_type=F32)
    lg = lg + jnp.dot(ol, rwh_ref[...], preferred_element_type=F32)
    lg = lg + jnp.dot(oh, rwl_ref[...], preferred_element_type=F32)
    lg_ref[...] = lg + rb_ref[...]


def _outln_odd(x, att, w, g, b, rw, rb):
    bsz, seq, dim = x.shape
    tm = 512
    npair = att.shape[1]
    rwh = rw.astype(BF16)
    rwl = (rw - rwh.astype(F32)).astype(BF16)
    return pl.pallas_call(
        _outln_odd_body,
        grid=(bsz, seq // tm),
        in_specs=[pl.BlockSpec((None, tm, dim), lambda b_, s: (b_, s, 0)),
                  pl.BlockSpec((None, npair, tm, 128), lambda b_, s: (b_, 0, s, 0)),
                  _const_spec(w.shape), _const_spec(g.shape), _const_spec(b.shape),
                  _const_spec(rw.shape), _const_spec(rw.shape), _const_spec(rb.shape)],
        out_specs=[pl.BlockSpec((None, tm, dim), lambda b_, s: (b_, s, 0)),
                   pl.BlockSpec((None, tm, 128), lambda b_, s: (b_, s, 0))],
        out_shape=[jax.ShapeDtypeStruct((bsz, seq, dim), F32),
                   jax.ShapeDtypeStruct((bsz, seq, 128), F32)],
        compiler_params=_cparams(("parallel", "parallel")),
        name="outproj_ln_odd",
    )(x, att, w, g, b, rwh, rwl, rb)


MOE_FF_BOUNDS = (0, 1536, 2816)
MOE_NCHUNK = len(MOE_FF_BOUNDS) - 1
MOE_TRASH_BLOCKS = 2 + 2 * N_EXPERTS
MOE_NBUF = 3


def _route(logits, n_tok):
    top_val, top_idx = lax.top_k(logits, TOP_K)
    gates = jax.nn.softmax(top_val, axis=-1)
    n_assign = n_tok * TOP_K
    exp_flat = top_idx.reshape(n_assign).astype(jnp.int32)
    experts = jnp.arange(N_EXPERTS, dtype=jnp.int32)

    def lookup(table, idx):
        return jnp.sum(jnp.where(idx[:, None] == experts[None, :], table[None, :], 0), axis=1)

    counts = jnp.sum((exp_flat[:, None] == experts[None, :]).astype(jnp.int32), axis=0)
    starts = jnp.cumsum(counts) - counts
    padded_counts = (counts + MOE_BLOCK - 1) // MOE_BLOCK * MOE_BLOCK
    pends = jnp.cumsum(padded_counts)
    pstarts = pends - padded_counts
    order = jnp.argsort(exp_flat).astype(jnp.int32)
    n_blocks = -(-n_assign // MOE_BLOCK) + N_EXPERTS
    n_rows = n_blocks * MOE_BLOCK
    blk_exp = jnp.minimum(jnp.sum((pends[None, :] <= (jnp.arange(n_blocks, dtype=jnp.int32) * MOE_BLOCK)[:, None])
                                  .astype(jnp.int32), axis=1), N_EXPERTS - 1)
    rows = jnp.arange(n_rows, dtype=jnp.int32)
    row_exp = jnp.repeat(blk_exp, MOE_BLOCK)
    local = rows - lookup(pstarts, row_exp)
    valid = local < lookup(counts, row_exp)
    spos = jnp.clip(lookup(starts, row_exp) + local, 0, n_assign - 1)
    n_used = (pends[-1] // MOE_BLOCK).astype(jnp.int32)
    assign = jnp.where(valid, order[spos], -1)
    real_dst = (assign % TOP_K) * n_tok + assign // TOP_K
    in_blk = rows % MOE_BLOCK
    blk = rows // MOE_BLOCK
    trash_blk = jnp.where(blk < n_used, 1 + row_exp, 1 + N_EXPERTS + blk - n_used)
    dst_mid = jnp.where(valid, real_dst, n_assign + trash_blk * MOE_BLOCK + in_blk)
    dst_front = n_assign + jnp.arange(MOE_BLOCK, dtype=jnp.int32)
    back = jnp.zeros(((MOE_NBUF - 1) * MOE_BLOCK,), jnp.int32)
    dst_row = jnp.concatenate([dst_front, dst_mid, back]).reshape(n_blocks + MOE_NBUF, MOE_BLOCK)
    src_row = jnp.concatenate([back[:MOE_BLOCK], jnp.where(valid, assign // TOP_K, 0), back])
    src_row = src_row.reshape(n_blocks + MOE_NBUF, MOE_BLOCK)
    return gates, src_row, dst_row, blk_exp, n_used, n_blocks


def _moe_body(bexp_ref, nused_ref, src_ref, dst_ref, x_hbm, wg_ref, wu_ref, wd_ref, o_hbm,
              xf_ref, xb_ref, yb_ref, gsem, ssem, *, n_blocks, n_real):
    j = pl.program_id(0)
    slot = lax.rem(j, MOE_NBUF)
    slot_m1 = lax.rem(j + MOE_NBUF - 1, MOE_NBUF)
    slot_p1 = lax.rem(j + 1, MOE_NBUF)
    active = j < nused_ref[0]

    def gather_start(blk, n, s):
        tok = src_ref[blk + 1, n]
        pltpu.make_async_copy(x_hbm.at[pl.ds(tok, 1), :], xf_ref.at[s, pl.ds(n, 1), :], gsem.at[s]).start()

    def scatter_start(blk, n, s):
        dst = dst_ref[blk + 1, n]
        pltpu.make_async_copy(yb_ref.at[s, pl.ds(n, 1), :], o_hbm.at[pl.ds(dst, 1), :], ssem.at[s]).start()

    def gather_wait(s):
        pltpu.make_async_copy(x_hbm.at[pl.ds(0, MOE_BLOCK), :], xf_ref.at[s], gsem.at[s]).wait()

    def scatter_wait(s):
        pltpu.make_async_copy(yb_ref.at[s], o_hbm.at[pl.ds(0, MOE_BLOCK), :], ssem.at[s]).wait()

    @pl.when(j == 0)
    def _():
        yb_ref[...] = jnp.zeros_like(yb_ref)
        for t in range(MOE_TRASH_BLOCKS):
            fill = pltpu.make_async_copy(
                yb_ref.at[0], o_hbm.at[pl.ds(n_real + t * MOE_BLOCK, MOE_BLOCK), :], ssem.at[0])
            fill.start()
            fill.wait()

        def issue(n, carry):
            gather_start(0, n, 0)
            gather_start(1, n, 1)
            return carry

        lax.fori_loop(0, MOE_BLOCK, issue, 0)

    gather_wait(slot)

    @pl.when(j >= MOE_NBUF - 1)
    def _():
        scatter_wait(slot)

    @pl.when(active)
    def _():
        xb_ref[...] = xf_ref[slot].astype(BF16)

    per = -(-MOE_BLOCK // MOE_NCHUNK)
    for c in range(MOE_NCHUNK):
        @pl.when(j + c < nused_ref[0] + c)
        def _(c=c):
            for n in range(c * per, min((c + 1) * per, MOE_BLOCK)):
                gather_start(j + 2, n, slot_m1)
                scatter_start(j - 1, n, slot_m1)
            cs = slice(MOE_FF_BOUNDS[c], MOE_FF_BOUNDS[c + 1])
            xb = xb_ref[...]
            gt = jnp.dot(xb, wg_ref[:, cs], preferred_element_type=F32)
            up = jnp.dot(xb, wu_ref[:, cs], preferred_element_type=F32)
            hh = (gt * jax.nn.sigmoid(gt) * up).astype(BF16)
            part = jnp.dot(hh, wd_ref[cs, :], preferred_element_type=F32)
            if c == 0:
                yb_ref[slot] = part
            else:
                yb_ref[slot] += part

    @pl.when(jnp.logical_not(active))
    def _():
        def issue(n, carry):
            gather_start(j + 2, n, slot_m1)
            scatter_start(j - 1, n, slot_m1)
            return carry

        lax.fori_loop(0, MOE_BLOCK, issue, 0)

    @pl.when(j == n_blocks - 1)
    def _():
        def issue(n, carry):
            scatter_start(j, n, slot)
            return carry

        lax.fori_loop(0, MOE_BLOCK, issue, 0)
        scatter_wait(slot_p1)
        scatter_wait(slot_m1)
        scatter_wait(slot)
        gather_wait(slot_p1)
        gather_wait(slot_m1)


def _moe(x2d, wg, wu, wd, src_row, dst_row, blk_exp, n_used, n_blocks):
    n_tok, dim = x2d.shape
    last = jnp.maximum(n_used - 1, 0)
    jj = jnp.arange(n_blocks, dtype=jnp.int32)
    bexp_eff = blk_exp[jnp.minimum(jj, last)]
    n_real = TOP_K * n_tok
    out_rows = n_real + MOE_TRASH_BLOCKS * MOE_BLOCK

    def wmap(j, bexp, nused, src, dst):
        return (bexp[j], 0, 0)

    grid_spec = pltpu.PrefetchScalarGridSpec(
        num_scalar_prefetch=4,
        grid=(n_blocks,),
        in_specs=[pl.BlockSpec(memory_space=pl.ANY),
                  pl.BlockSpec((None, dim, D_FF), wmap),
                  pl.BlockSpec((None, dim, D_FF), wmap),
                  pl.BlockSpec((None, D_FF, dim), wmap)],
        out_specs=pl.BlockSpec(memory_space=pl.ANY),
        scratch_shapes=[pltpu.VMEM((MOE_NBUF, MOE_BLOCK, dim), F32),
                        pltpu.VMEM((MOE_BLOCK, dim), BF16),
                        pltpu.VMEM((MOE_NBUF, MOE_BLOCK, dim), F32),
                        pltpu.SemaphoreType.DMA((MOE_NBUF,)),
                        pltpu.SemaphoreType.DMA((MOE_NBUF,))],
    )
    body = functools.partial(_moe_body, n_blocks=n_blocks, n_real=n_real)
    return pl.pallas_call(
        body,
        grid_spec=grid_spec,
        out_shape=jax.ShapeDtypeStruct((out_rows, dim), F32),
        compiler_params=pltpu.CompilerParams(dimension_semantics=("arbitrary",),
                                             vmem_limit_bytes=60 * 1024 * 1024),
        name="moe_swiglu",
    )(bexp_eff, n_used.reshape(1), src_row, dst_row, x2d, wg, wu, wd)


def _combine_body(x_ref, gp_ref, y0_ref, y1_ref, g_ref, b_ref, o_ref):
    gp = gp_ref[...]
    moe = gp[:, 0:1] * y0_ref[...] + gp[:, 1:2] * y1_ref[...]
    y = ALPHA * x_ref[...] + moe
    o_ref[...] = _layernorm(y, g_ref[...], b_ref[...])


def _combine(x2d, gates_pad, yrows, g, b):
    n_tok, dim = x2d.shape
    tm = 512
    nt = n_tok // tm
    return pl.pallas_call(
        _combine_body,
        grid=(nt,),
        in_specs=[pl.BlockSpec((tm, dim), lambda i: (i, 0)),
                  pl.BlockSpec((tm, 128), lambda i: (i, 0)),
                  pl.BlockSpec((tm, dim), lambda i: (i, 0)),
                  pl.BlockSpec((tm, dim), lambda i: (nt + i, 0)),
                  _const_spec(g.shape), _const_spec(b.shape)],
        out_specs=pl.BlockSpec((tm, dim), lambda i: (i, 0)),
        out_shape=jax.ShapeDtypeStruct((n_tok, dim), F32),
        compiler_params=_cparams(("parallel",)),
        name="moe_combine_ln",
    )(x2d, gates_pad, yrows, yrows, g, b)


def _row(v):
    return v.astype(F32).reshape(1, -1)


def _even_layer(x, w_in, a_re, a_im, log_dt, b_re, b_im, c_re, c_im, d_skip, glu_w, glu_b,
                w_out, ln1_g, ln1_b, w_gate, w_up, w_down, ln2_g, ln2_b):
    bsz, seq, dim = x.shape
    h = _inproj(x, w_in.astype(BF16))
    wb, wc, lam = _s5_discretize(a_re, a_im, log_dt, b_re, b_im, c_re, c_im, bsz)
    ya = _s5(h, wb, wc, lam, _row(d_skip), glu_w.astype(BF16), _row(glu_b))
    yb = _retention(h, 1)
    w_out = w_out.astype(BF16)
    x1 = _outln_even(x, ya, yb, w_out[:S5_CHANNELS], w_out[S5_CHANNELS:], _row(ln1_g), _row(ln1_b))
    x2 = _ffn(x1.reshape(bsz * seq, dim), w_gate.astype(BF16), w_up.astype(BF16), w_down.astype(BF16),
              _row(ln2_g), _row(ln2_b))
    return x2.reshape(bsz, seq, dim)


def _odd_layer(x, w_qkv, w_out, ln1_g, ln1_b, router_w, router_b, w_gate, w_up, w_down, ln2_g, ln2_b):
    bsz, seq, dim = x.shape
    n_tok = bsz * seq
    qkv = _qkv(x, w_qkv.astype(BF16))
    att = _dilated_attention(qkv)
    rw = jnp.zeros((dim, 128), F32).at[:, :N_EXPERTS].set(router_w.astype(F32))
    rb = jnp.zeros((1, 128), F32).at[:, :N_EXPERTS].set(router_b.astype(F32)[None, :])
    x1, logits = _outln_odd(x, att, w_out.astype(BF16), _row(ln1_g), _row(ln1_b), rw, rb)
    x1 = x1.reshape(n_tok, dim)
    gates, src_row, dst_row, blk_exp, n_used, n_blocks = _route(logits.reshape(n_tok, 128)[:, :N_EXPERTS], n_tok)
    yrows = _moe(x1, w_gate.astype(BF16), w_up.astype(BF16), w_down.astype(BF16),
                 src_row, dst_row, blk_exp, n_used, n_blocks)
    gates_pad = jnp.zeros((n_tok, 128), F32).at[:, :TOP_K].set(gates)
    out = _combine(x1, gates_pad, yrows, _row(ln2_g), _row(ln2_b))
    return out.reshape(bsz, seq, dim)


def kernel(x, ev_w_in, ev_s5_a_re, ev_s5_a_im, ev_s5_log_dt, ev_s5_b_re, ev_s5_b_im, ev_s5_c_re, ev_s5_c_im, ev_s5_d, ev_s5_glu_w, ev_s5_glu_b, ev_w_out, ev_ln1_g, ev_ln1_b, ev_ffn_w_gate, ev_ffn_w_up, ev_ffn_w_down, ev_ln2_g, ev_ln2_b, od_w_qkv, od_w_out, od_ln1_g, od_ln1_b, od_router_w, od_router_b, od_moe_w_gate, od_moe_w_up, od_moe_w_down, od_ln2_g, od_ln2_b):
    for layer in range(DEPTH):
        i = layer // 2
        if layer % 2 == 0:
            x = _even_layer(x, ev_w_in[i], ev_s5_a_re[i], ev_s5_a_im[i], ev_s5_log_dt[i], ev_s5_b_re[i],
                            ev_s5_b_im[i], ev_s5_c_re[i], ev_s5_c_im[i], ev_s5_d[i], ev_s5_glu_w[i],
                            ev_s5_glu_b[i], ev_w_out[i], ev_ln1_g[i], ev_ln1_b[i], ev_ffn_w_gate[i],
                            ev_ffn_w_up[i], ev_ffn_w_down[i], ev_ln2_g[i], ev_ln2_b[i])
        else:
            x = _odd_layer(x, od_w_qkv[i], od_w_out[i], od_ln1_g[i], od_ln1_b[i], od_router_w[i],
                           od_router_b[i], od_moe_w_gate[i], od_moe_w_up[i], od_moe_w_down[i],
                           od_ln2_g[i], od_ln2_b[i])
    return x
```

```python
import functools
import math

import numpy as np
import jax
import jax.numpy as jnp
from jax import lax
from jax.experimental import pallas as pl
from jax.experimental.pallas import tpu as pltpu

F32 = jnp.float32
BF16 = jnp.bfloat16

LN_EPS = 1e-5
DEPTH = 2
ALPHA = (2.0 * DEPTH) ** 0.25

S5_CHANNELS = 512
S5_GROUPS = 32
S5_GROUP = 16
S5_STATE = 64
S5_NSTATE = S5_GROUPS * S5_STATE

RET_HEADS = 4
RET_HEAD_DIM = 128
RET_CHUNK = 128

ATT_HEADS = 16
ATT_HEAD_DIM = 64
ATT_PAIRS = ATT_HEADS // 2
ATT_BLOCK = 128
ATT_SPAN = 128
ATT_DILATIONS = (1, 4, 16)
NEG_BIG = -1e30
LOG2E = math.log2(math.e)
ATT_Q_SCALE = ATT_HEAD_DIM ** -0.5 * LOG2E

D_FF = 2816
N_EXPERTS = 8
TOP_K = 2
MOE_BLOCK = 512

VMEM_LIMIT = 56 * 1024 * 1024


def _cparams(sem):
    return pltpu.CompilerParams(dimension_semantics=sem, vmem_limit_bytes=VMEM_LIMIT)


def _layernorm(y, g, b):
    mu = jnp.mean(y, -1, keepdims=True)
    yc = y - mu
    var = jnp.mean(yc * yc, -1, keepdims=True)
    return yc * lax.rsqrt(var + LN_EPS) * g + b


def _const_spec(shape):
    nd = len(shape)
    return pl.BlockSpec(shape, lambda *_: (0,) * nd)


def _inproj_body(x_ref, w_ref, h_ref):
    xb = x_ref[...].astype(BF16)
    h_ref[...] = jnp.dot(xb, w_ref[...], preferred_element_type=F32).astype(BF16)


def _inproj(x, w):
    bsz, seq, dim = x.shape
    tm = 512
    nh = w.shape[1]
    return pl.pallas_call(
        _inproj_body,
        grid=(bsz, seq // tm),
        in_specs=[pl.BlockSpec((None, tm, dim), lambda b, s: (b, s, 0)),
                  _const_spec(w.shape)],
        out_specs=pl.BlockSpec((None, tm, nh), lambda b, s: (b, s, 0)),
        out_shape=jax.ShapeDtypeStruct((bsz, seq, nh), BF16),
        compiler_params=_cparams(("parallel", "parallel")),
        name="inproj",
    )(x, w)


def _s5_discretize(a_re, a_im, log_dt, b_re, b_im, c_re, c_im, bsz):
    ar = a_re.astype(F32)
    ai = a_im.astype(F32)
    dt = jnp.exp(log_dt.astype(F32))[:, None]
    mag = jnp.exp(ar * dt)
    lr = mag * jnp.cos(ai * dt)
    li = mag * jnp.sin(ai * dt)
    den = ar * ar + ai * ai
    zr = ((lr - 1.0) * ar + li * ai) / den
    zi = (li * ar - (lr - 1.0) * ai) / den
    br = b_re.astype(F32)
    bi = b_im.astype(F32)
    bbr = zr[..., None] * br - zi[..., None] * bi
    bbi = zr[..., None] * bi + zi[..., None] * br
    eye = jnp.eye(S5_GROUPS, dtype=F32)
    nch = S5_CHANNELS
    wbr = jnp.einsum('gpc,gh->gchp', bbr, eye).reshape(nch, S5_NSTATE)
    wbi = jnp.einsum('gpc,gh->gchp', bbi, eye).reshape(nch, S5_NSTATE)
    wb = jnp.concatenate([wbr, wbi], axis=1).astype(BF16)
    wcr = jnp.einsum('gcp,gh->hpgc', c_re.astype(F32), eye).reshape(S5_NSTATE, nch)
    wci = jnp.einsum('gcp,gh->hpgc', c_im.astype(F32), eye).reshape(S5_NSTATE, nch)
    wc = jnp.concatenate([wcr, -wci], axis=0).astype(BF16)
    lam = jnp.stack([jnp.broadcast_to(lr.reshape(1, S5_NSTATE), (bsz, S5_NSTATE)),
                     jnp.broadcast_to(li.reshape(1, S5_NSTATE), (bsz, S5_NSTATE))])
    return wb, wc, lam


def _gelu_tanh(x):
    return 0.5 * x * (1.0 + jnp.tanh(math.sqrt(2.0 / math.pi) * (x + 0.044715 * (x * x * x))))


def _s5_body(u_ref, wb_ref, wc_ref, lam_ref, d_ref, gw_ref, gb_ref, o_ref, xs_ref, st_ref, utb_ref, ys_ref,
             *, lc, bsz, ncb):
    @pl.when(pl.program_id(0) == 0)
    def _():
        st_ref[...] = jnp.zeros_like(st_ref)

    nlb = S5_CHANNELS // 128
    for b in range(bsz):
        ub = u_ref[b].astype(F32)
        for c in range(nlb):
            utb_ref[c, pl.ds(b, lc, stride=bsz), :] = ub[:, c * 128:(c + 1) * 128]
    uf = jnp.concatenate([utb_ref[c] for c in range(nlb)], axis=-1)
    ub = uf.astype(BF16)
    half_c = S5_CHANNELS // 2
    half_s = S5_NSTATE // 2
    for q in range(4):
        hq = q % 2
        xs_ref[:, q * half_s:(q + 1) * half_s] = jnp.dot(
            ub[:, hq * half_c:(hq + 1) * half_c],
            wb_ref[hq * half_c:(hq + 1) * half_c, q * half_s:(q + 1) * half_s],
            preferred_element_type=F32)
    width = S5_NSTATE // ncb
    for cb in range(ncb):
        re = slice(cb * width, (cb + 1) * width)
        im = slice(S5_NSTATE + cb * width, S5_NSTATE + (cb + 1) * width)
        lr = lam_ref[0, :, re]
        li = lam_ref[1, :, re]

        def step(t, carry, re=re, im=im, lr=lr, li=li):
            xr, xi = carry
            row = pl.multiple_of(t * bsz, bsz)
            nxr = lr * xr - li * xi + xs_ref[pl.ds(row, bsz), re]
            nxi = lr * xi + li * xr + xs_ref[pl.ds(row, bsz), im]
            xs_ref[pl.ds(row, bsz), re] = nxr
            xs_ref[pl.ds(row, bsz), im] = nxi
            return nxr, nxi

        xr, xi = lax.fori_loop(0, lc, step, (st_ref[:, re], st_ref[:, im]), unroll=8)
        st_ref[:, re] = xr
        st_ref[:, im] = xi

    ys = []
    for hq in range(2):
        cols = slice(hq * half_c, (hq + 1) * half_c)
        part = None
        for ri in range(2):
            rows = slice((2 * ri + hq) * half_s, (2 * ri + hq + 1) * half_s)
            t = jnp.dot(xs_ref[:, rows].astype(BF16), wc_ref[rows, cols], preferred_element_type=F32)
            part = t if part is None else part + t
        ys.append(part)
    y = jnp.concatenate(ys, axis=-1)
    y = y + d_ref[...] * uf
    y = _gelu_tanh(y)
    z = jnp.dot(y.astype(BF16), gw_ref[...], preferred_element_type=F32) + gb_ref[...]
    yo = y * jax.nn.sigmoid(z)
    for c in range(nlb):
        ys_ref[c] = yo[:, c * 128:(c + 1) * 128]
    for b in range(bsz):
        o_ref[b] = jnp.concatenate([ys_ref[c, pl.ds(b, lc, stride=bsz), :] for c in range(nlb)],
                                   axis=-1).astype(BF16)


def _s5(h, wb, wc, lam, d_skip, glu_w, glu_b):
    bsz, seq, _ = h.shape
    lc = 64
    blk = lc * bsz
    body = functools.partial(_s5_body, lc=lc, bsz=bsz, ncb=4)
    return pl.pallas_call(
        body,
        grid=(seq // lc,),
        in_specs=[pl.BlockSpec((bsz, lc, S5_CHANNELS), lambda s: (0, s, 0)),
                  _const_spec(wb.shape), _const_spec(wc.shape), _const_spec(lam.shape),
                  _const_spec(d_skip.shape), _const_spec(glu_w.shape), _const_spec(glu_b.shape)],
        out_specs=pl.BlockSpec((bsz, lc, S5_CHANNELS), lambda s: (0, s, 0)),
        out_shape=jax.ShapeDtypeStruct((bsz, seq, S5_CHANNELS), BF16),
        scratch_shapes=[pltpu.VMEM((blk, 2 * S5_NSTATE), F32),
                        pltpu.VMEM((bsz, 2 * S5_NSTATE), F32),
                        pltpu.VMEM((S5_CHANNELS // 128, blk, 128), F32),
                        pltpu.VMEM((S5_CHANNELS // 128, blk, 128), F32)],
        compiler_params=_cparams(("arbitrary",)),
        name="s5_mixer",
    )(h, wb, wc, lam, d_skip, glu_w, glu_b)


def _ret_constants():
    hh = np.arange(RET_HEADS, dtype=np.float64)
    log_gamma = np.log1p(-np.exp2(-5.0 - hh))
    pos = np.arange(RET_CHUNK, dtype=np.float64)
    diff = pos[:, None] - pos[None, :]
    scale = RET_HEAD_DIM ** -0.5
    decay = np.where(diff >= 0, np.exp(log_gamma[:, None, None] * np.maximum(diff, 0.0)), 0.0) * scale
    zeta = np.exp(log_gamma[:, None] * (RET_CHUNK - 1.0 - pos)) * scale
    xi = np.exp(log_gamma[:, None] * (pos + 1.0))
    chunk_decay = np.exp(log_gamma * RET_CHUNK)
    zeta_b = np.broadcast_to(zeta[:, :, None], (RET_HEADS, RET_CHUNK, RET_HEAD_DIM))
    xi_b = np.broadcast_to(xi[:, :, None], (RET_HEADS, RET_CHUNK, RET_HEAD_DIM))
    return (decay.astype(np.float32), np.ascontiguousarray(zeta_b).astype(np.float32),
            np.ascontiguousarray(xi_b).astype(np.float32), [float(c) for c in chunk_decay])


def _ret_body(q_ref, k_ref, v_ref, g_ref, dec_ref, zeta_ref, xi_ref, o_ref, st_ref, *, ts, chunk_decay):
    @pl.when(pl.program_id(1) == 0)
    def _():
        st_ref[...] = jnp.zeros_like(st_ref)

    n_chunks = ts // RET_CHUNK
    for h in range(RET_HEADS):
        cols = slice(h * RET_HEAD_DIM, (h + 1) * RET_HEAD_DIM)
        states = [st_ref[h]]
        for c in range(n_chunks):
            rows = slice(c * RET_CHUNK, (c + 1) * RET_CHUNK)
            kz = (k_ref[rows, cols].astype(F32) * zeta_ref[h]).astype(BF16)
            kv = lax.dot_general(kz, v_ref[rows, cols], (((0,), (0,)), ((), ())), preferred_element_type=F32)
            states.append(states[-1] * chunk_decay[h] + kv)
        st_ref[h] = states[n_chunks]
        for c in range(n_chunks):
            rows = slice(c * RET_CHUNK, (c + 1) * RET_CHUNK)
            q = q_ref[rows, cols]
            s = lax.dot_general(q, k_ref[rows, cols], (((1,), (1,)), ((), ())),
                                preferred_element_type=F32) * dec_ref[h]
            inner = jnp.dot(s.astype(BF16), v_ref[rows, cols], preferred_element_type=F32)
            cross = jnp.dot(q, states[c].astype(BF16), preferred_element_type=F32) * xi_ref[h]
            ret = inner + cross
            mu = jnp.mean(ret, -1, keepdims=True)
            rc = ret - mu
            var = jnp.mean(rc * rc, -1, keepdims=True)
            rn = rc * lax.rsqrt(var + LN_EPS)
            gg = g_ref[rows, cols].astype(F32)
            o_ref[rows, cols] = (gg * jax.nn.sigmoid(gg) * rn).astype(BF16)


def _retention(h, first_block):
    bsz, seq, _ = h.shape
    ts = 512
    width = RET_HEADS * RET_HEAD_DIM
    decay, zeta_b, xi_b, chunk_decay = _ret_constants()
    body = functools.partial(_ret_body, ts=ts, chunk_decay=chunk_decay)

    def col(j):
        return pl.BlockSpec((None, ts, width), lambda b, s, j=j: (b, s, j))

    return pl.pallas_call(
        body,
        grid=(bsz, seq // ts),
        in_specs=[col(first_block), col(first_block + 1), col(first_block + 2), col(first_block + 3),
                  _const_spec(decay.shape), _const_spec(zeta_b.shape), _const_spec(xi_b.shape)],
        out_specs=pl.BlockSpec((None, ts, width), lambda b, s: (b, s, 0)),
        out_shape=jax.ShapeDtypeStruct((bsz, seq, width), BF16),
        scratch_shapes=[pltpu.VMEM((RET_HEADS, RET_HEAD_DIM, RET_HEAD_DIM), F32)],
        compiler_params=_cparams(("parallel", "arbitrary")),
        name="retention",
    )(h, h, h, h, jnp.asarray(decay), jnp.asarray(zeta_b), jnp.asarray(xi_b))


FFN_BOUNDS = (0, 1536, 2816)


def _even_tail_body(x_ref, ya_ref, yb_ref, w1_ref, w2_ref, g1_ref, b1_ref,
                    wg_ref, wu_ref, wd_ref, g2_ref, b2_ref, o_ref):
    mix = jnp.dot(ya_ref[...], w1_ref[...], preferred_element_type=F32)
    mix = mix + jnp.dot(yb_ref[...], w2_ref[...], preferred_element_type=F32)
    x = _layernorm(ALPHA * x_ref[...] + mix, g1_ref[...], b1_ref[...])
    xb = x.astype(BF16)
    acc = None
    for c in range(len(FFN_BOUNDS) - 1):
        cs = slice(FFN_BOUNDS[c], FFN_BOUNDS[c + 1])
        gt = jnp.dot(xb, wg_ref[:, cs], preferred_element_type=F32)
        up = jnp.dot(xb, wu_ref[:, cs], preferred_element_type=F32)
        hh = (gt * jax.nn.sigmoid(gt) * up).astype(BF16)
        part = jnp.dot(hh, wd_ref[cs, :], preferred_element_type=F32)
        acc = part if acc is None else acc + part
    y = ALPHA * x + acc
    o_ref[...] = _layernorm(y, g2_ref[...], b2_ref[...])


def _even_tail(x2d, ya2d, yb2d, w1, w2, g1, b1, wg, wu, wd, g2, b2):
    n_tok, dim = x2d.shape
    tm = 512
    single = pl.Buffered(1)

    def wspec(shape):
        return pl.BlockSpec(shape, lambda i: (0, 0), pipeline_mode=single)

    def rows(width):
        return pl.BlockSpec((tm, width), lambda i: (i, 0))

    return pl.pallas_call(
        _even_tail_body,
        grid=(n_tok // tm,),
        in_specs=[rows(dim), rows(ya2d.shape[1]), rows(yb2d.shape[1]),
                  wspec(w1.shape), wspec(w2.shape), _const_spec(g1.shape), _const_spec(b1.shape),
                  wspec(wg.shape), wspec(wu.shape), wspec(wd.shape),
                  _const_spec(g2.shape), _const_spec(b2.shape)],
        out_specs=rows(dim),
        out_shape=jax.ShapeDtypeStruct((n_tok, dim), F32),
        compiler_params=_cparams(("parallel",)),
        name="outproj_ffn_ln",
    )(x2d, ya2d, yb2d, w1, w2, g1, b1, wg, wu, wd, g2, b2)


def _qkv_body(x_ref, w_ref, o_ref):
    xb = x_ref[...].astype(BF16)
    r = jnp.dot(xb, w_ref[...], preferred_element_type=F32)
    nblk = o_ref.shape[0]
    for j in range(nblk):
        blk = r[:, j * 128:(j + 1) * 128]
        if j < ATT_PAIRS:
            blk = blk * ATT_Q_SCALE
        o_ref[j] = blk.astype(BF16)


def _qkv(x, w):
    bsz, seq, dim = x.shape
    tm = 512
    nblk = w.shape[1] // 128
    return pl.pallas_call(
        _qkv_body,
        grid=(bsz, seq // tm),
        in_specs=[pl.BlockSpec((None, tm, dim), lambda b, s: (b, s, 0)),
                  _const_spec(w.shape)],
        out_specs=pl.BlockSpec((None, nblk, tm, 128), lambda b, s: (b, 0, s, 0)),
        out_shape=jax.ShapeDtypeStruct((bsz, nblk, seq, 128), BF16),
        compiler_params=_cparams(("parallel", "parallel")),
        name="qkv_proj",
    )(x, w)


ATT_GROUP = 16


def _att_constants():
    slopes = np.exp2(-8.0 * np.arange(1, ATT_HEADS + 1, dtype=np.float64) / ATT_HEADS)
    sl = np.zeros((ATT_PAIRS, 2 * ATT_BLOCK, 2 * ATT_BLOCK), np.float32)
    for hp in range(ATT_PAIRS):
        sl[hp, :ATT_BLOCK, :] = slopes[2 * hp]
        sl[hp, ATT_BLOCK:, :] = slopes[2 * hp + 1]
    return sl


def _att_unit(q, kw, vw, bias, old, low, ones, last):
    zero = jnp.zeros_like(q)
    qs = jnp.concatenate([jnp.where(low, q, zero), jnp.where(low, zero, q)], axis=0)
    s = lax.dot_general(qs, kw, (((1,), (1,)), ((), ())), preferred_element_type=F32) + bias
    sa = s[:, :128]
    sb = s[:, 128:]
    mrow = jnp.max(jnp.maximum(sa, sb), axis=-1, keepdims=True)
    m_new = jnp.broadcast_to(mrow, (2 * ATT_BLOCK, 128))
    if old is not None:
        m_old, acc_old, l0_old, l1_old = old
        m_new = jnp.maximum(m_old, m_new)
    p = jnp.concatenate([jnp.exp2(sa - m_new), jnp.exp2(sb - m_new)], axis=1).astype(BF16)
    vaug = jnp.concatenate([vw, ones], axis=1)
    res = jnp.dot(p, vaug, preferred_element_type=F32)
    acc = jnp.where(low, res[:ATT_BLOCK, :128], res[ATT_BLOCK:, :128])
    l0 = res[:ATT_BLOCK, 128:]
    l1 = res[ATT_BLOCK:, 128:]
    if old is not None:
        a = jnp.exp2(m_old - m_new)
        a0 = a[:ATT_BLOCK]
        a1 = a[ATT_BLOCK:]
        acc = jnp.where(low, a0, a1) * acc_old + acc
        l0 = a0 * l0_old + l0
        l1 = a1 * l1_old + l1
    if last:
        acc = acc / jnp.where(low, l0, l1)
    return m_new, acc, l0, l1


def _att_body(q_ref, k_ref, v_ref, sl_ref, o_ref,
              tmp_ref, q4_ref, k4_ref, v4_ref, nat_ref, de4_ref, bias_ref, *, seq):
    quarter = seq // 4
    for src, dst in ((q_ref, q4_ref), (k_ref, k4_ref), (v_ref, v4_ref)):
        tmp_ref[...] = src[...].astype(F32)
        for r in range(4):
            dst[r] = tmp_ref[pl.ds(r, quarter, stride=4), :]
    lane = lax.broadcasted_iota(jnp.int32, (ATT_BLOCK, 128), 1)
    low = lane < ATT_HEAD_DIM
    ones = jnp.ones((2 * ATT_BLOCK, 128), BF16)

    qi = lax.broadcasted_iota(jnp.int32, (2 * ATT_BLOCK, 2 * ATT_BLOCK), 0) & (ATT_BLOCK - 1)
    ki = lax.broadcasted_iota(jnp.int32, (2 * ATT_BLOCK, 2 * ATT_BLOCK), 1)
    sl = sl_ref[...]
    for bi, d in enumerate(ATT_DILATIONS):
        for oi in range(2):
            steps = (qi - ki + oi * ATT_BLOCK).astype(F32)
            allowed = (steps >= 0.0) & (steps <= float(ATT_SPAN))
            bias_ref[2 * bi + oi] = jnp.where(allowed, -(sl * (float(d) * LOG2E)) * steps, NEG_BIG)

    n_units = seq // ATT_BLOCK
    for bi, d in enumerate(ATT_DILATIONS):
        first = bi == 0
        last = bi == len(ATT_DILATIONS) - 1
        shift = d.bit_length() - 1

        def group(it, carry, bi=bi, d=d, shift=shift, first=first, last=last):
            loaded = []
            for g in range(ATT_GROUP):
                n = it * ATT_GROUP + g
                i = n >> shift
                r = n & (d - 1)
                iw = jnp.maximum(i - 1, 0)
                if d == 1:
                    qsl = pl.ds(pl.multiple_of(i * ATT_BLOCK, ATT_BLOCK), ATT_BLOCK)
                    wsl = pl.ds(pl.multiple_of(iw * ATT_BLOCK, ATT_BLOCK), 2 * ATT_BLOCK)
                    data = (q_ref[qsl, :], k_ref[wsl, :], v_ref[wsl, :])
                    old_at = None
                    new_at = lambda s, qsl=qsl: nat_ref.at[s, qsl, :]
                else:
                    if d == 4:
                        r4 = r
                        qsl = pl.ds(pl.multiple_of(i * ATT_BLOCK, ATT_BLOCK), ATT_BLOCK)
                        wsl = pl.ds(pl.multiple_of(iw * ATT_BLOCK, ATT_BLOCK), 2 * ATT_BLOCK)
                        osl = pl.ds(r + (4 * ATT_BLOCK) * i, ATT_BLOCK, stride=4)
                        old_at = lambda s, osl=osl: nat_ref.at[s, osl, :]
                    else:
                        r4 = r & 3
                        qsl = pl.ds((r >> 2) + (4 * ATT_BLOCK) * i, ATT_BLOCK, stride=4)
                        wsl = pl.ds((r >> 2) + (4 * ATT_BLOCK) * iw, 2 * ATT_BLOCK, stride=4)
                        old_at = lambda s, r4=r4, qsl=qsl: de4_ref.at[s, r4, qsl, :]
                    data = (q4_ref[r4, qsl, :].astype(BF16), k4_ref[r4, wsl, :].astype(BF16),
                            v4_ref[r4, wsl, :].astype(BF16))
                    new_at = lambda s, r4=r4, qsl=qsl: de4_ref.at[s, r4, qsl, :]
                old = None
                if old_at is not None:
                    old = (jnp.concatenate([old_at(1)[...], old_at(2)[...]], axis=0),
                           old_at(0)[...], old_at(3)[...], old_at(4)[...])
                loaded.append((new_at, data, bias_ref[2 * bi + jnp.minimum(i, 1)], old))
            results = [_att_unit(q, kw, vw, bias, old, low, ones, last)
                       for (_, (q, kw, vw), bias, old) in loaded]
            for (new_at, *_), (m_new, acc, l0, l1) in zip(loaded, results):
                new_at(0)[...] = acc
                if not last:
                    new_at(1)[...] = m_new[:ATT_BLOCK]
                    new_at(2)[...] = m_new[ATT_BLOCK:]
                    new_at(3)[...] = l0
                    new_at(4)[...] = l1
            return carry

        lax.fori_loop(0, n_units // ATT_GROUP, group, 0)

    for r in range(4):
        nat_ref[0, pl.ds(r, quarter, stride=4), :] = de4_ref[0, r]
    o_ref[...] = nat_ref[0].astype(BF16)


def _dilated_attention(qkv):
    bsz, nblk, seq, _ = qkv.shape
    sl = jnp.asarray(_att_constants())
    specs = [pl.BlockSpec((None, None, seq, 128), lambda b, hp, j=j: (b, j * ATT_PAIRS + hp, 0, 0))
             for j in range(3)]
    specs.append(pl.BlockSpec((None,) + sl.shape[1:], lambda b, hp: (hp, 0, 0)))
    body = functools.partial(_att_body, seq=seq)
    return pl.pallas_call(
        body,
        grid=(bsz, ATT_PAIRS),
        in_specs=specs,
        out_specs=pl.BlockSpec((None, None, seq, 128), lambda b, hp: (b, hp, 0, 0)),
        out_shape=jax.ShapeDtypeStruct((bsz, ATT_PAIRS, seq, 128), BF16),
        scratch_shapes=[pltpu.VMEM((seq, 128), F32)]
        + [pltpu.VMEM((4, seq // 4, 128), F32) for _ in range(3)]
        + [pltpu.VMEM((5, seq, 128), F32), pltpu.VMEM((5, 4, seq // 4, 128), F32),
           pltpu.VMEM((2 * len(ATT_DILATIONS), 2 * ATT_BLOCK, 2 * ATT_BLOCK), F32)],
        compiler_params=_cparams(("parallel", "parallel")),
        name="dilated_attention",
    )(qkv, qkv, qkv, sl)


def _outln_odd_body(x_ref, a_ref, w_ref, g_ref, b_ref, rwh_ref, rwl_ref, rb_ref, o_ref, lg_ref):
    a = jnp.concatenate([a_ref[j] for j in range(a_ref.shape[0])], axis=-1)
    acc = jnp.dot(a, w_ref[...], preferred_element_type=F32)
    y = ALPHA * x_ref[...] + acc
    o = _layernorm(y, g_ref[...], b_ref[...])
    o_ref[...] = o
    oh = o.astype(BF16)
    ol = (o - oh.astype(F32)).astype(BF16)
    lg = jnp.dot(oh, rwh_ref[...], preferred_element_type=F32)
    lg = lg + jnp.dot(ol, rwh_ref[...], preferred_element_type=F32)
    lg = lg + jnp.dot(oh, rwl_ref[...], preferred_element_type=F32)
    lg_ref[...] = lg + rb_ref[...]


def _outln_odd(x, att, w, g, b, rw, rb):
    bsz, seq, dim = x.shape
    tm = 512
    npair = att.shape[1]
    rwh = rw.astype(BF16)
    rwl = (rw - rwh.astype(F32)).astype(BF16)
    return pl.pallas_call(
        _outln_odd_body,
        grid=(bsz, seq // tm),
        in_specs=[pl.BlockSpec((None, tm, dim), lambda b_, s: (b_, s, 0)),
                  pl.BlockSpec((None, npair, tm, 128), lambda b_, s: (b_, 0, s, 0)),
                  _const_spec(w.shape), _const_spec(g.shape), _const_spec(b.shape),
                  _const_spec(rw.shape), _const_spec(rw.shape), _const_spec(rb.shape)],
        out_specs=[pl.BlockSpec((None, tm, dim), lambda b_, s: (b_, s, 0)),
                   pl.BlockSpec((None, tm, 128), lambda b_, s: (b_, s, 0))],
        out_shape=[jax.ShapeDtypeStruct((bsz, seq, dim), F32),
                   jax.ShapeDtypeStruct((bsz, seq, 128), F32)],
        compiler_params=_cparams(("parallel", "parallel")),
        name="outproj_ln_odd",
    )(x, att, w, g, b, rwh, rwl, rb)


MOE_FF_BOUNDS = (0, 768, 1536, 2304, 2816)
MOE_NCHUNK = len(MOE_FF_BOUNDS) - 1
MOE_COPY_SHARE = (0.25, 0.25, 0.25, 0.25)
MOE_TRASH_BLOCKS = 2 + 2 * N_EXPERTS
MOE_NBUF = 3


def _route(logits, n_tok):
    top_val, top_idx = lax.top_k(logits, TOP_K)
    gates = jax.nn.softmax(top_val, axis=-1)
    n_assign = n_tok * TOP_K
    exp_flat = top_idx.reshape(n_assign).astype(jnp.int32)
    experts = jnp.arange(N_EXPERTS, dtype=jnp.int32)

    def lookup(table, idx):
        return jnp.sum(jnp.where(idx[:, None] == experts[None, :], table[None, :], 0), axis=1)

    counts = jnp.sum((exp_flat[:, None] == experts[None, :]).astype(jnp.int32), axis=0)
    starts = jnp.cumsum(counts) - counts
    padded_counts = (counts + MOE_BLOCK - 1) // MOE_BLOCK * MOE_BLOCK
    pends = jnp.cumsum(padded_counts)
    pstarts = pends - padded_counts
    order = jnp.argsort(exp_flat).astype(jnp.int32)
    n_blocks = -(-n_assign // MOE_BLOCK) + N_EXPERTS
    n_rows = n_blocks * MOE_BLOCK
    blk_exp = jnp.minimum(jnp.sum((pends[None, :] <= (jnp.arange(n_blocks, dtype=jnp.int32) * MOE_BLOCK)[:, None])
                                  .astype(jnp.int32), axis=1), N_EXPERTS - 1)
    rows = jnp.arange(n_rows, dtype=jnp.int32)
    row_exp = jnp.repeat(blk_exp, MOE_BLOCK)
    local = rows - lookup(pstarts, row_exp)
    valid = local < lookup(counts, row_exp)
    spos = jnp.clip(lookup(starts, row_exp) + local, 0, n_assign - 1)
    n_used = (pends[-1] // MOE_BLOCK).astype(jnp.int32)
    assign = jnp.where(valid, order[spos], -1)
    real_dst = (assign % TOP_K) * n_tok + assign // TOP_K
    in_blk = rows % MOE_BLOCK
    blk = rows // MOE_BLOCK
    trash_blk = jnp.where(blk < n_used, 1 + row_exp, 1 + N_EXPERTS + blk - n_used)
    dst_mid = jnp.where(valid, real_dst, n_assign + trash_blk * MOE_BLOCK + in_blk)
    dst_front = n_assign + jnp.arange(MOE_BLOCK, dtype=jnp.int32)
    back = jnp.zeros(((MOE_NBUF - 1) * MOE_BLOCK,), jnp.int32)
    dst_row = jnp.concatenate([dst_front, dst_mid, back]).reshape(n_blocks + MOE_NBUF, MOE_BLOCK)
    src_row = jnp.concatenate([back[:MOE_BLOCK], jnp.where(valid, assign // TOP_K, 0), back])
    src_row = src_row.reshape(n_blocks + MOE_NBUF, MOE_BLOCK)
    return gates, src_row, dst_row, blk_exp, n_used, n_blocks


def _moe_body(bexp_ref, nused_ref, src_ref, dst_ref, x_hbm, wg_ref, wu_ref, wd_ref, o_hbm,
              xf_ref, xb_ref, yb_ref, gsem, ssem, *, n_blocks, n_real):
    j = pl.program_id(0)
    slot = lax.rem(j, MOE_NBUF)
    slot_m1 = lax.rem(j + MOE_NBUF - 1, MOE_NBUF)
    slot_p1 = lax.rem(j + 1, MOE_NBUF)
    active = j < nused_ref[0]

    def gather_start(blk, n, s):
        tok = src_ref[blk + 1, n]
        pltpu.make_async_copy(x_hbm.at[pl.ds(tok, 1), :], xf_ref.at[s, pl.ds(n, 1), :], gsem.at[s]).start()

    def scatter_start(blk, n, s):
        dst = dst_ref[blk + 1, n]
        pltpu.make_async_copy(yb_ref.at[s, pl.ds(n, 1), :], o_hbm.at[pl.ds(dst, 1), :], ssem.at[s]).start()

    def gather_wait(s):
        pltpu.make_async_copy(x_hbm.at[pl.ds(0, MOE_BLOCK), :], xf_ref.at[s], gsem.at[s]).wait()

    def scatter_wait(s):
        pltpu.make_async_copy(yb_ref.at[s], o_hbm.at[pl.ds(0, MOE_BLOCK), :], ssem.at[s]).wait()

    @pl.when(j == 0)
    def _():
        yb_ref[...] = jnp.zeros_like(yb_ref)
        for t in range(MOE_TRASH_BLOCKS):
            fill = pltpu.make_async_copy(
                yb_ref.at[0], o_hbm.at[pl.ds(n_real + t * MOE_BLOCK, MOE_BLOCK), :], ssem.at[0])
            fill.start()
            fill.wait()

        def issue(n, carry):
            gather_start(0, n, 0)
            gather_start(1, n, 1)
            return carry

        lax.fori_loop(0, MOE_BLOCK, issue, 0)

    gather_wait(slot)

    @pl.when(j >= MOE_NBUF - 1)
    def _():
        scatter_wait(slot)

    @pl.when(active)
    def _():
        xb_ref[...] = xf_ref[slot].astype(BF16)

    edges = [round(MOE_BLOCK * sum(MOE_COPY_SHARE[:c])) for c in range(MOE_NCHUNK)] + [MOE_BLOCK]
    for c in range(MOE_NCHUNK):
        @pl.when(j + c < nused_ref[0] + c)
        def _(c=c):
            for n in range(edges[c], edges[c + 1]):
                gather_start(j + 2, n, slot_m1)
                scatter_start(j - 1, n, slot_m1)
            cs = slice(MOE_FF_BOUNDS[c], MOE_FF_BOUNDS[c + 1])
            xb = xb_ref[...]
            gt = jnp.dot(xb, wg_ref[:, cs], preferred_element_type=F32)
            up = jnp.dot(xb, wu_ref[:, cs], preferred_element_type=F32)
            hh = (gt * jax.nn.sigmoid(gt) * up).astype(BF16)
            part = jnp.dot(hh, wd_ref[cs, :], preferred_element_type=F32)
            if c == 0:
                yb_ref[slot] = part
            else:
                yb_ref[slot] += part

    @pl.when(jnp.logical_not(active))
    def _():
        def issue(n, carry):
            gather_start(j + 2, n, slot_m1)
            scatter_start(j - 1, n, slot_m1)
            return carry

        lax.fori_loop(0, MOE_BLOCK, issue, 0)

    @pl.when(j == n_blocks - 1)
    def _():
        def issue(n, carry):
            scatter_start(j, n, slot)
            return carry

        lax.fori_loop(0, MOE_BLOCK, issue, 0)
        scatter_wait(slot_p1)
        scatter_wait(slot_m1)
        scatter_wait(slot)
        gather_wait(slot_p1)
        gather_wait(slot_m1)


def _moe(x2d, wg, wu, wd, src_row, dst_row, blk_exp, n_used, n_blocks):
    n_tok, dim = x2d.shape
    last = jnp.maximum(n_used - 1, 0)
    jj = jnp.arange(n_blocks, dtype=jnp.int32)
    bexp_eff = blk_exp[jnp.minimum(jj, last)]
    n_real = TOP_K * n_tok
    out_rows = n_real + MOE_TRASH_BLOCKS * MOE_BLOCK

    def wmap(j, bexp, nused, src, dst):
        return (bexp[j], 0, 0)

    grid_spec = pltpu.PrefetchScalarGridSpec(
        num_scalar_prefetch=4,
        grid=(n_blocks,),
        in_specs=[pl.BlockSpec(memory_space=pl.ANY),
                  pl.BlockSpec((None, dim, D_FF), wmap),
                  pl.BlockSpec((None, dim, D_FF), wmap),
                  pl.BlockSpec((None, D_FF, dim), wmap)],
        out_specs=pl.BlockSpec(memory_space=pl.ANY),
        scratch_shapes=[pltpu.VMEM((MOE_NBUF, MOE_BLOCK, dim), F32),
                        pltpu.VMEM((MOE_BLOCK, dim), BF16),
                        pltpu.VMEM((MOE_NBUF, MOE_BLOCK, dim), F32),
                        pltpu.SemaphoreType.DMA((MOE_NBUF,)),
                        pltpu.SemaphoreType.DMA((MOE_NBUF,))],
    )
    body = functools.partial(_moe_body, n_blocks=n_blocks, n_real=n_real)
    return pl.pallas_call(
        body,
        grid_spec=grid_spec,
        out_shape=jax.ShapeDtypeStruct((out_rows, dim), F32),
        compiler_params=pltpu.CompilerParams(dimension_semantics=("arbitrary",),
                                             vmem_limit_bytes=60 * 1024 * 1024),
        name="moe_swiglu",
    )(bexp_eff, n_used.reshape(1), src_row, dst_row, x2d, wg, wu, wd)


def _combine_body(x_ref, gp_ref, y0_ref, y1_ref, g_ref, b_ref, o_ref):
    gp = gp_ref[...]
    moe = gp[:, 0:1] * y0_ref[...] + gp[:, 1:2] * y1_ref[...]
    y = ALPHA * x_ref[...] + moe
    o_ref[...] = _layernorm(y, g_ref[...], b_ref[...])


def _combine(x2d, gates_pad, yrows, g, b):
    n_tok, dim = x2d.shape
    tm = 512
    nt = n_tok // tm
    return pl.pallas_call(
        _combine_body,
        grid=(nt,),
        in_specs=[pl.BlockSpec((tm, dim), lambda i: (i, 0)),
                  pl.BlockSpec((tm, 128), lambda i: (i, 0)),
                  pl.BlockSpec((tm, dim), lambda i: (i, 0)),
                  pl.BlockSpec((tm, dim), lambda i: (nt + i, 0)),
                  _const_spec(g.shape), _const_spec(b.shape)],
        out_specs=pl.BlockSpec((tm, dim), lambda i: (i, 0)),
        out_shape=jax.ShapeDtypeStruct((n_tok, dim), F32),
        compiler_params=_cparams(("parallel",)),
        name="moe_combine_ln",
    )(x2d, gates_pad, yrows, yrows, g, b)


def _row(v):
    return v.astype(F32).reshape(1, -1)


def _even_layer(x, w_in, a_re, a_im, log_dt, b_re, b_im, c_re, c_im, d_skip, glu_w, glu_b,
                w_out, ln1_g, ln1_b, w_gate, w_up, w_down, ln2_g, ln2_b):
    bsz, seq, dim = x.shape
    h = _inproj(x, w_in.astype(BF16))
    wb, wc, lam = _s5_discretize(a_re, a_im, log_dt, b_re, b_im, c_re, c_im, bsz)
    ya = _s5(h, wb, wc, lam, _row(d_skip), glu_w.astype(BF16), _row(glu_b))
    yb = _retention(h, 1)
    w_out = w_out.astype(BF16)
    n_tok = bsz * seq
    x2 = _even_tail(x.reshape(n_tok, dim), ya.reshape(n_tok, -1), yb.reshape(n_tok, -1),
                    w_out[:S5_CHANNELS], w_out[S5_CHANNELS:], _row(ln1_g), _row(ln1_b),
                    w_gate.astype(BF16), w_up.astype(BF16), w_down.astype(BF16), _row(ln2_g), _row(ln2_b))
    return x2.reshape(bsz, seq, dim)


def _odd_layer(x, w_qkv, w_out, ln1_g, ln1_b, router_w, router_b, w_gate, w_up, w_down, ln2_g, ln2_b):
    bsz, seq, dim = x.shape
    n_tok = bsz * seq
    qkv = _qkv(x, w_qkv.astype(BF16))
    att = _dilated_attention(qkv)
    rw = jnp.zeros((dim, 128), F32).at[:, :N_EXPERTS].set(router_w.astype(F32))
    rb = jnp.zeros((1, 128), F32).at[:, :N_EXPERTS].set(router_b.astype(F32)[None, :])
    x1, logits = _outln_odd(x, att, w_out.astype(BF16), _row(ln1_g), _row(ln1_b), rw, rb)
    x1 = x1.reshape(n_tok, dim)
    gates, src_row, dst_row, blk_exp, n_used, n_blocks = _route(logits.reshape(n_tok, 128)[:, :N_EXPERTS], n_tok)
    yrows = _moe(x1, w_gate.astype(BF16), w_up.astype(BF16), w_down.astype(BF16),
                 src_row, dst_row, blk_exp, n_used, n_blocks)
    gates_pad = jnp.zeros((n_tok, 128), F32).at[:, :TOP_K].set(gates)
    out = _combine(x1, gates_pad, yrows, _row(ln2_g), _row(ln2_b))
    return out.reshape(bsz, seq, dim)


def kernel(x, ev_w_in, ev_s5_a_re, ev_s5_a_im, ev_s5_log_dt, ev_s5_b_re, ev_s5_b_im, ev_s5_c_re, ev_s5_c_im, ev_s5_d, ev_s5_glu_w, ev_s5_glu_b, ev_w_out, ev_ln1_g, ev_ln1_b, ev_ffn_w_gate, ev_ffn_w_up, ev_ffn_w_down, ev_ln2_g, ev_ln2_b, od_w_qkv, od_w_out, od_ln1_g, od_ln1_b, od_router_w, od_router_b, od_moe_w_gate, od_moe_w_up, od_moe_w_down, od_ln2_g, od_ln2_b):
    for layer in range(DEPTH):
        i = layer // 2
        if layer % 2 == 0:
            x = _even_layer(x, ev_w_in[i], ev_s5_a_re[i], ev_s5_a_im[i], ev_s5_log_dt[i], ev_s5_b_re[i],
                            ev_s5_b_im[i], ev_s5_c_re[i], ev_s5_c_im[i], ev_s5_d[i], ev_s5_glu_w[i],
                            ev_s5_glu_b[i], ev_w_out[i], ev_ln1_g[i], ev_ln1_b[i], ev_ffn_w_gate[i],
                            ev_ffn_w_up[i], ev_ffn_w_down[i], ev_ln2_g[i], ev_ln2_b[i])
        else:
            x = _odd_layer(x, od_w_qkv[i], od_w_out[i], od_ln1_g[i], od_ln1_b[i], od_router_w[i],
                           od_router_b[i], od_moe_w_gate[i], od_moe_w_up[i], od_moe_w_down[i],
                           od_ln2_g[i], od_ln2_b[i])
    return x
```

```python
import functools
import math

import numpy as np
import jax
import jax.numpy as jnp
from jax import lax
from jax.experimental import pallas as pl
from jax.experimental.pallas import tpu as pltpu

F32 = jnp.float32
BF16 = jnp.bfloat16

LN_EPS = 1e-5
DEPTH = 2
ALPHA = (2.0 * DEPTH) ** 0.25

S5_CHANNELS = 512
S5_GROUPS = 32
S5_GROUP = 16
S5_STATE = 64
S5_NSTATE = S5_GROUPS * S5_STATE

RET_HEADS = 4
RET_HEAD_DIM = 128
RET_CHUNK = 128

ATT_HEADS = 16
ATT_HEAD_DIM = 64
ATT_PAIRS = ATT_HEADS // 2
ATT_BLOCK = 128
ATT_SPAN = 128
ATT_DILATIONS = (1, 4, 16)
NEG_BIG = -1e30
LOG2E = math.log2(math.e)
ATT_Q_SCALE = ATT_HEAD_DIM ** -0.5 * LOG2E

D_FF = 2816
N_EXPERTS = 8
TOP_K = 2
MOE_BLOCK = 512

VMEM_LIMIT = 56 * 1024 * 1024


def _cparams(sem):
    return pltpu.CompilerParams(dimension_semantics=sem, vmem_limit_bytes=VMEM_LIMIT)


def _layernorm(y, g, b):
    mu = jnp.mean(y, -1, keepdims=True)
    yc = y - mu
    var = jnp.mean(yc * yc, -1, keepdims=True)
    return yc * lax.rsqrt(var + LN_EPS) * g + b


def _const_spec(shape):
    nd = len(shape)
    return pl.BlockSpec(shape, lambda *_: (0,) * nd)


def _inproj_body(x_ref, w_ref, h_ref):
    xb = x_ref[...].astype(BF16)
    h_ref[...] = jnp.dot(xb, w_ref[...], preferred_element_type=F32).astype(BF16)


def _inproj(x, w):
    bsz, seq, dim = x.shape
    tm = 512
    nh = w.shape[1]
    return pl.pallas_call(
        _inproj_body,
        grid=(bsz, seq // tm),
        in_specs=[pl.BlockSpec((None, tm, dim), lambda b, s: (b, s, 0)),
                  _const_spec(w.shape)],
        out_specs=pl.BlockSpec((None, tm, nh), lambda b, s: (b, s, 0)),
        out_shape=jax.ShapeDtypeStruct((bsz, seq, nh), BF16),
        compiler_params=_cparams(("parallel", "parallel")),
        name="inproj",
    )(x, w)


def _s5_discretize(a_re, a_im, log_dt, b_re, b_im, c_re, c_im, bsz):
    ar = a_re.astype(F32)
    ai = a_im.astype(F32)
    dt = jnp.exp(log_dt.astype(F32))[:, None]
    mag = jnp.exp(ar * dt)
    lr = mag * jnp.cos(ai * dt)
    li = mag * jnp.sin(ai * dt)
    den = ar * ar + ai * ai
    zr = ((lr - 1.0) * ar + li * ai) / den
    zi = (li * ar - (lr - 1.0) * ai) / den
    br = b_re.astype(F32)
    bi = b_im.astype(F32)
    bbr = zr[..., None] * br - zi[..., None] * bi
    bbi = zr[..., None] * bi + zi[..., None] * br
    eye = jnp.eye(S5_GROUPS, dtype=F32)
    nch = S5_CHANNELS
    wbr = jnp.einsum('gpc,gh->gchp', bbr, eye).reshape(nch, S5_NSTATE)
    wbi = jnp.einsum('gpc,gh->gchp', bbi, eye).reshape(nch, S5_NSTATE)
    wb = jnp.concatenate([wbr, wbi], axis=1).astype(BF16)
    wcr = jnp.einsum('gcp,gh->hpgc', c_re.astype(F32), eye).reshape(S5_NSTATE, nch)
    wci = jnp.einsum('gcp,gh->hpgc', c_im.astype(F32), eye).reshape(S5_NSTATE, nch)
    wc = jnp.concatenate([wcr, -wci], axis=0).astype(BF16)
    lam = jnp.stack([jnp.broadcast_to(lr.reshape(1, S5_NSTATE), (bsz, S5_NSTATE)),
                     jnp.broadcast_to(li.reshape(1, S5_NSTATE), (bsz, S5_NSTATE))])
    return wb, wc, lam


def _gelu_tanh(x):
    return 0.5 * x * (1.0 + jnp.tanh(math.sqrt(2.0 / math.pi) * (x + 0.044715 * (x * x * x))))


def _s5_body(u_ref, wb_ref, wc_ref, lam_ref, d_ref, gw_ref, gb_ref, o_ref, st_ref, utb_ref, ys_ref, *xs_refs,
             lc, bsz):
    ncb = len(xs_refs)

    @pl.when(pl.program_id(0) == 0)
    def _():
        st_ref[...] = jnp.zeros_like(st_ref)

    nlb = S5_CHANNELS // 128
    for b in range(bsz):
        ub = u_ref[b].astype(F32)
        for c in range(nlb):
            utb_ref[c, pl.ds(b, lc, stride=bsz), :] = ub[:, c * 128:(c + 1) * 128]
    uf = jnp.concatenate([utb_ref[c] for c in range(nlb)], axis=-1)
    ub = uf.astype(BF16)
    half_c = S5_CHANNELS // 2
    half_s = S5_NSTATE // 2
    width = S5_NSTATE // ncb
    ys = [None, None]
    for cb in range(ncb):
        xs_ref = xs_refs[cb]
        re = slice(cb * width, (cb + 1) * width)
        im = slice(S5_NSTATE + cb * width, S5_NSTATE + (cb + 1) * width)
        hq = (cb * width) // half_s
        ch = slice(hq * half_c, (hq + 1) * half_c)
        xs_ref[:, :width] = jnp.dot(ub[:, ch], wb_ref[ch, re], preferred_element_type=F32)
        xs_ref[:, width:] = jnp.dot(ub[:, ch], wb_ref[ch, im], preferred_element_type=F32)
        lr = lam_ref[0, :, re]
        li = lam_ref[1, :, re]
        xr = st_ref[:, re]
        xi = st_ref[:, im]
        for t in range(lc):
            rows = slice(t * bsz, (t + 1) * bsz)
            nxr = lr * xr - li * xi + xs_ref[rows, :width]
            nxi = lr * xi + li * xr + xs_ref[rows, width:]
            xs_ref[rows, :width] = nxr
            xs_ref[rows, width:] = nxi
            xr, xi = nxr, nxi
        st_ref[:, re] = xr
        st_ref[:, im] = xi
        part = jnp.dot(xs_ref[:, :width].astype(BF16), wc_ref[re, ch], preferred_element_type=F32)
        part = part + jnp.dot(xs_ref[:, width:].astype(BF16), wc_ref[im, ch], preferred_element_type=F32)
        ys[hq] = part if ys[hq] is None else ys[hq] + part
    y = jnp.concatenate(ys, axis=-1)
    y = y + d_ref[...] * uf
    y = _gelu_tanh(y)
    z = jnp.dot(y.astype(BF16), gw_ref[...], preferred_element_type=F32) + gb_ref[...]
    yo = y * jax.nn.sigmoid(z)
    for c in range(nlb):
        ys_ref[c] = yo[:, c * 128:(c + 1) * 128]
    for b in range(bsz):
        o_ref[b] = jnp.concatenate([ys_ref[c, pl.ds(b, lc, stride=bsz), :] for c in range(nlb)],
                                   axis=-1).astype(BF16)


def _s5(h, wb, wc, lam, d_skip, glu_w, glu_b):
    bsz, seq, _ = h.shape
    lc = 64
    blk = lc * bsz
    ncb = 4
    body = functools.partial(_s5_body, lc=lc, bsz=bsz)
    return pl.pallas_call(
        body,
        grid=(seq // lc,),
        in_specs=[pl.BlockSpec((bsz, lc, S5_CHANNELS), lambda s: (0, s, 0)),
                  _const_spec(wb.shape), _const_spec(wc.shape), _const_spec(lam.shape),
                  _const_spec(d_skip.shape), _const_spec(glu_w.shape), _const_spec(glu_b.shape)],
        out_specs=pl.BlockSpec((bsz, lc, S5_CHANNELS), lambda s: (0, s, 0)),
        out_shape=jax.ShapeDtypeStruct((bsz, seq, S5_CHANNELS), BF16),
        scratch_shapes=[pltpu.VMEM((bsz, 2 * S5_NSTATE), F32),
                        pltpu.VMEM((S5_CHANNELS // 128, blk, 128), F32),
                        pltpu.VMEM((S5_CHANNELS // 128, blk, 128), F32)]
        + [pltpu.VMEM((blk, 2 * S5_NSTATE // ncb), F32) for _ in range(ncb)],
        compiler_params=_cparams(("arbitrary",)),
        name="s5_mixer",
    )(h, wb, wc, lam, d_skip, glu_w, glu_b)


def _ret_constants():
    hh = np.arange(RET_HEADS, dtype=np.float64)
    log_gamma = np.log1p(-np.exp2(-5.0 - hh))
    pos = np.arange(RET_CHUNK, dtype=np.float64)
    diff = pos[:, None] - pos[None, :]
    scale = RET_HEAD_DIM ** -0.5
    decay = np.where(diff >= 0, np.exp(log_gamma[:, None, None] * np.maximum(diff, 0.0)), 0.0) * scale
    zeta = np.exp(log_gamma[:, None] * (RET_CHUNK - 1.0 - pos)) * scale
    xi = np.exp(log_gamma[:, None] * (pos + 1.0))
    chunk_decay = np.exp(log_gamma * RET_CHUNK)
    zeta_b = np.broadcast_to(zeta[:, :, None], (RET_HEADS, RET_CHUNK, RET_HEAD_DIM))
    xi_b = np.broadcast_to(xi[:, :, None], (RET_HEADS, RET_CHUNK, RET_HEAD_DIM))
    return (decay.astype(np.float32), np.ascontiguousarray(zeta_b).astype(np.float32),
            np.ascontiguousarray(xi_b).astype(np.float32), [float(c) for c in chunk_decay])


def _ret_body(q_ref, k_ref, v_ref, g_ref, dec_ref, zeta_ref, xi_ref, o_ref, st_ref, *, ts, chunk_decay):
    @pl.when(pl.program_id(1) == 0)
    def _():
        st_ref[...] = jnp.zeros_like(st_ref)

    n_chunks = ts // RET_CHUNK
    for h in range(RET_HEADS):
        cols = slice(h * RET_HEAD_DIM, (h + 1) * RET_HEAD_DIM)
        states = [st_ref[h]]
        for c in range(n_chunks):
            rows = slice(c * RET_CHUNK, (c + 1) * RET_CHUNK)
            kz = (k_ref[rows, cols].astype(F32) * zeta_ref[h]).astype(BF16)
            kv = lax.dot_general(kz, v_ref[rows, cols], (((0,), (0,)), ((), ())), preferred_element_type=F32)
            states.append(states[-1] * chunk_decay[h] + kv)
        st_ref[h] = states[n_chunks]
        for c in range(n_chunks):
            rows = slice(c * RET_CHUNK, (c + 1) * RET_CHUNK)
            q = q_ref[rows, cols]
            s = lax.dot_general(q, k_ref[rows, cols], (((1,), (1,)), ((), ())),
                                preferred_element_type=F32) * dec_ref[h]
            inner = jnp.dot(s.astype(BF16), v_ref[rows, cols], preferred_element_type=F32)
            cross = jnp.dot(q, states[c].astype(BF16), preferred_element_type=F32) * xi_ref[h]
            ret = inner + cross
            mu = jnp.mean(ret, -1, keepdims=True)
            rc = ret - mu
            var = jnp.mean(rc * rc, -1, keepdims=True)
            rn = rc * lax.rsqrt(var + LN_EPS)
            gg = g_ref[rows, cols].astype(F32)
            o_ref[rows, cols] = (gg * jax.nn.sigmoid(gg) * rn).astype(BF16)


def _retention(h, first_block):
    bsz, seq, _ = h.shape
    ts = 512
    width = RET_HEADS * RET_HEAD_DIM
    decay, zeta_b, xi_b, chunk_decay = _ret_constants()
    body = functools.partial(_ret_body, ts=ts, chunk_decay=chunk_decay)

    def col(j):
        return pl.BlockSpec((None, ts, width), lambda b, s, j=j: (b, s, j))

    return pl.pallas_call(
        body,
        grid=(bsz, seq // ts),
        in_specs=[col(first_block), col(first_block + 1), col(first_block + 2), col(first_block + 3),
                  _const_spec(decay.shape), _const_spec(zeta_b.shape), _const_spec(xi_b.shape)],
        out_specs=pl.BlockSpec((None, ts, width), lambda b, s: (b, s, 0)),
        out_shape=jax.ShapeDtypeStruct((bsz, seq, width), BF16),
        scratch_shapes=[pltpu.VMEM((RET_HEADS, RET_HEAD_DIM, RET_HEAD_DIM), F32)],
        compiler_params=_cparams(("parallel", "arbitrary")),
        name="retention",
    )(h, h, h, h, jnp.asarray(decay), jnp.asarray(zeta_b), jnp.asarray(xi_b))


FFN_BOUNDS = (0, 1536, 2816)


def _even_tail_body(x_ref, ya_ref, yb_ref, w1_ref, w2_ref, g1_ref, b1_ref,
                    wg_ref, wu_ref, wd_ref, g2_ref, b2_ref, o_ref):
    mix = jnp.dot(ya_ref[...], w1_ref[...], preferred_element_type=F32)
    mix = mix + jnp.dot(yb_ref[...], w2_ref[...], preferred_element_type=F32)
    x = _layernorm(ALPHA * x_ref[...] + mix, g1_ref[...], b1_ref[...])
    xb = x.astype(BF16)
    acc = None
    for c in range(len(FFN_BOUNDS) - 1):
        cs = slice(FFN_BOUNDS[c], FFN_BOUNDS[c + 1])
        gt = jnp.dot(xb, wg_ref[:, cs], preferred_element_type=F32)
        up = jnp.dot(xb, wu_ref[:, cs], preferred_element_type=F32)
        hh = (gt * jax.nn.sigmoid(gt) * up).astype(BF16)
        part = jnp.dot(hh, wd_ref[cs, :], preferred_element_type=F32)
        acc = part if acc is None else acc + part
    y = ALPHA * x + acc
    o_ref[...] = _layernorm(y, g2_ref[...], b2_ref[...])


def _even_tail(x2d, ya2d, yb2d, w1, w2, g1, b1, wg, wu, wd, g2, b2):
    n_tok, dim = x2d.shape
    tm = 512
    single = pl.Buffered(1)

    def wspec(shape):
        return pl.BlockSpec(shape, lambda i: (0, 0), pipeline_mode=single)

    def rows(width):
        return pl.BlockSpec((tm, width), lambda i: (i, 0))

    return pl.pallas_call(
        _even_tail_body,
        grid=(n_tok // tm,),
        in_specs=[rows(dim), rows(ya2d.shape[1]), rows(yb2d.shape[1]),
                  wspec(w1.shape), wspec(w2.shape), _const_spec(g1.shape), _const_spec(b1.shape),
                  wspec(wg.shape), wspec(wu.shape), wspec(wd.shape),
                  _const_spec(g2.shape), _const_spec(b2.shape)],
        out_specs=rows(dim),
        out_shape=jax.ShapeDtypeStruct((n_tok, dim), F32),
        compiler_params=_cparams(("parallel",)),
        name="outproj_ffn_ln",
    )(x2d, ya2d, yb2d, w1, w2, g1, b1, wg, wu, wd, g2, b2)


def _qkv_body(x_ref, w_ref, o_ref):
    xb = x_ref[...].astype(BF16)
    r = jnp.dot(xb, w_ref[...], preferred_element_type=F32)
    nblk = o_ref.shape[0]
    for j in range(nblk):
        blk = r[:, j * 128:(j + 1) * 128]
        if j < ATT_PAIRS:
            blk = blk * ATT_Q_SCALE
        o_ref[j] = blk.astype(BF16)


def _qkv(x, w):
    bsz, seq, dim = x.shape
    tm = 512
    nblk = w.shape[1] // 128
    return pl.pallas_call(
        _qkv_body,
        grid=(bsz, seq // tm),
        in_specs=[pl.BlockSpec((None, tm, dim), lambda b, s: (b, s, 0)),
                  _const_spec(w.shape)],
        out_specs=pl.BlockSpec((None, nblk, tm, 128), lambda b, s: (b, 0, s, 0)),
        out_shape=jax.ShapeDtypeStruct((bsz, nblk, seq, 128), BF16),
        compiler_params=_cparams(("parallel", "parallel")),
        name="qkv_proj",
    )(x, w)


ATT_GROUP = 16


def _att_bias_tiles():
    slopes = jnp.exp2(-8.0 * jnp.arange(1, ATT_HEADS + 1, dtype=F32) / ATT_HEADS).reshape(ATT_PAIRS, 2)
    slope_rows = jnp.repeat(slopes, ATT_BLOCK, axis=1)[:, None, :, None]
    qi = jnp.arange(2 * ATT_BLOCK, dtype=jnp.int32)[:, None] % ATT_BLOCK
    ki = jnp.arange(2 * ATT_BLOCK, dtype=jnp.int32)[None, :]
    tiles = []
    for d in ATT_DILATIONS:
        for oi in range(2):
            steps = (qi - ki + oi * ATT_BLOCK).astype(F32)[None, None]
            allowed = (steps >= 0.0) & (steps <= float(ATT_SPAN))
            tiles.append(jnp.where(allowed, -(slope_rows * (float(d) * LOG2E)) * steps, NEG_BIG))
    return jnp.concatenate(tiles, axis=1)


def _att_unit(q, kw, vw, bias, old, low, ones, last):
    zero = jnp.zeros_like(q)
    qs = jnp.concatenate([jnp.where(low, q, zero), jnp.where(low, zero, q)], axis=0)
    s = lax.dot_general(qs, kw, (((1,), (1,)), ((), ())), preferred_element_type=F32) + bias
    sa = s[:, :128]
    sb = s[:, 128:]
    mrow = jnp.max(jnp.maximum(sa, sb), axis=-1, keepdims=True)
    m_new = jnp.broadcast_to(mrow, (2 * ATT_BLOCK, 128))
    if old is not None:
        m_old, acc_old, l0_old, l1_old = old
        m_new = jnp.maximum(m_old, m_new)
    p = jnp.concatenate([jnp.exp2(sa - m_new), jnp.exp2(sb - m_new)], axis=1).astype(BF16)
    vaug = jnp.concatenate([vw, ones], axis=1)
    res = jnp.dot(p, vaug, preferred_element_type=F32)
    acc = jnp.where(low, res[:ATT_BLOCK, :128], res[ATT_BLOCK:, :128])
    l0 = res[:ATT_BLOCK, 128:]
    l1 = res[ATT_BLOCK:, 128:]
    if old is not None:
        a = jnp.exp2(m_old - m_new)
        a0 = a[:ATT_BLOCK]
        a1 = a[ATT_BLOCK:]
        acc = jnp.where(low, a0, a1) * acc_old + acc
        l0 = a0 * l0_old + l0
        l1 = a1 * l1_old + l1
    if last:
        acc = acc / jnp.where(low, l0, l1)
    return m_new, acc, l0, l1


def _att_body(q_ref, k_ref, v_ref, bias_ref, o_ref,
              tmp_ref, q4_ref, k4_ref, v4_ref, nat_ref, de4_ref, *, seq):
    quarter = seq // 4
    for src, dst in ((q_ref, q4_ref), (k_ref, k4_ref), (v_ref, v4_ref)):
        tmp_ref[...] = src[...].astype(F32)
        for r in range(4):
            dst[r] = tmp_ref[pl.ds(r, quarter, stride=4), :]
    lane = lax.broadcasted_iota(jnp.int32, (ATT_BLOCK, 128), 1)
    low = lane < ATT_HEAD_DIM
    ones = jnp.ones((2 * ATT_BLOCK, 128), BF16)

    n_units = seq // ATT_BLOCK
    for bi, d in enumerate(ATT_DILATIONS):
        first = bi == 0
        last = bi == len(ATT_DILATIONS) - 1
        shift = d.bit_length() - 1

        def group(it, carry, bi=bi, d=d, shift=shift, first=first, last=last):
            loaded = []
            for g in range(ATT_GROUP):
                n = it * ATT_GROUP + g
                i = n >> shift
                r = n & (d - 1)
                iw = jnp.maximum(i - 1, 0)
                if d == 1:
                    qsl = pl.ds(pl.multiple_of(i * ATT_BLOCK, ATT_BLOCK), ATT_BLOCK)
                    wsl = pl.ds(pl.multiple_of(iw * ATT_BLOCK, ATT_BLOCK), 2 * ATT_BLOCK)
                    data = (q_ref[qsl, :], k_ref[wsl, :], v_ref[wsl, :])
                    old_at = None
                    new_at = lambda s, qsl=qsl: nat_ref.at[s, qsl, :]
                else:
                    if d == 4:
                        r4 = r
                        qsl = pl.ds(pl.multiple_of(i * ATT_BLOCK, ATT_BLOCK), ATT_BLOCK)
                        wsl = pl.ds(pl.multiple_of(iw * ATT_BLOCK, ATT_BLOCK), 2 * ATT_BLOCK)
                        osl = pl.ds(r + (4 * ATT_BLOCK) * i, ATT_BLOCK, stride=4)
                        old_at = lambda s, osl=osl: nat_ref.at[s, osl, :]
                    else:
                        r4 = r & 3
                        qsl = pl.ds((r >> 2) + (4 * ATT_BLOCK) * i, ATT_BLOCK, stride=4)
                        wsl = pl.ds((r >> 2) + (4 * ATT_BLOCK) * iw, 2 * ATT_BLOCK, stride=4)
                        old_at = lambda s, r4=r4, qsl=qsl: de4_ref.at[s, r4, qsl, :]
                    data = (q4_ref[r4, qsl, :].astype(BF16), k4_ref[r4, wsl, :].astype(BF16),
                            v4_ref[r4, wsl, :].astype(BF16))
                    new_at = lambda s, r4=r4, qsl=qsl: de4_ref.at[s, r4, qsl, :]
                old = None
                if old_at is not None:
                    old = (jnp.concatenate([old_at(1)[...], old_at(2)[...]], axis=0),
                           old_at(0)[...], old_at(3)[...], old_at(4)[...])
                loaded.append((new_at, data, bias_ref[2 * bi + jnp.minimum(i, 1)], old))
            results = [_att_unit(q, kw, vw, bias, old, low, ones, last)
                       for (_, (q, kw, vw), bias, old) in loaded]
            for (new_at, *_), (m_new, acc, l0, l1) in zip(loaded, results):
                new_at(0)[...] = acc
                if not last:
                    new_at(1)[...] = m_new[:ATT_BLOCK]
                    new_at(2)[...] = m_new[ATT_BLOCK:]
                    new_at(3)[...] = l0
                    new_at(4)[...] = l1
            return carry

        lax.fori_loop(0, n_units // ATT_GROUP, group, 0)

    for r in range(4):
        nat_ref[0, pl.ds(r, quarter, stride=4), :] = de4_ref[0, r]
    o_ref[...] = nat_ref[0].astype(BF16)


def _dilated_attention(qkv):
    bsz, nblk, seq, _ = qkv.shape
    bias = _att_bias_tiles()
    specs = [pl.BlockSpec((None, None, seq, 128), lambda hp, b, j=j: (b, j * ATT_PAIRS + hp, 0, 0))
             for j in range(3)]
    specs.append(pl.BlockSpec((None,) + bias.shape[1:], lambda hp, b: (hp, 0, 0, 0)))
    body = functools.partial(_att_body, seq=seq)
    return pl.pallas_call(
        body,
        grid=(ATT_PAIRS, bsz),
        in_specs=specs,
        out_specs=pl.BlockSpec((None, None, seq, 128), lambda hp, b: (b, hp, 0, 0)),
        out_shape=jax.ShapeDtypeStruct((bsz, ATT_PAIRS, seq, 128), BF16),
        scratch_shapes=[pltpu.VMEM((seq, 128), F32)]
        + [pltpu.VMEM((4, seq // 4, 128), F32) for _ in range(3)]
        + [pltpu.VMEM((5, seq, 128), F32), pltpu.VMEM((5, 4, seq // 4, 128), F32)],
        compiler_params=_cparams(("parallel", "parallel")),
        name="dilated_attention",
    )(qkv, qkv, qkv, bias)


def _outln_odd_body(x_ref, a_ref, w_ref, g_ref, b_ref, rwh_ref, rwl_ref, rb_ref, o_ref, lg_ref):
    a = jnp.concatenate([a_ref[j] for j in range(a_ref.shape[0])], axis=-1)
    acc = jnp.dot(a, w_ref[...], preferred_element_type=F32)
    y = ALPHA * x_ref[...] + acc
    o = _layernorm(y, g_ref[...], b_ref[...])
    o_ref[...] = o
    oh = o.astype(BF16)
    ol = (o - oh.astype(F32)).astype(BF16)
    lg = jnp.dot(oh, rwh_ref[...], preferred_element_type=F32)
    lg = lg + jnp.dot(ol, rwh_ref[...], preferred_element_type=F32)
    lg = lg + jnp.dot(oh, rwl_ref[...], preferred_element_type=F32)
    lg_ref[...] = lg + rb_ref[...]


def _outln_odd(x, att, w, g, b, rw, rb):
    bsz, seq, dim = x.shape
    tm = 512
    npair = att.shape[1]
    rwh = rw.astype(BF16)
    rwl = (rw - rwh.astype(F32)).astype(BF16)
    return pl.pallas_call(
        _outln_odd_body,
        grid=(bsz, seq // tm),
        in_specs=[pl.BlockSpec((None, tm, dim), lambda b_, s: (b_, s, 0)),
                  pl.BlockSpec((None, npair, tm, 128), lambda b_, s: (b_, 0, s, 0)),
                  _const_spec(w.shape), _const_spec(g.shape), _const_spec(b.shape),
                  _const_spec(rw.shape), _const_spec(rw.shape), _const_spec(rb.shape)],
        out_specs=[pl.BlockSpec((None, tm, dim), lambda b_, s: (b_, s, 0)),
                   pl.BlockSpec((None, tm, 128), lambda b_, s: (b_, s, 0))],
        out_shape=[jax.ShapeDtypeStruct((bsz, seq, dim), F32),
                   jax.ShapeDtypeStruct((bsz, seq, 128), F32)],
        compiler_params=_cparams(("parallel", "parallel")),
        name="outproj_ln_odd",
    )(x, att, w, g, b, rwh, rwl, rb)


MOE_FF_BOUNDS = (0, 768, 1536, 2304, 2816)
MOE_NCHUNK = len(MOE_FF_BOUNDS) - 1
MOE_COPY_SHARE = (0.25, 0.25, 0.25, 0.25)
MOE_TRASH_BLOCKS = 2 + 2 * N_EXPERTS
MOE_NBUF = 3


def _route(logits, n_tok):
    top_val, top_idx = lax.top_k(logits, TOP_K)
    gates = jax.nn.softmax(top_val, axis=-1)
    n_assign = n_tok * TOP_K
    exp_flat = top_idx.reshape(n_assign).astype(jnp.int32)
    experts = jnp.arange(N_EXPERTS, dtype=jnp.int32)

    def lookup(table, idx):
        return jnp.sum(jnp.where(idx[:, None] == experts[None, :], table[None, :], 0), axis=1)

    counts = jnp.sum((exp_flat[:, None] == experts[None, :]).astype(jnp.int32), axis=0)
    starts = jnp.cumsum(counts) - counts
    padded_counts = (counts + MOE_BLOCK - 1) // MOE_BLOCK * MOE_BLOCK
    pends = jnp.cumsum(padded_counts)
    pstarts = pends - padded_counts
    order = jnp.argsort(exp_flat).astype(jnp.int32)
    n_blocks = -(-n_assign // MOE_BLOCK) + N_EXPERTS
    n_rows = n_blocks * MOE_BLOCK
    blk_exp = jnp.minimum(jnp.sum((pends[None, :] <= (jnp.arange(n_blocks, dtype=jnp.int32) * MOE_BLOCK)[:, None])
                                  .astype(jnp.int32), axis=1), N_EXPERTS - 1)
    rows = jnp.arange(n_rows, dtype=jnp.int32)
    row_exp = jnp.repeat(blk_exp, MOE_BLOCK)
    local = rows - lookup(pstarts, row_exp)
    valid = local < lookup(counts, row_exp)
    spos = jnp.clip(lookup(starts, row_exp) + local, 0, n_assign - 1)
    n_used = (pends[-1] // MOE_BLOCK).astype(jnp.int32)
    assign = jnp.where(valid, order[spos], -1)
    real_dst = (assign % TOP_K) * n_tok + assign // TOP_K
    in_blk = rows % MOE_BLOCK
    blk = rows // MOE_BLOCK
    trash_blk = jnp.where(blk < n_used, 1 + row_exp, 1 + N_EXPERTS + blk - n_used)
    dst_mid = jnp.where(valid, real_dst, n_assign + trash_blk * MOE_BLOCK + in_blk)
    dst_front = n_assign + jnp.arange(MOE_BLOCK, dtype=jnp.int32)
    back = jnp.zeros(((MOE_NBUF - 1) * MOE_BLOCK,), jnp.int32)
    dst_row = jnp.concatenate([dst_front, dst_mid, back]).reshape(n_blocks + MOE_NBUF, MOE_BLOCK)
    src_row = jnp.concatenate([back[:MOE_BLOCK], jnp.where(valid, assign // TOP_K, 0), back])
    src_row = src_row.reshape(n_blocks + MOE_NBUF, MOE_BLOCK)
    return gates, src_row, dst_row, blk_exp, n_used, n_blocks


def _moe_body(bexp_ref, nused_ref, src_ref, dst_ref, x_hbm, wg_ref, wu_ref, wd_ref, o_hbm,
              xf_ref, xb_ref, yb_ref, gsem, ssem, *, n_blocks, n_real):
    j = pl.program_id(0)
    slot = lax.rem(j, MOE_NBUF)
    slot_m1 = lax.rem(j + MOE_NBUF - 1, MOE_NBUF)
    slot_p1 = lax.rem(j + 1, MOE_NBUF)
    active = j < nused_ref[0]

    def gather_start(blk, n, s):
        tok = src_ref[blk + 1, n]
        pltpu.make_async_copy(x_hbm.at[pl.ds(tok, 1), :], xf_ref.at[s, pl.ds(n, 1), :], gsem.at[s]).start()

    def scatter_start(blk, n, s):
        dst = dst_ref[blk + 1, n]
        pltpu.make_async_copy(yb_ref.at[s, pl.ds(n, 1), :], o_hbm.at[pl.ds(dst, 1), :], ssem.at[s]).start()

    def gather_wait(s):
        pltpu.make_async_copy(x_hbm.at[pl.ds(0, MOE_BLOCK), :], xf_ref.at[s], gsem.at[s]).wait()

    def scatter_wait(s):
        pltpu.make_async_copy(yb_ref.at[s], o_hbm.at[pl.ds(0, MOE_BLOCK), :], ssem.at[s]).wait()

    @pl.when(j == 0)
    def _():
        yb_ref[...] = jnp.zeros_like(yb_ref)
        fills = [pltpu.make_async_copy(
            yb_ref.at[0], o_hbm.at[pl.ds(n_real + t * MOE_BLOCK, MOE_BLOCK), :], ssem.at[0])
            for t in range(MOE_TRASH_BLOCKS)]
        for fill in fills:
            fill.start()
        for fill in fills:
            fill.wait()

        def issue(n, carry):
            gather_start(0, n, 0)
            gather_start(1, n, 1)
            return carry

        lax.fori_loop(0, MOE_BLOCK, issue, 0)

    gather_wait(slot)

    @pl.when(j >= MOE_NBUF - 1)
    def _():
        scatter_wait(slot)

    @pl.when(active)
    def _():
        xb_ref[...] = xf_ref[slot].astype(BF16)

    edges = [round(MOE_BLOCK * sum(MOE_COPY_SHARE[:c])) for c in range(MOE_NCHUNK)] + [MOE_BLOCK]
    for c in range(MOE_NCHUNK):
        @pl.when(j + c < nused_ref[0] + c)
        def _(c=c):
            for n in range(edges[c], edges[c + 1]):
                gather_start(j + 2, n, slot_m1)
                scatter_start(j - 1, n, slot_m1)
            cs = slice(MOE_FF_BOUNDS[c], MOE_FF_BOUNDS[c + 1])
            xb = xb_ref[...]
            gt = jnp.dot(xb, wg_ref[:, cs], preferred_element_type=F32)
            up = jnp.dot(xb, wu_ref[:, cs], preferred_element_type=F32)
            hh = (gt * jax.nn.sigmoid(gt) * up).astype(BF16)
            part = jnp.dot(hh, wd_ref[cs, :], preferred_element_type=F32)
            if c == 0:
                yb_ref[slot] = part
            else:
                yb_ref[slot] += part

    @pl.when(jnp.logical_not(active))
    def _():
        def issue(n, carry):
            gather_start(j + 2, n, slot_m1)
            scatter_start(j - 1, n, slot_m1)
            return carry

        lax.fori_loop(0, MOE_BLOCK, issue, 0)

    @pl.when(j == n_blocks - 1)
    def _():
        def issue(n, carry):
            scatter_start(j, n, slot)
            return carry

        lax.fori_loop(0, MOE_BLOCK, issue, 0)
        scatter_wait(slot_p1)
        scatter_wait(slot_m1)
        scatter_wait(slot)
        gather_wait(slot_p1)
        gather_wait(slot_m1)


def _moe(x2d, wg, wu, wd, src_row, dst_row, blk_exp, n_used, n_blocks):
    n_tok, dim = x2d.shape
    last = jnp.maximum(n_used - 1, 0)
    jj = jnp.arange(n_blocks, dtype=jnp.int32)
    bexp_eff = blk_exp[jnp.minimum(jj, last)]
    n_real = TOP_K * n_tok
    out_rows = n_real + MOE_TRASH_BLOCKS * MOE_BLOCK

    def wmap(j, bexp, nused, src, dst):
        return (bexp[j], 0, 0)

    grid_spec = pltpu.PrefetchScalarGridSpec(
        num_scalar_prefetch=4,
        grid=(n_blocks,),
        in_specs=[pl.BlockSpec(memory_space=pl.ANY),
                  pl.BlockSpec((None, dim, D_FF), wmap),
                  pl.BlockSpec((None, dim, D_FF), wmap),
                  pl.BlockSpec((None, D_FF, dim), wmap)],
        out_specs=pl.BlockSpec(memory_space=pl.ANY),
        scratch_shapes=[pltpu.VMEM((MOE_NBUF, MOE_BLOCK, dim), F32),
                        pltpu.VMEM((MOE_BLOCK, dim), BF16),
                        pltpu.VMEM((MOE_NBUF, MOE_BLOCK, dim), F32),
                        pltpu.SemaphoreType.DMA((MOE_NBUF,)),
                        pltpu.SemaphoreType.DMA((MOE_NBUF,))],
    )
    body = functools.partial(_moe_body, n_blocks=n_blocks, n_real=n_real)
    return pl.pallas_call(
        body,
        grid_spec=grid_spec,
        out_shape=jax.ShapeDtypeStruct((out_rows, dim), F32),
        compiler_params=pltpu.CompilerParams(dimension_semantics=("arbitrary",),
                                             vmem_limit_bytes=60 * 1024 * 1024),
        name="moe_swiglu",
    )(bexp_eff, n_used.reshape(1), src_row, dst_row, x2d, wg, wu, wd)


def _combine_body(x_ref, gp_ref, y0_ref, y1_ref, g_ref, b_ref, o_ref):
    gp = gp_ref[...]
    moe = gp[:, 0:1] * y0_ref[...] + gp[:, 1:2] * y1_ref[...]
    y = ALPHA * x_ref[...] + moe
    o_ref[...] = _layernorm(y, g_ref[...], b_ref[...])


def _combine(x2d, gates_pad, yrows, g, b):
    n_tok, dim = x2d.shape
    tm = 512
    nt = n_tok // tm
    return pl.pallas_call(
        _combine_body,
        grid=(nt,),
        in_specs=[pl.BlockSpec((tm, dim), lambda i: (i, 0)),
                  pl.BlockSpec((tm, 128), lambda i: (i, 0)),
                  pl.BlockSpec((tm, dim), lambda i: (i, 0)),
                  pl.BlockSpec((tm, dim), lambda i: (nt + i, 0)),
                  _const_spec(g.shape), _const_spec(b.shape)],
        out_specs=pl.BlockSpec((tm, dim), lambda i: (i, 0)),
        out_shape=jax.ShapeDtypeStruct((n_tok, dim), F32),
        compiler_params=_cparams(("parallel",)),
        name="moe_combine_ln",
    )(x2d, gates_pad, yrows, yrows, g, b)


def _row(v):
    return v.astype(F32).reshape(1, -1)


def _even_layer(x, w_in, a_re, a_im, log_dt, b_re, b_im, c_re, c_im, d_skip, glu_w, glu_b,
                w_out, ln1_g, ln1_b, w_gate, w_up, w_down, ln2_g, ln2_b):
    bsz, seq, dim = x.shape
    h = _inproj(x, w_in.astype(BF16))
    wb, wc, lam = _s5_discretize(a_re, a_im, log_dt, b_re, b_im, c_re, c_im, bsz)
    ya = _s5(h, wb, wc, lam, _row(d_skip), glu_w.astype(BF16), _row(glu_b))
    yb = _retention(h, 1)
    w_out = w_out.astype(BF16)
    n_tok = bsz * seq
    x2 = _even_tail(x.reshape(n_tok, dim), ya.reshape(n_tok, -1), yb.reshape(n_tok, -1),
                    w_out[:S5_CHANNELS], w_out[S5_CHANNELS:], _row(ln1_g), _row(ln1_b),
                    w_gate.astype(BF16), w_up.astype(BF16), w_down.astype(BF16), _row(ln2_g), _row(ln2_b))
    return x2.reshape(bsz, seq, dim)


def _odd_layer(x, w_qkv, w_out, ln1_g, ln1_b, router_w, router_b, w_gate, w_up, w_down, ln2_g, ln2_b):
    bsz, seq, dim = x.shape
    n_tok = bsz * seq
    qkv = _qkv(x, w_qkv.astype(BF16))
    att = _dilated_attention(qkv)
    rw = jnp.zeros((dim, 128), F32).at[:, :N_EXPERTS].set(router_w.astype(F32))
    rb = jnp.zeros((1, 128), F32).at[:, :N_EXPERTS].set(router_b.astype(F32)[None, :])
    x1, logits = _outln_odd(x, att, w_out.astype(BF16), _row(ln1_g), _row(ln1_b), rw, rb)
    x1 = x1.reshape(n_tok, dim)
    gates, src_row, dst_row, blk_exp, n_used, n_blocks = _route(logits.reshape(n_tok, 128)[:, :N_EXPERTS], n_tok)
    yrows = _moe(x1, w_gate.astype(BF16), w_up.astype(BF16), w_down.astype(BF16),
                 src_row, dst_row, blk_exp, n_used, n_blocks)
    gates_pad = jnp.zeros((n_tok, 128), F32).at[:, :TOP_K].set(gates)
    out = _combine(x1, gates_pad, yrows, _row(ln2_g), _row(ln2_b))
    return out.reshape(bsz, seq, dim)


def kernel(x, ev_w_in, ev_s5_a_re, ev_s5_a_im, ev_s5_log_dt, ev_s5_b_re, ev_s5_b_im, ev_s5_c_re, ev_s5_c_im, ev_s5_d, ev_s5_glu_w, ev_s5_glu_b, ev_w_out, ev_ln1_g, ev_ln1_b, ev_ffn_w_gate, ev_ffn_w_up, ev_ffn_w_down, ev_ln2_g, ev_ln2_b, od_w_qkv, od_w_out, od_ln1_g, od_ln1_b, od_router_w, od_router_b, od_moe_w_gate, od_moe_w_up, od_moe_w_down, od_ln2_g, od_ln2_b):
    for layer in range(DEPTH):
        i = layer // 2
        if layer % 2 == 0:
            x = _even_layer(x, ev_w_in[i], ev_s5_a_re[i], ev_s5_a_im[i], ev_s5_log_dt[i], ev_s5_b_re[i],
                            ev_s5_b_im[i], ev_s5_c_re[i], ev_s5_c_im[i], ev_s5_d[i], ev_s5_glu_w[i],
                            ev_s5_glu_b[i], ev_w_out[i], ev_ln1_g[i], ev_ln1_b[i], ev_ffn_w_gate[i],
                            ev_ffn_w_up[i], ev_ffn_w_down[i], ev_ln2_g[i], ev_ln2_b[i])
        else:
            x = _odd_layer(x, od_w_qkv[i], od_w_out[i], od_ln1_g[i], od_ln1_b[i], od_router_w[i],
                           od_router_b[i], od_moe_w_gate[i], od_moe_w_up[i], od_moe_w_down[i],
                           od_ln2_g[i], od_ln2_b[i])
    return x
```

```python
import functools
import math

import numpy as np
import jax
import jax.numpy as jnp
from jax import lax
from jax.experimental import pallas as pl
from jax.experimental.pallas import tpu as pltpu

F32 = jnp.float32
BF16 = jnp.bfloat16

LN_EPS = 1e-5
DEPTH = 2
ALPHA = (2.0 * DEPTH) ** 0.25

S5_CHANNELS = 512
S5_GROUPS = 32
S5_GROUP = 16
S5_STATE = 64
S5_NSTATE = S5_GROUPS * S5_STATE

RET_HEADS = 4
RET_HEAD_DIM = 128
RET_CHUNK = 128

ATT_HEADS = 16
ATT_HEAD_DIM = 64
ATT_PAIRS = ATT_HEADS // 2
ATT_BLOCK = 128
ATT_SPAN = 128
ATT_DILATIONS = (1, 4, 16)
NEG_BIG = -1e30
LOG2E = math.log2(math.e)
ATT_Q_SCALE = ATT_HEAD_DIM ** -0.5 * LOG2E

D_FF = 2816
N_EXPERTS = 8
TOP_K = 2
MOE_BLOCK = 512

VMEM_LIMIT = 56 * 1024 * 1024


def _cparams(sem):
    return pltpu.CompilerParams(dimension_semantics=sem, vmem_limit_bytes=VMEM_LIMIT)


def _layernorm(y, g, b):
    mu = jnp.mean(y, -1, keepdims=True)
    yc = y - mu
    var = jnp.mean(yc * yc, -1, keepdims=True)
    return yc * lax.rsqrt(var + LN_EPS) * g + b


def _const_spec(shape):
    nd = len(shape)
    return pl.BlockSpec(shape, lambda *_: (0,) * nd)


def _inproj_body(x_ref, w_ref, h_ref):
    xb = x_ref[...].astype(BF16)
    h_ref[...] = jnp.dot(xb, w_ref[...], preferred_element_type=F32).astype(BF16)


def _inproj(x, w):
    bsz, seq, dim = x.shape
    tm = 512
    nh = w.shape[1]
    return pl.pallas_call(
        _inproj_body,
        grid=(bsz, seq // tm),
        in_specs=[pl.BlockSpec((None, tm, dim), lambda b, s: (b, s, 0)),
                  _const_spec(w.shape)],
        out_specs=pl.BlockSpec((None, tm, nh), lambda b, s: (b, s, 0)),
        out_shape=jax.ShapeDtypeStruct((bsz, seq, nh), BF16),
        compiler_params=_cparams(("parallel", "parallel")),
        name="inproj",
    )(x, w)


def _s5_discretize(a_re, a_im, log_dt, b_re, b_im, c_re, c_im, bsz):
    ar = a_re.astype(F32)
    ai = a_im.astype(F32)
    dt = jnp.exp(log_dt.astype(F32))[:, None]
    mag = jnp.exp(ar * dt)
    lr = mag * jnp.cos(ai * dt)
    li = mag * jnp.sin(ai * dt)
    den = ar * ar + ai * ai
    zr = ((lr - 1.0) * ar + li * ai) / den
    zi = (li * ar - (lr - 1.0) * ai) / den
    br = b_re.astype(F32)
    bi = b_im.astype(F32)
    bbr = zr[..., None] * br - zi[..., None] * bi
    bbi = zr[..., None] * bi + zi[..., None] * br
    eye = jnp.eye(S5_GROUPS, dtype=F32)
    nch = S5_CHANNELS
    wbr = jnp.einsum('gpc,gh->gchp', bbr, eye).reshape(nch, S5_NSTATE)
    wbi = jnp.einsum('gpc,gh->gchp', bbi, eye).reshape(nch, S5_NSTATE)
    wb = jnp.concatenate([wbr, wbi], axis=1).astype(BF16)
    wcr = jnp.einsum('gcp,gh->hpgc', c_re.astype(F32), eye).reshape(S5_NSTATE, nch)
    wci = jnp.einsum('gcp,gh->hpgc', c_im.astype(F32), eye).reshape(S5_NSTATE, nch)
    wc = jnp.concatenate([wcr, -wci], axis=0).astype(BF16)
    lam = jnp.stack([jnp.broadcast_to(lr.reshape(1, S5_NSTATE), (bsz, S5_NSTATE)),
                     jnp.broadcast_to(li.reshape(1, S5_NSTATE), (bsz, S5_NSTATE))])
    return wb, wc, lam


def _gelu_tanh(x):
    return 0.5 * x * (1.0 + jnp.tanh(math.sqrt(2.0 / math.pi) * (x + 0.044715 * (x * x * x))))


def _s5_body(u_ref, wb_ref, wc_ref, lam_ref, d_ref, gw_ref, gb_ref, o_ref, st_ref, utb_ref, ys_ref, *xs_refs,
             lc, bsz):
    ncb = len(xs_refs)

    @pl.when(pl.program_id(0) == 0)
    def _():
        st_ref[...] = jnp.zeros_like(st_ref)

    nlb = S5_CHANNELS // 128
    for b in range(bsz):
        ub = u_ref[b].astype(F32)
        for c in range(nlb):
            utb_ref[c, pl.ds(b, lc, stride=bsz), :] = ub[:, c * 128:(c + 1) * 128]
    uf = jnp.concatenate([utb_ref[c] for c in range(nlb)], axis=-1)
    ub = uf.astype(BF16)
    half_c = S5_CHANNELS // 2
    half_s = S5_NSTATE // 2
    width = S5_NSTATE // ncb
    ys = [None, None]
    for cb in range(ncb):
        xs_ref = xs_refs[cb]
        re = slice(cb * width, (cb + 1) * width)
        im = slice(S5_NSTATE + cb * width, S5_NSTATE + (cb + 1) * width)
        hq = (cb * width) // half_s
        ch = slice(hq * half_c, (hq + 1) * half_c)
        xs_ref[:, :width] = jnp.dot(ub[:, ch], wb_ref[ch, re], preferred_element_type=F32)
        xs_ref[:, width:] = jnp.dot(ub[:, ch], wb_ref[ch, im], preferred_element_type=F32)
        lr = lam_ref[0, :, re]
        li = lam_ref[1, :, re]
        xr = st_ref[:, re]
        xi = st_ref[:, im]
        for t in range(lc):
            rows = slice(t * bsz, (t + 1) * bsz)
            nxr = lr * xr - li * xi + xs_ref[rows, :width]
            nxi = lr * xi + li * xr + xs_ref[rows, width:]
            xs_ref[rows, :width] = nxr
            xs_ref[rows, width:] = nxi
            xr, xi = nxr, nxi
        st_ref[:, re] = xr
        st_ref[:, im] = xi
        part = jnp.dot(xs_ref[:, :width].astype(BF16), wc_ref[re, ch], preferred_element_type=F32)
        part = part + jnp.dot(xs_ref[:, width:].astype(BF16), wc_ref[im, ch], preferred_element_type=F32)
        ys[hq] = part if ys[hq] is None else ys[hq] + part
    y = jnp.concatenate(ys, axis=-1)
    y = y + d_ref[...] * uf
    y = _gelu_tanh(y)
    z = jnp.dot(y.astype(BF16), gw_ref[...], preferred_element_type=F32) + gb_ref[...]
    yo = y * jax.nn.sigmoid(z)
    for c in range(nlb):
        ys_ref[c] = yo[:, c * 128:(c + 1) * 128]
    for b in range(bsz):
        o_ref[b] = jnp.concatenate([ys_ref[c, pl.ds(b, lc, stride=bsz), :] for c in range(nlb)],
                                   axis=-1).astype(BF16)


def _s5(h, wb, wc, lam, d_skip, glu_w, glu_b):
    bsz, seq, _ = h.shape
    lc = 64
    blk = lc * bsz
    ncb = 4
    body = functools.partial(_s5_body, lc=lc, bsz=bsz)
    return pl.pallas_call(
        body,
        grid=(seq // lc,),
        in_specs=[pl.BlockSpec((bsz, lc, S5_CHANNELS), lambda s: (0, s, 0)),
                  _const_spec(wb.shape), _const_spec(wc.shape), _const_spec(lam.shape),
                  _const_spec(d_skip.shape), _const_spec(glu_w.shape), _const_spec(glu_b.shape)],
        out_specs=pl.BlockSpec((bsz, lc, S5_CHANNELS), lambda s: (0, s, 0)),
        out_shape=jax.ShapeDtypeStruct((bsz, seq, S5_CHANNELS), BF16),
        scratch_shapes=[pltpu.VMEM((bsz, 2 * S5_NSTATE), F32),
                        pltpu.VMEM((S5_CHANNELS // 128, blk, 128), F32),
                        pltpu.VMEM((S5_CHANNELS // 128, blk, 128), F32)]
        + [pltpu.VMEM((blk, 2 * S5_NSTATE // ncb), F32) for _ in range(ncb)],
        compiler_params=_cparams(("arbitrary",)),
        name="s5_mixer",
    )(h, wb, wc, lam, d_skip, glu_w, glu_b)


def _ret_constants():
    hh = np.arange(RET_HEADS, dtype=np.float64)
    log_gamma = np.log1p(-np.exp2(-5.0 - hh))
    pos = np.arange(RET_CHUNK, dtype=np.float64)
    diff = pos[:, None] - pos[None, :]
    scale = RET_HEAD_DIM ** -0.5
    decay = np.where(diff >= 0, np.exp(log_gamma[:, None, None] * np.maximum(diff, 0.0)), 0.0) * scale
    zeta = np.exp(log_gamma[:, None] * (RET_CHUNK - 1.0 - pos)) * scale
    xi = np.exp(log_gamma[:, None] * (pos + 1.0))
    chunk_decay = np.exp(log_gamma * RET_CHUNK)
    zeta_b = np.broadcast_to(zeta[:, :, None], (RET_HEADS, RET_CHUNK, RET_HEAD_DIM))
    xi_b = np.broadcast_to(xi[:, :, None], (RET_HEADS, RET_CHUNK, RET_HEAD_DIM))
    return (decay.astype(np.float32), np.ascontiguousarray(zeta_b).astype(np.float32),
            np.ascontiguousarray(xi_b).astype(np.float32), [float(c) for c in chunk_decay])


def _ret_body(q_ref, k_ref, v_ref, g_ref, dec_ref, zeta_ref, xi_ref, o_ref, st_ref, *, ts, chunk_decay):
    @pl.when(pl.program_id(1) == 0)
    def _():
        st_ref[...] = jnp.zeros_like(st_ref)

    n_chunks = ts // RET_CHUNK
    for h in range(RET_HEADS):
        cols = slice(h * RET_HEAD_DIM, (h + 1) * RET_HEAD_DIM)
        states = [st_ref[h]]
        for c in range(n_chunks):
            rows = slice(c * RET_CHUNK, (c + 1) * RET_CHUNK)
            kz = (k_ref[rows, cols].astype(F32) * zeta_ref[h]).astype(BF16)
            kv = lax.dot_general(kz, v_ref[rows, cols], (((0,), (0,)), ((), ())), preferred_element_type=F32)
            states.append(states[-1] * chunk_decay[h] + kv)
        st_ref[h] = states[n_chunks]
        for c in range(n_chunks):
            rows = slice(c * RET_CHUNK, (c + 1) * RET_CHUNK)
            q = q_ref[rows, cols]
            s = lax.dot_general(q, k_ref[rows, cols], (((1,), (1,)), ((), ())),
                                preferred_element_type=F32) * dec_ref[h]
            inner = jnp.dot(s.astype(BF16), v_ref[rows, cols], preferred_element_type=F32)
            cross = jnp.dot(q, states[c].astype(BF16), preferred_element_type=F32) * xi_ref[h]
            ret = inner + cross
            mu = jnp.mean(ret, -1, keepdims=True)
            rc = ret - mu
            var = jnp.mean(rc * rc, -1, keepdims=True)
            rn = rc * lax.rsqrt(var + LN_EPS)
            gg = g_ref[rows, cols].astype(F32)
            o_ref[rows, cols] = (gg * jax.nn.sigmoid(gg) * rn).astype(BF16)


def _retention(h, first_block):
    bsz, seq, _ = h.shape
    ts = 512
    width = RET_HEADS * RET_HEAD_DIM
    decay, zeta_b, xi_b, chunk_decay = _ret_constants()
    body = functools.partial(_ret_body, ts=ts, chunk_decay=chunk_decay)

    def col(j):
        return pl.BlockSpec((None, ts, width), lambda b, s, j=j: (b, s, j))

    return pl.pallas_call(
        body,
        grid=(bsz, seq // ts),
        in_specs=[col(first_block), col(first_block + 1), col(first_block + 2), col(first_block + 3),
                  _const_spec(decay.shape), _const_spec(zeta_b.shape), _const_spec(xi_b.shape)],
        out_specs=pl.BlockSpec((None, ts, width), lambda b, s: (b, s, 0)),
        out_shape=jax.ShapeDtypeStruct((bsz, seq, width), BF16),
        scratch_shapes=[pltpu.VMEM((RET_HEADS, RET_HEAD_DIM, RET_HEAD_DIM), F32)],
        compiler_params=_cparams(("parallel", "arbitrary")),
        name="retention",
    )(h, h, h, h, jnp.asarray(decay), jnp.asarray(zeta_b), jnp.asarray(xi_b))


FFN_BOUNDS = (0, 1536, 2816)


def _even_tail_body(x_ref, ya_ref, yb_ref, w1_ref, w2_ref, g1_ref, b1_ref,
                    wg_ref, wu_ref, wd_ref, g2_ref, b2_ref, o_ref):
    mix = jnp.dot(ya_ref[...], w1_ref[...], preferred_element_type=F32)
    mix = mix + jnp.dot(yb_ref[...], w2_ref[...], preferred_element_type=F32)
    x = _layernorm(ALPHA * x_ref[...] + mix, g1_ref[...], b1_ref[...])
    xb = x.astype(BF16)
    acc = None
    for c in range(len(FFN_BOUNDS) - 1):
        cs = slice(FFN_BOUNDS[c], FFN_BOUNDS[c + 1])
        gt = jnp.dot(xb, wg_ref[:, cs], preferred_element_type=F32)
        up = jnp.dot(xb, wu_ref[:, cs], preferred_element_type=F32)
        hh = (gt * jax.nn.sigmoid(gt) * up).astype(BF16)
        part = jnp.dot(hh, wd_ref[cs, :], preferred_element_type=F32)
        acc = part if acc is None else acc + part
    y = ALPHA * x + acc
    o_ref[...] = _layernorm(y, g2_ref[...], b2_ref[...])


def _even_tail(x2d, ya2d, yb2d, w1, w2, g1, b1, wg, wu, wd, g2, b2):
    n_tok, dim = x2d.shape
    tm = 512
    single = pl.Buffered(1)

    def wspec(shape):
        return pl.BlockSpec(shape, lambda i: (0, 0), pipeline_mode=single)

    def rows(width):
        return pl.BlockSpec((tm, width), lambda i: (i, 0))

    return pl.pallas_call(
        _even_tail_body,
        grid=(n_tok // tm,),
        in_specs=[rows(dim), rows(ya2d.shape[1]), rows(yb2d.shape[1]),
                  wspec(w1.shape), wspec(w2.shape), _const_spec(g1.shape), _const_spec(b1.shape),
                  wspec(wg.shape), wspec(wu.shape), wspec(wd.shape),
                  _const_spec(g2.shape), _const_spec(b2.shape)],
        out_specs=rows(dim),
        out_shape=jax.ShapeDtypeStruct((n_tok, dim), F32),
        compiler_params=_cparams(("parallel",)),
        name="outproj_ffn_ln",
    )(x2d, ya2d, yb2d, w1, w2, g1, b1, wg, wu, wd, g2, b2)


def _qkv_body(x_ref, w_ref, o_ref):
    xb = x_ref[...].astype(BF16)
    r = jnp.dot(xb, w_ref[...], preferred_element_type=F32)
    nblk = o_ref.shape[0]
    for j in range(nblk):
        blk = r[:, j * 128:(j + 1) * 128]
        if j < ATT_PAIRS:
            blk = blk * ATT_Q_SCALE
        o_ref[j] = blk.astype(BF16)


def _qkv(x, w):
    bsz, seq, dim = x.shape
    tm = 512
    nblk = w.shape[1] // 128
    return pl.pallas_call(
        _qkv_body,
        grid=(bsz, seq // tm),
        in_specs=[pl.BlockSpec((None, tm, dim), lambda b, s: (b, s, 0)),
                  _const_spec(w.shape)],
        out_specs=pl.BlockSpec((None, nblk, tm, 128), lambda b, s: (b, 0, s, 0)),
        out_shape=jax.ShapeDtypeStruct((bsz, nblk, seq, 128), BF16),
        compiler_params=_cparams(("parallel", "parallel")),
        name="qkv_proj",
    )(x, w)


ATT_GROUP = 16


def _att_bias_tiles():
    slopes = jnp.exp2(-8.0 * jnp.arange(1, ATT_HEADS + 1, dtype=F32) / ATT_HEADS).reshape(ATT_PAIRS, 2)
    slope_rows = jnp.repeat(slopes, ATT_BLOCK, axis=1)[:, None, :, None]
    qi = jnp.arange(2 * ATT_BLOCK, dtype=jnp.int32)[:, None] % ATT_BLOCK
    ki = jnp.arange(2 * ATT_BLOCK, dtype=jnp.int32)[None, :]
    tiles = []
    for d in ATT_DILATIONS:
        for oi in range(2):
            steps = (qi - ki + oi * ATT_BLOCK).astype(F32)[None, None]
            allowed = (steps >= 0.0) & (steps <= float(ATT_SPAN))
            tiles.append(jnp.where(allowed, -(slope_rows * (float(d) * LOG2E)) * steps, NEG_BIG))
    return jnp.concatenate(tiles, axis=1)


def _stack_heads(q, low):
    zero = jnp.zeros_like(q)
    return jnp.concatenate([jnp.where(low, q, zero), jnp.where(low, zero, q)], axis=0)


def _att_unit(qs, kw, vw, bias, old, low, ones, last):
    s = lax.dot_general(qs, kw, (((1,), (1,)), ((), ())), preferred_element_type=F32) + bias
    sa = s[:, :128]
    sb = s[:, 128:]
    mrow = jnp.max(jnp.maximum(sa, sb), axis=-1, keepdims=True)
    m_new = jnp.broadcast_to(mrow, (2 * ATT_BLOCK, 128))
    if old is not None:
        m_old, acc_old, den_old = old
        m_new = jnp.maximum(m_old, m_new)
    p = jnp.concatenate([jnp.exp2(sa - m_new), jnp.exp2(sb - m_new)], axis=1).astype(BF16)
    vaug = jnp.concatenate([vw, ones], axis=1)
    res = jnp.dot(p, vaug, preferred_element_type=F32)
    acc = jnp.where(low, res[:ATT_BLOCK, :128], res[ATT_BLOCK:, :128])
    den = jnp.where(low, res[:ATT_BLOCK, 128:], res[ATT_BLOCK:, 128:])
    if old is not None:
        a = jnp.exp2(m_old - m_new)
        a_pair = jnp.where(low, a[:ATT_BLOCK], a[ATT_BLOCK:])
        acc = a_pair * acc_old + acc
        den = a_pair * den_old + den
    if last:
        acc = acc / den
    return m_new, acc, den


def _att_body(q_ref, k_ref, v_ref, bias_ref, o_ref,
              tmp_ref, q4_ref, k4_ref, v4_ref, qs4_ref, k4b_ref, v4b_ref, nat_ref, de4_ref, *, seq):
    quarter = seq // 4
    for src, dst in ((q_ref, q4_ref), (k_ref, k4_ref), (v_ref, v4_ref)):
        tmp_ref[...] = src[...].astype(F32)
        for r in range(4):
            dst[r] = tmp_ref[pl.ds(r, quarter, stride=4), :]
    lane = lax.broadcasted_iota(jnp.int32, (ATT_BLOCK, 128), 1)
    low = lane < ATT_HEAD_DIM
    ones = jnp.ones((2 * ATT_BLOCK, 128), BF16)
    low_all = lax.broadcasted_iota(jnp.int32, (quarter, 128), 1) < ATT_HEAD_DIM
    for r in range(4):
        q4 = q4_ref[r].astype(BF16)
        qs4_ref[0, r] = jnp.where(low_all, q4, jnp.zeros_like(q4))
        qs4_ref[1, r] = jnp.where(low_all, jnp.zeros_like(q4), q4)
        k4b_ref[r] = k4_ref[r].astype(BF16)
        v4b_ref[r] = v4_ref[r].astype(BF16)

    n_units = seq // ATT_BLOCK
    for bi, d in enumerate(ATT_DILATIONS):
        first = bi == 0
        last = bi == len(ATT_DILATIONS) - 1
        shift = d.bit_length() - 1

        def group(it, carry, bi=bi, d=d, shift=shift, first=first, last=last):
            loaded = []
            for g in range(ATT_GROUP):
                n = it * ATT_GROUP + g
                i = n >> shift
                r = n & (d - 1)
                iw = jnp.maximum(i - 1, 0)
                if d == 1:
                    qsl = pl.ds(pl.multiple_of(i * ATT_BLOCK, ATT_BLOCK), ATT_BLOCK)
                    wsl = pl.ds(pl.multiple_of(iw * ATT_BLOCK, ATT_BLOCK), 2 * ATT_BLOCK)
                    data = (_stack_heads(q_ref[qsl, :], low), k_ref[wsl, :], v_ref[wsl, :])
                    old_at = None
                    new_at = lambda s, qsl=qsl: nat_ref.at[s, qsl, :]
                elif d == 4:
                    qsl = pl.ds(pl.multiple_of(i * ATT_BLOCK, ATT_BLOCK), ATT_BLOCK)
                    wsl = pl.ds(pl.multiple_of(iw * ATT_BLOCK, ATT_BLOCK), 2 * ATT_BLOCK)
                    osl = pl.ds(r + (4 * ATT_BLOCK) * i, ATT_BLOCK, stride=4)
                    data = (jnp.concatenate([qs4_ref[0, r, qsl, :], qs4_ref[1, r, qsl, :]], axis=0),
                            k4b_ref[r, wsl, :], v4b_ref[r, wsl, :])
                    old_at = lambda s, osl=osl: nat_ref.at[s, osl, :]
                    new_at = lambda s, r=r, qsl=qsl: de4_ref.at[s, r, qsl, :]
                else:
                    r4 = r & 3
                    qsl = pl.ds((r >> 2) + (4 * ATT_BLOCK) * i, ATT_BLOCK, stride=4)
                    wsl = pl.ds((r >> 2) + (4 * ATT_BLOCK) * iw, 2 * ATT_BLOCK, stride=4)
                    data = (_stack_heads(q4_ref[r4, qsl, :].astype(BF16), low),
                            k4_ref[r4, wsl, :].astype(BF16), v4_ref[r4, wsl, :].astype(BF16))
                    old_at = lambda s, r4=r4, qsl=qsl: de4_ref.at[s, r4, qsl, :]
                    new_at = old_at
                old = None
                if old_at is not None:
                    old = (jnp.concatenate([old_at(1)[...], old_at(2)[...]], axis=0),
                           old_at(0)[...], old_at(3)[...])
                loaded.append((new_at, data, bias_ref[2 * bi + jnp.minimum(i, 1)], old))
            results = [_att_unit(qs, kw, vw, bias, old, low, ones, last)
                       for (_, (qs, kw, vw), bias, old) in loaded]
            for (new_at, *_), (m_new, acc, den) in zip(loaded, results):
                new_at(0)[...] = acc
                if not last:
                    new_at(1)[...] = m_new[:ATT_BLOCK]
                    new_at(2)[...] = m_new[ATT_BLOCK:]
                    new_at(3)[...] = den
            return carry

        lax.fori_loop(0, n_units // ATT_GROUP, group, 0)

    for r in range(4):
        nat_ref[0, pl.ds(r, quarter, stride=4), :] = de4_ref[0, r]
    o_ref[...] = nat_ref[0].astype(BF16)


def _dilated_attention(qkv):
    bsz, nblk, seq, _ = qkv.shape
    bias = _att_bias_tiles()
    specs = [pl.BlockSpec((None, None, seq, 128), lambda hp, b, j=j: (b, j * ATT_PAIRS + hp, 0, 0))
             for j in range(3)]
    specs.append(pl.BlockSpec((None,) + bias.shape[1:], lambda hp, b: (hp, 0, 0, 0)))
    body = functools.partial(_att_body, seq=seq)
    return pl.pallas_call(
        body,
        grid=(ATT_PAIRS, bsz),
        in_specs=specs,
        out_specs=pl.BlockSpec((None, None, seq, 128), lambda hp, b: (b, hp, 0, 0)),
        out_shape=jax.ShapeDtypeStruct((bsz, ATT_PAIRS, seq, 128), BF16),
        scratch_shapes=[pltpu.VMEM((seq, 128), F32)]
        + [pltpu.VMEM((4, seq // 4, 128), F32) for _ in range(3)]
        + [pltpu.VMEM((2, 4, seq // 4, 128), BF16),
           pltpu.VMEM((4, seq // 4, 128), BF16), pltpu.VMEM((4, seq // 4, 128), BF16),
           pltpu.VMEM((4, seq, 128), F32), pltpu.VMEM((4, 4, seq // 4, 128), F32)],
        compiler_params=_cparams(("parallel", "parallel")),
        name="dilated_attention",
    )(qkv, qkv, qkv, bias)


def _outln_odd_body(x_ref, a_ref, w_ref, g_ref, b_ref, rw_ref, rb_ref, o_ref, lg_ref):
    a = jnp.concatenate([a_ref[j] for j in range(a_ref.shape[0])], axis=-1)
    acc = jnp.dot(a, w_ref[...], preferred_element_type=F32)
    y = ALPHA * x_ref[...] + acc
    o = _layernorm(y, g_ref[...], b_ref[...])
    o_ref[...] = o
    nl = lg_ref.shape[-1]
    oh = o.astype(BF16)
    ol = (o - oh.astype(F32)).astype(BF16)
    both = jnp.dot(oh, rw_ref[...], preferred_element_type=F32)
    lg = both[:, :nl] + both[:, nl:] + jnp.dot(ol, rw_ref[:, :nl], preferred_element_type=F32)
    lg_ref[...] = lg + rb_ref[...]


def _outln_odd(x, att, w, g, b, rw, rb):
    bsz, seq, dim = x.shape
    tm = 512
    npair = att.shape[1]
    rwh = rw.astype(BF16)
    rwl = (rw - rwh.astype(F32)).astype(BF16)
    rw2 = jnp.concatenate([rwh, rwl], axis=1)
    return pl.pallas_call(
        _outln_odd_body,
        grid=(bsz, seq // tm),
        in_specs=[pl.BlockSpec((None, tm, dim), lambda b_, s: (b_, s, 0)),
                  pl.BlockSpec((None, npair, tm, 128), lambda b_, s: (b_, 0, s, 0)),
                  _const_spec(w.shape), _const_spec(g.shape), _const_spec(b.shape),
                  _const_spec(rw2.shape), _const_spec(rb.shape)],
        out_specs=[pl.BlockSpec((None, tm, dim), lambda b_, s: (b_, s, 0)),
                   pl.BlockSpec((None, tm, 128), lambda b_, s: (b_, s, 0))],
        out_shape=[jax.ShapeDtypeStruct((bsz, seq, dim), F32),
                   jax.ShapeDtypeStruct((bsz, seq, 128), F32)],
        compiler_params=_cparams(("parallel", "parallel")),
        name="outproj_ln_odd",
    )(x, att, w, g, b, rw2, rb)


MOE_FF_BOUNDS = (0, 768, 1536, 2304, 2816)
MOE_NCHUNK = len(MOE_FF_BOUNDS) - 1
MOE_COPY_SHARE = (0.25, 0.25, 0.25, 0.25)
MOE_TRASH_BLOCKS = 2 + 2 * N_EXPERTS
MOE_NBUF = 3


def _route(logits, n_tok):
    top_val, top_idx = lax.top_k(logits, TOP_K)
    gates = jax.nn.softmax(top_val, axis=-1)
    n_assign = n_tok * TOP_K
    exp_flat = top_idx.reshape(n_assign).astype(jnp.int32)
    experts = jnp.arange(N_EXPERTS, dtype=jnp.int32)

    def lookup(table, idx):
        return jnp.sum(jnp.where(idx[:, None] == experts[None, :], table[None, :], 0), axis=1)

    counts = jnp.sum((exp_flat[:, None] == experts[None, :]).astype(jnp.int32), axis=0)
    starts = jnp.cumsum(counts) - counts
    padded_counts = (counts + MOE_BLOCK - 1) // MOE_BLOCK * MOE_BLOCK
    pends = jnp.cumsum(padded_counts)
    pstarts = pends - padded_counts
    order = jnp.argsort(exp_flat).astype(jnp.int32)
    n_blocks = -(-n_assign // MOE_BLOCK) + N_EXPERTS
    n_rows = n_blocks * MOE_BLOCK
    blk_exp = jnp.minimum(jnp.sum((pends[None, :] <= (jnp.arange(n_blocks, dtype=jnp.int32) * MOE_BLOCK)[:, None])
                                  .astype(jnp.int32), axis=1), N_EXPERTS - 1)
    rows = jnp.arange(n_rows, dtype=jnp.int32)
    row_exp = jnp.repeat(blk_exp, MOE_BLOCK)
    local = rows - lookup(pstarts, row_exp)
    valid = local < lookup(counts, row_exp)
    spos = jnp.clip(lookup(starts, row_exp) + local, 0, n_assign - 1)
    n_used = (pends[-1] // MOE_BLOCK).astype(jnp.int32)
    assign = jnp.where(valid, order[spos], -1)
    real_dst = (assign % TOP_K) * n_tok + assign // TOP_K
    in_blk = rows % MOE_BLOCK
    blk = rows // MOE_BLOCK
    trash_blk = jnp.where(blk < n_used, 1 + row_exp, 1 + N_EXPERTS + blk - n_used)
    dst_mid = jnp.where(valid, real_dst, n_assign + trash_blk * MOE_BLOCK + in_blk)
    dst_front = n_assign + jnp.arange(MOE_BLOCK, dtype=jnp.int32)
    back = jnp.zeros(((MOE_NBUF - 1) * MOE_BLOCK,), jnp.int32)
    dst_row = jnp.concatenate([dst_front, dst_mid, back]).reshape(n_blocks + MOE_NBUF, MOE_BLOCK)
    src_row = jnp.concatenate([back[:MOE_BLOCK], jnp.where(valid, assign // TOP_K, 0), back])
    src_row = src_row.reshape(n_blocks + MOE_NBUF, MOE_BLOCK)
    return gates, src_row, dst_row, blk_exp, n_used, n_blocks


def _moe_body(bexp_ref, nused_ref, src_ref, dst_ref, x_hbm, wg_ref, wu_ref, wd_ref, o_hbm,
              xf_ref, xb_ref, yb_ref, gsem, ssem, *, n_blocks, n_real):
    j = pl.program_id(0)
    slot = lax.rem(j, MOE_NBUF)
    slot_m1 = lax.rem(j + MOE_NBUF - 1, MOE_NBUF)
    slot_p1 = lax.rem(j + 1, MOE_NBUF)
    active = j < nused_ref[0]

    def gather_start(blk, n, s):
        tok = src_ref[blk + 1, n]
        pltpu.make_async_copy(x_hbm.at[pl.ds(tok, 1), :], xf_ref.at[s, pl.ds(n, 1), :], gsem.at[s]).start()

    def scatter_start(blk, n, s):
        dst = dst_ref[blk + 1, n]
        pltpu.make_async_copy(yb_ref.at[s, pl.ds(n, 1), :], o_hbm.at[pl.ds(dst, 1), :], ssem.at[s]).start()

    def gather_wait(s):
        pltpu.make_async_copy(x_hbm.at[pl.ds(0, MOE_BLOCK), :], xf_ref.at[s], gsem.at[s]).wait()

    def scatter_wait(s):
        pltpu.make_async_copy(yb_ref.at[s], o_hbm.at[pl.ds(0, MOE_BLOCK), :], ssem.at[s]).wait()

    @pl.when(j == 0)
    def _():
        yb_ref[...] = jnp.zeros_like(yb_ref)
        fills = [pltpu.make_async_copy(
            yb_ref.at[0], o_hbm.at[pl.ds(n_real + t * MOE_BLOCK, MOE_BLOCK), :], ssem.at[0])
            for t in range(MOE_TRASH_BLOCKS)]
        for fill in fills:
            fill.start()
        for fill in fills:
            fill.wait()

        def issue(n, carry):
            gather_start(0, n, 0)
            gather_start(1, n, 1)
            return carry

        lax.fori_loop(0, MOE_BLOCK, issue, 0)

    gather_wait(slot)

    @pl.when(j >= MOE_NBUF - 1)
    def _():
        scatter_wait(slot)

    @pl.when(active)
    def _():
        xb_ref[...] = xf_ref[slot].astype(BF16)

    edges = [round(MOE_BLOCK * sum(MOE_COPY_SHARE[:c])) for c in range(MOE_NCHUNK)] + [MOE_BLOCK]
    for c in range(MOE_NCHUNK):
        @pl.when(j + c < nused_ref[0] + c)
        def _(c=c):
            for n in range(edges[c], edges[c + 1]):
                gather_start(j + 2, n, slot_m1)
                scatter_start(j - 1, n, slot_m1)
            cs = slice(MOE_FF_BOUNDS[c], MOE_FF_BOUNDS[c + 1])
            xb = xb_ref[...]
            gt = jnp.dot(xb, wg_ref[:, cs], preferred_element_type=F32)
            up = jnp.dot(xb, wu_ref[:, cs], preferred_element_type=F32)
            hh = (gt * jax.nn.sigmoid(gt) * up).astype(BF16)
            part = jnp.dot(hh, wd_ref[cs, :], preferred_element_type=F32)
            if c == 0:
                yb_ref[slot] = part
            else:
                yb_ref[slot] += part

    @pl.when(jnp.logical_not(active))
    def _():
        def issue(n, carry):
            gather_start(j + 2, n, slot_m1)
            scatter_start(j - 1, n, slot_m1)
            return carry

        lax.fori_loop(0, MOE_BLOCK, issue, 0)

    @pl.when(j == n_blocks - 1)
    def _():
        def issue(n, carry):
            scatter_start(j, n, slot)
            return carry

        lax.fori_loop(0, MOE_BLOCK, issue, 0)
        scatter_wait(slot_p1)
        scatter_wait(slot_m1)
        scatter_wait(slot)
        gather_wait(slot_p1)
        gather_wait(slot_m1)


def _moe(x2d, wg, wu, wd, src_row, dst_row, blk_exp, n_used, n_blocks):
    n_tok, dim = x2d.shape
    last = jnp.maximum(n_used - 1, 0)
    jj = jnp.arange(n_blocks, dtype=jnp.int32)
    bexp_eff = blk_exp[jnp.minimum(jj, last)]
    n_real = TOP_K * n_tok
    out_rows = n_real + MOE_TRASH_BLOCKS * MOE_BLOCK

    def wmap(j, bexp, nused, src, dst):
        return (bexp[j], 0, 0)

    grid_spec = pltpu.PrefetchScalarGridSpec(
        num_scalar_prefetch=4,
        grid=(n_blocks,),
        in_specs=[pl.BlockSpec(memory_space=pl.ANY),
                  pl.BlockSpec((None, dim, D_FF), wmap),
                  pl.BlockSpec((None, dim, D_FF), wmap),
                  pl.BlockSpec((None, D_FF, dim), wmap)],
        out_specs=pl.BlockSpec(memory_space=pl.ANY),
        scratch_shapes=[pltpu.VMEM((MOE_NBUF, MOE_BLOCK, dim), F32),
                        pltpu.VMEM((MOE_BLOCK, dim), BF16),
                        pltpu.VMEM((MOE_NBUF, MOE_BLOCK, dim), F32),
                        pltpu.SemaphoreType.DMA((MOE_NBUF,)),
                        pltpu.SemaphoreType.DMA((MOE_NBUF,))],
    )
    body = functools.partial(_moe_body, n_blocks=n_blocks, n_real=n_real)
    return pl.pallas_call(
        body,
        grid_spec=grid_spec,
        out_shape=jax.ShapeDtypeStruct((out_rows, dim), F32),
        compiler_params=pltpu.CompilerParams(dimension_semantics=("arbitrary",),
                                             vmem_limit_bytes=60 * 1024 * 1024),
        name="moe_swiglu",
    )(bexp_eff, n_used.reshape(1), src_row, dst_row, x2d, wg, wu, wd)


def _combine_body(x_ref, gp_ref, y0_ref, y1_ref, g_ref, b_ref, o_ref):
    gp = gp_ref[...]
    moe = gp[:, 0:1] * y0_ref[...] + gp[:, 1:2] * y1_ref[...]
    y = ALPHA * x_ref[...] + moe
    o_ref[...] = _layernorm(y, g_ref[...], b_ref[...])


def _combine(x2d, gates_pad, yrows, g, b):
    n_tok, dim = x2d.shape
    tm = 512
    nt = n_tok // tm
    return pl.pallas_call(
        _combine_body,
        grid=(nt,),
        in_specs=[pl.BlockSpec((tm, dim), lambda i: (i, 0)),
                  pl.BlockSpec((tm, 128), lambda i: (i, 0)),
                  pl.BlockSpec((tm, dim), lambda i: (i, 0)),
                  pl.BlockSpec((tm, dim), lambda i: (nt + i, 0)),
                  _const_spec(g.shape), _const_spec(b.shape)],
        out_specs=pl.BlockSpec((tm, dim), lambda i: (i, 0)),
        out_shape=jax.ShapeDtypeStruct((n_tok, dim), F32),
        compiler_params=_cparams(("parallel",)),
        name="moe_combine_ln",
    )(x2d, gates_pad, yrows, yrows, g, b)


def _row(v):
    return v.astype(F32).reshape(1, -1)


def _even_layer(x, w_in, a_re, a_im, log_dt, b_re, b_im, c_re, c_im, d_skip, glu_w, glu_b,
                w_out, ln1_g, ln1_b, w_gate, w_up, w_down, ln2_g, ln2_b):
    bsz, seq, dim = x.shape
    h = _inproj(x, w_in.astype(BF16))
    wb, wc, lam = _s5_discretize(a_re, a_im, log_dt, b_re, b_im, c_re, c_im, bsz)
    ya = _s5(h, wb, wc, lam, _row(d_skip), glu_w.astype(BF16), _row(glu_b))
    yb = _retention(h, 1)
    w_out = w_out.astype(BF16)
    n_tok = bsz * seq
    x2 = _even_tail(x.reshape(n_tok, dim), ya.reshape(n_tok, -1), yb.reshape(n_tok, -1),
                    w_out[:S5_CHANNELS], w_out[S5_CHANNELS:], _row(ln1_g), _row(ln1_b),
                    w_gate.astype(BF16), w_up.astype(BF16), w_down.astype(BF16), _row(ln2_g), _row(ln2_b))
    return x2.reshape(bsz, seq, dim)


def _odd_layer(x, w_qkv, w_out, ln1_g, ln1_b, router_w, router_b, w_gate, w_up, w_down, ln2_g, ln2_b):
    bsz, seq, dim = x.shape
    n_tok = bsz * seq
    qkv = _qkv(x, w_qkv.astype(BF16))
    att = _dilated_attention(qkv)
    rw = jnp.zeros((dim, 128), F32).at[:, :N_EXPERTS].set(router_w.astype(F32))
    rb = jnp.zeros((1, 128), F32).at[:, :N_EXPERTS].set(router_b.astype(F32)[None, :])
    x1, logits = _outln_odd(x, att, w_out.astype(BF16), _row(ln1_g), _row(ln1_b), rw, rb)
    x1 = x1.reshape(n_tok, dim)
    gates, src_row, dst_row, blk_exp, n_used, n_blocks = _route(logits.reshape(n_tok, 128)[:, :N_EXPERTS], n_tok)
    yrows = _moe(x1, w_gate.astype(BF16), w_up.astype(BF16), w_down.astype(BF16),
                 src_row, dst_row, blk_exp, n_used, n_blocks)
    gates_pad = jnp.zeros((n_tok, 128), F32).at[:, :TOP_K].set(gates)
    out = _combine(x1, gates_pad, yrows, _row(ln2_g), _row(ln2_b))
    return out.reshape(bsz, seq, dim)


def kernel(x, ev_w_in, ev_s5_a_re, ev_s5_a_im, ev_s5_log_dt, ev_s5_b_re, ev_s5_b_im, ev_s5_c_re, ev_s5_c_im, ev_s5_d, ev_s5_glu_w, ev_s5_glu_b, ev_w_out, ev_ln1_g, ev_ln1_b, ev_ffn_w_gate, ev_ffn_w_up, ev_ffn_w_down, ev_ln2_g, ev_ln2_b, od_w_qkv, od_w_out, od_ln1_g, od_ln1_b, od_router_w, od_router_b, od_moe_w_gate, od_moe_w_up, od_moe_w_down, od_ln2_g, od_ln2_b):
    for layer in range(DEPTH):
        i = layer // 2
        if layer % 2 == 0:
            x = _even_layer(x, ev_w_in[i], ev_s5_a_re[i], ev_s5_a_im[i], ev_s5_log_dt[i], ev_s5_b_re[i],
                            ev_s5_b_im[i], ev_s5_c_re[i], ev_s5_c_im[i], ev_s5_d[i], ev_s5_glu_w[i],
                            ev_s5_glu_b[i], ev_w_out[i], ev_ln1_g[i], ev_ln1_b[i], ev_ffn_w_gate[i],
                            ev_ffn_w_up[i], ev_ffn_w_down[i], ev_ln2_g[i], ev_ln2_b[i])
        else:
            x = _odd_layer(x, od_w_qkv[i], od_w_out[i], od_ln1_g[i], od_ln1_b[i], od_router_w[i],
                           od_router_b[i], od_moe_w_gate[i], od_moe_w_up[i], od_moe_w_down[i],
                           od_ln2_g[i], od_ln2_b[i])
    return x
```

```python
import functools
import math

import numpy as np
import jax
import jax.numpy as jnp
from jax import lax
from jax.experimental import pallas as pl
from jax.experimental.pallas import tpu as pltpu

F32 = jnp.float32
BF16 = jnp.bfloat16

LN_EPS = 1e-5
DEPTH = 2
ALPHA = (2.0 * DEPTH) ** 0.25

S5_CHANNELS = 512
S5_GROUPS = 32
S5_GROUP = 16
S5_STATE = 64
S5_NSTATE = S5_GROUPS * S5_STATE
S5_TIME_CHUNK = 64

RET_HEADS = 4
RET_HEAD_DIM = 128
RET_CHUNK = 128

ATT_HEADS = 16
ATT_HEAD_DIM = 64
ATT_PAIRS = ATT_HEADS // 2
ATT_BLOCK = 128
ATT_SPAN = 128
ATT_DILATIONS = (1, 4, 16)
NEG_BIG = -1e30
LOG2E = math.log2(math.e)
ATT_Q_SCALE = ATT_HEAD_DIM ** -0.5 * LOG2E

D_FF = 2816
N_EXPERTS = 8
TOP_K = 2
MOE_BLOCK = 512

V7X_VMEM_BYTES = 64 * 1024 * 1024
VMEM_LIMIT = V7X_VMEM_BYTES - 8 * 1024 * 1024
MOE_VMEM_LIMIT = V7X_VMEM_BYTES - 4 * 1024 * 1024
ROW_TILE = 512


def _cparams(sem):
    return pltpu.CompilerParams(dimension_semantics=sem, vmem_limit_bytes=VMEM_LIMIT)


def _layernorm(y, g, b):
    mu = jnp.mean(y, -1, keepdims=True)
    yc = y - mu
    var = jnp.mean(yc * yc, -1, keepdims=True)
    return yc * lax.rsqrt(var + LN_EPS) * g + b


def _const_spec(shape):
    nd = len(shape)
    return pl.BlockSpec(shape, lambda *_: (0,) * nd)


def _inproj_body(x_ref, w_ref, h_ref):
    xb = x_ref[...].astype(BF16)
    h_ref[...] = jnp.dot(xb, w_ref[...], preferred_element_type=F32).astype(BF16)


def _inproj(x, w):
    bsz, seq, dim = x.shape
    tm = ROW_TILE
    nh = w.shape[1]
    return pl.pallas_call(
        _inproj_body,
        grid=(bsz, seq // tm),
        in_specs=[pl.BlockSpec((None, tm, dim), lambda b, s: (b, s, 0)),
                  _const_spec(w.shape)],
        out_specs=pl.BlockSpec((None, tm, nh), lambda b, s: (b, s, 0)),
        out_shape=jax.ShapeDtypeStruct((bsz, seq, nh), BF16),
        compiler_params=_cparams(("parallel", "parallel")),
        name="inproj",
    )(x, w)


def _s5_disc_body(ar_ref, ai_ref, ldt_ref, br_ref, bi_ref, lr_ref, li_ref, bbr_ref, bbi_ref):
    ar = ar_ref[...]
    ai = ai_ref[...]
    dt = jnp.exp(ldt_ref[...])
    mag = jnp.exp(ar * dt)
    lr = mag * jnp.cos(ai * dt)
    li = mag * jnp.sin(ai * dt)
    den = ar * ar + ai * ai
    zr = ((lr - 1.0) * ar + li * ai) / den
    zi = (li * ar - (lr - 1.0) * ai) / den
    lr_ref[...] = lr
    li_ref[...] = li
    br = br_ref[...]
    bi = bi_ref[...]
    bbr_ref[...] = zr[:, None, :] * br - zi[:, None, :] * bi
    bbi_ref[...] = zr[:, None, :] * bi + zi[:, None, :] * br


def _s5_discretize(a_re, a_im, log_dt, b_re, b_im, c_re, c_im, bsz):
    n_g, n_p = a_re.shape
    n_c = b_re.shape[-1]
    ldt = jnp.broadcast_to(log_dt.astype(F32)[:, None], (n_g, n_p))
    brt = jnp.swapaxes(b_re.astype(F32), 1, 2)
    bit = jnp.swapaxes(b_im.astype(F32), 1, 2)
    gp = jax.ShapeDtypeStruct((n_g, n_p), F32)
    gcp = jax.ShapeDtypeStruct((n_g, n_c, n_p), F32)
    lr, li, bbr, bbi = pl.pallas_call(
        _s5_disc_body, out_shape=[gp, gp, gcp, gcp], name="s5_discretize",
    )(a_re.astype(F32), a_im.astype(F32), ldt, brt, bit)
    eye = jnp.eye(S5_GROUPS, dtype=F32)
    nch = S5_CHANNELS
    wbr = jnp.einsum('gcp,gh->gchp', bbr, eye).reshape(nch, S5_NSTATE)
    wbi = jnp.einsum('gcp,gh->gchp', bbi, eye).reshape(nch, S5_NSTATE)
    wb = jnp.concatenate([wbr, wbi], axis=1).astype(BF16)
    wcr = jnp.einsum('gcp,gh->hpgc', c_re.astype(F32), eye).reshape(S5_NSTATE, nch)
    wci = jnp.einsum('gcp,gh->hpgc', c_im.astype(F32), eye).reshape(S5_NSTATE, nch)
    wc = jnp.concatenate([wcr, wci], axis=0).astype(BF16)
    lam = jnp.stack([jnp.broadcast_to(lr.reshape(1, S5_NSTATE), (bsz, S5_NSTATE)),
                     jnp.broadcast_to(li.reshape(1, S5_NSTATE), (bsz, S5_NSTATE))])
    return wb, wc, lam


def _gelu_tanh(x):
    return 0.5 * x * (1.0 + jnp.tanh(math.sqrt(2.0 / math.pi) * (x + 0.044715 * (x * x * x))))


def _s5_body(u_ref, wb_ref, wc_ref, lam_ref, d_ref, gw_ref, gb_ref, o_ref, st_ref, utb_ref, ys_ref, *xs_refs,
             lc, bsz):
    ncb = len(xs_refs)

    @pl.when(pl.program_id(0) == 0)
    def _():
        st_ref[...] = jnp.zeros_like(st_ref)

    nlb = S5_CHANNELS // 128
    for b in range(bsz):
        ub = u_ref[b].astype(F32)
        for c in range(nlb):
            utb_ref[c, pl.ds(b, lc, stride=bsz), :] = ub[:, c * 128:(c + 1) * 128]
    uf = jnp.concatenate([utb_ref[c] for c in range(nlb)], axis=-1)
    ub = uf.astype(BF16)
    half_c = S5_CHANNELS // 2
    half_s = S5_NSTATE // 2
    width = S5_NSTATE // ncb
    ys = [None, None]
    for cb in range(ncb):
        xs_ref = xs_refs[cb]
        re = slice(cb * width, (cb + 1) * width)
        im = slice(S5_NSTATE + cb * width, S5_NSTATE + (cb + 1) * width)
        hq = (cb * width) // half_s
        ch = slice(hq * half_c, (hq + 1) * half_c)
        xs_ref[:, :width] = jnp.dot(ub[:, ch], wb_ref[ch, re], preferred_element_type=F32)
        xs_ref[:, width:] = jnp.dot(ub[:, ch], wb_ref[ch, im], preferred_element_type=F32)
        lr = lam_ref[0, :, re]
        li = lam_ref[1, :, re]
        xr = st_ref[:, re]
        xi = st_ref[:, im]
        for t in range(lc):
            rows = slice(t * bsz, (t + 1) * bsz)
            nxr = lr * xr - li * xi + xs_ref[rows, :width]
            nxi = lr * xi + li * xr + xs_ref[rows, width:]
            xs_ref[rows, :width] = nxr
            xs_ref[rows, width:] = nxi
            xr, xi = nxr, nxi
        st_ref[:, re] = xr
        st_ref[:, im] = xi
        part = jnp.dot(xs_ref[:, :width].astype(BF16), wc_ref[re, ch], preferred_element_type=F32)
        part = part - jnp.dot(xs_ref[:, width:].astype(BF16), wc_ref[im, ch], preferred_element_type=F32)
        ys[hq] = part if ys[hq] is None else ys[hq] + part
    y = jnp.concatenate(ys, axis=-1)
    y = y + d_ref[...] * uf
    y = _gelu_tanh(y)
    z = jnp.dot(y.astype(BF16), gw_ref[...], preferred_element_type=F32) + gb_ref[...]
    yo = y * jax.nn.sigmoid(z)
    for c in range(nlb):
        ys_ref[c] = yo[:, c * 128:(c + 1) * 128]
    for b in range(bsz):
        o_ref[b] = jnp.concatenate([ys_ref[c, pl.ds(b, lc, stride=bsz), :] for c in range(nlb)],
                                   axis=-1).astype(BF16)


def _s5(h, wb, wc, lam, d_skip, glu_w, glu_b):
    bsz, seq, _ = h.shape
    lc = S5_TIME_CHUNK
    blk = lc * bsz
    ncb = 4
    body = functools.partial(_s5_body, lc=lc, bsz=bsz)
    return pl.pallas_call(
        body,
        grid=(seq // lc,),
        in_specs=[pl.BlockSpec((bsz, lc, S5_CHANNELS), lambda s: (0, s, 0)),
                  _const_spec(wb.shape), _const_spec(wc.shape), _const_spec(lam.shape),
                  _const_spec(d_skip.shape), _const_spec(glu_w.shape), _const_spec(glu_b.shape)],
        out_specs=pl.BlockSpec((bsz, lc, S5_CHANNELS), lambda s: (0, s, 0)),
        out_shape=jax.ShapeDtypeStruct((bsz, seq, S5_CHANNELS), BF16),
        scratch_shapes=[pltpu.VMEM((bsz, 2 * S5_NSTATE), F32),
                        pltpu.VMEM((S5_CHANNELS // 128, blk, 128), F32),
                        pltpu.VMEM((S5_CHANNELS // 128, blk, 128), F32)]
        + [pltpu.VMEM((blk, 2 * S5_NSTATE // ncb), F32) for _ in range(ncb)],
        compiler_params=_cparams(("arbitrary",)),
        name="s5_mixer",
    )(h, wb, wc, lam, d_skip, glu_w, glu_b)


def _ret_constants():
    hh = np.arange(RET_HEADS, dtype=np.float64)
    log_gamma = np.log1p(-np.exp2(-5.0 - hh))
    pos = np.arange(RET_CHUNK, dtype=np.float64)
    diff = pos[:, None] - pos[None, :]
    scale = RET_HEAD_DIM ** -0.5
    decay = np.where(diff >= 0, np.exp(log_gamma[:, None, None] * np.maximum(diff, 0.0)), 0.0) * scale
    zeta = np.exp(log_gamma[:, None] * (RET_CHUNK - 1.0 - pos)) * scale
    xi = np.exp(log_gamma[:, None] * (pos + 1.0))
    chunk_decay = np.exp(log_gamma * RET_CHUNK)
    zeta_b = np.broadcast_to(zeta[:, :, None], (RET_HEADS, RET_CHUNK, RET_HEAD_DIM))
    xi_b = np.broadcast_to(xi[:, :, None], (RET_HEADS, RET_CHUNK, RET_HEAD_DIM))
    return (decay.astype(np.float32), np.ascontiguousarray(zeta_b).astype(np.float32),
            np.ascontiguousarray(xi_b).astype(np.float32), [float(c) for c in chunk_decay])


def _ret_body(q_ref, k_ref, v_ref, g_ref, dec_ref, zeta_ref, xi_ref, o_ref, st_ref, *, ts, chunk_decay):
    @pl.when(pl.program_id(1) == 0)
    def _():
        st_ref[...] = jnp.zeros_like(st_ref)

    n_chunks = ts // RET_CHUNK
    for h in range(RET_HEADS):
        cols = slice(h * RET_HEAD_DIM, (h + 1) * RET_HEAD_DIM)
        states = [st_ref[h]]
        for c in range(n_chunks):
            rows = slice(c * RET_CHUNK, (c + 1) * RET_CHUNK)
            kz = (k_ref[rows, cols].astype(F32) * zeta_ref[h]).astype(BF16)
            kv = lax.dot_general(kz, v_ref[rows, cols], (((0,), (0,)), ((), ())), preferred_element_type=F32)
            states.append(states[-1] * chunk_decay[h] + kv)
        st_ref[h] = states[n_chunks]
        for c in range(n_chunks):
            rows = slice(c * RET_CHUNK, (c + 1) * RET_CHUNK)
            q = q_ref[rows, cols]
            s = lax.dot_general(q, k_ref[rows, cols], (((1,), (1,)), ((), ())),
                                preferred_element_type=F32) * dec_ref[h]
            inner = jnp.dot(s.astype(BF16), v_ref[rows, cols], preferred_element_type=F32)
            cross = jnp.dot(q, states[c].astype(BF16), preferred_element_type=F32) * xi_ref[h]
            ret = inner + cross
            mu = jnp.mean(ret, -1, keepdims=True)
            rc = ret - mu
            var = jnp.mean(rc * rc, -1, keepdims=True)
            rn = rc * lax.rsqrt(var + LN_EPS)
            gg = g_ref[rows, cols].astype(F32)
            o_ref[rows, cols] = (gg * jax.nn.sigmoid(gg) * rn).astype(BF16)


def _retention(h, first_block):
    bsz, seq, _ = h.shape
    ts = ROW_TILE
    width = RET_HEADS * RET_HEAD_DIM
    decay, zeta_b, xi_b, chunk_decay = _ret_constants()
    body = functools.partial(_ret_body, ts=ts, chunk_decay=chunk_decay)

    def col(j):
        return pl.BlockSpec((None, ts, width), lambda b, s, j=j: (b, s, j))

    return pl.pallas_call(
        body,
        grid=(bsz, seq // ts),
        in_specs=[col(first_block), col(first_block + 1), col(first_block + 2), col(first_block + 3),
                  _const_spec(decay.shape), _const_spec(zeta_b.shape), _const_spec(xi_b.shape)],
        out_specs=pl.BlockSpec((None, ts, width), lambda b, s: (b, s, 0)),
        out_shape=jax.ShapeDtypeStruct((bsz, seq, width), BF16),
        scratch_shapes=[pltpu.VMEM((RET_HEADS, RET_HEAD_DIM, RET_HEAD_DIM), F32)],
        compiler_params=_cparams(("parallel", "arbitrary")),
        name="retention",
    )(h, h, h, h, jnp.asarray(decay), jnp.asarray(zeta_b), jnp.asarray(xi_b))


FFN_BOUNDS = (0, 1536, 2816)


def _even_tail_body(x_ref, ya_ref, yb_ref, w1_ref, w2_ref, g1_ref, b1_ref,
                    wg_ref, wu_ref, wd_ref, g2_ref, b2_ref, *rest):
    n_cast = (len(rest) - 1) // 2
    o_ref = rest[n_cast]
    for src, dst in zip(rest[:n_cast], rest[n_cast + 1:]):
        dst[...] = src[...].astype(BF16)
    mix = jnp.dot(ya_ref[...], w1_ref[...], preferred_element_type=F32)
    mix = mix + jnp.dot(yb_ref[...], w2_ref[...], preferred_element_type=F32)
    x = _layernorm(ALPHA * x_ref[...] + mix, g1_ref[...], b1_ref[...])
    xb = x.astype(BF16)
    acc = None
    for c in range(len(FFN_BOUNDS) - 1):
        cs = slice(FFN_BOUNDS[c], FFN_BOUNDS[c + 1])
        gt = jnp.dot(xb, wg_ref[:, cs], preferred_element_type=F32)
        up = jnp.dot(xb, wu_ref[:, cs], preferred_element_type=F32)
        hh = (gt * jax.nn.sigmoid(gt) * up).astype(BF16)
        part = jnp.dot(hh, wd_ref[cs, :], preferred_element_type=F32)
        acc = part if acc is None else acc + part
    y = ALPHA * x + acc
    o_ref[...] = _layernorm(y, g2_ref[...], b2_ref[...])


def _even_tail(x2d, ya2d, yb2d, w1, w2, g1, b1, wg, wu, wd, g2, b2, cast=()):
    n_tok, dim = x2d.shape
    tm = ROW_TILE
    steps = n_tok // tm
    single = pl.Buffered(1)

    def wspec(shape):
        return pl.BlockSpec(shape, lambda i: (0, 0), pipeline_mode=single)

    def rows(width):
        return pl.BlockSpec((tm, width), lambda i: (i, 0))

    cast_specs = []
    for a in cast:
        assert a.shape[0] % (16 * steps) == 0, a.shape
        cast_specs.append(pl.BlockSpec((a.shape[0] // steps, a.shape[1]), lambda i: (i, 0)))
    outs = pl.pallas_call(
        _even_tail_body,
        grid=(steps,),
        in_specs=[rows(dim), rows(ya2d.shape[1]), rows(yb2d.shape[1]),
                  wspec(w1.shape), wspec(w2.shape), _const_spec(g1.shape), _const_spec(b1.shape),
                  wspec(wg.shape), wspec(wu.shape), wspec(wd.shape),
                  _const_spec(g2.shape), _const_spec(b2.shape)] + cast_specs,
        out_specs=[rows(dim)] + cast_specs,
        out_shape=[jax.ShapeDtypeStruct((n_tok, dim), F32)]
        + [jax.ShapeDtypeStruct(a.shape, BF16) for a in cast],
        compiler_params=_cparams(("parallel",)),
        name="outproj_ffn_ln",
    )(x2d, ya2d, yb2d, w1, w2, g1, b1, wg, wu, wd, g2, b2, *cast)
    return outs[0], tuple(outs[1:])


def _qkv_body(x_ref, w_ref, o_ref):
    xb = x_ref[...].astype(BF16)
    r = jnp.dot(xb, w_ref[...], preferred_element_type=F32)
    nblk = o_ref.shape[0]
    for j in range(nblk):
        blk = r[:, j * 128:(j + 1) * 128]
        if j < ATT_PAIRS:
            blk = blk * ATT_Q_SCALE
        o_ref[j] = blk.astype(BF16)


def _qkv(x, w):
    bsz, seq, dim = x.shape
    tm = ROW_TILE
    nblk = w.shape[1] // 128
    return pl.pallas_call(
        _qkv_body,
        grid=(bsz, seq // tm),
        in_specs=[pl.BlockSpec((None, tm, dim), lambda b, s: (b, s, 0)),
                  _const_spec(w.shape)],
        out_specs=pl.BlockSpec((None, nblk, tm, 128), lambda b, s: (b, 0, s, 0)),
        out_shape=jax.ShapeDtypeStruct((bsz, nblk, seq, 128), BF16),
        compiler_params=_cparams(("parallel", "parallel")),
        name="qkv_proj",
    )(x, w)


ATT_GROUP = 16


def _att_bias_tiles():
    slopes = jnp.exp2(-8.0 * jnp.arange(1, ATT_HEADS + 1, dtype=F32) / ATT_HEADS).reshape(ATT_PAIRS, 2)
    slope_rows = jnp.repeat(slopes, ATT_BLOCK, axis=1)[:, None, :, None]
    qi = jnp.arange(2 * ATT_BLOCK, dtype=jnp.int32)[:, None] % ATT_BLOCK
    ki = jnp.arange(2 * ATT_BLOCK, dtype=jnp.int32)[None, :]
    tiles = []
    for d in ATT_DILATIONS:
        for oi in range(2):
            steps = (qi - ki + oi * ATT_BLOCK).astype(F32)[None, None]
            allowed = (steps >= 0.0) & (steps <= float(ATT_SPAN))
            tiles.append(jnp.where(allowed, -(slope_rows * (float(d) * LOG2E)) * steps, NEG_BIG))
    return jnp.concatenate(tiles, axis=1)


def _stack_heads(q, low):
    zero = jnp.zeros_like(q)
    return jnp.concatenate([jnp.where(low, q, zero), jnp.where(low, zero, q)], axis=0)


def _att_unit(qs, kw, vw, bias, old, low, ones, last):
    s = lax.dot_general(qs, kw, (((1,), (1,)), ((), ())), preferred_element_type=F32) + bias
    sa = s[:, :128]
    sb = s[:, 128:]
    mrow = jnp.max(jnp.maximum(sa, sb), axis=-1, keepdims=True)
    m_new = jnp.broadcast_to(mrow, (2 * ATT_BLOCK, 128))
    if old is not None:
        m_old, acc_old, den_old = old
        m_new = jnp.maximum(m_old, m_new)
    p = jnp.concatenate([jnp.exp2(sa - m_new), jnp.exp2(sb - m_new)], axis=1).astype(BF16)
    vaug = jnp.concatenate([vw, ones], axis=1)
    res = jnp.dot(p, vaug, preferred_element_type=F32)
    acc = jnp.where(low, res[:ATT_BLOCK, :128], res[ATT_BLOCK:, :128])
    den = jnp.where(low, res[:ATT_BLOCK, 128:], res[ATT_BLOCK:, 128:])
    if old is not None:
        a = jnp.exp2(m_old - m_new)
        a_pair = jnp.where(low, a[:ATT_BLOCK], a[ATT_BLOCK:])
        acc = a_pair * acc_old + acc
        den = a_pair * den_old + den
    if last:
        acc = acc / den
    return m_new, acc, den


def _att_body(q_ref, k_ref, v_ref, bias_ref, o_ref,
              tmp_ref, q4_ref, k4_ref, v4_ref, qs4_ref, k4b_ref, v4b_ref, nat_ref, de4_ref, *, seq):
    quarter = seq // 4
    for src, dst in ((q_ref, q4_ref), (k_ref, k4_ref), (v_ref, v4_ref)):
        tmp_ref[...] = src[...].astype(F32)
        for r in range(4):
            dst[r] = tmp_ref[pl.ds(r, quarter, stride=4), :]
    lane = lax.broadcasted_iota(jnp.int32, (ATT_BLOCK, 128), 1)
    low = lane < ATT_HEAD_DIM
    ones = jnp.ones((2 * ATT_BLOCK, 128), BF16)
    low_all = lax.broadcasted_iota(jnp.int32, (quarter, 128), 1) < ATT_HEAD_DIM
    for r in range(4):
        q4 = q4_ref[r].astype(BF16)
        qs4_ref[0, r] = jnp.where(low_all, q4, jnp.zeros_like(q4))
        qs4_ref[1, r] = jnp.where(low_all, jnp.zeros_like(q4), q4)
        k4b_ref[r] = k4_ref[r].astype(BF16)
        v4b_ref[r] = v4_ref[r].astype(BF16)

    n_units = seq // ATT_BLOCK
    for bi, d in enumerate(ATT_DILATIONS):
        first = bi == 0
        last = bi == len(ATT_DILATIONS) - 1
        shift = d.bit_length() - 1

        def group(it, carry, bi=bi, d=d, shift=shift, first=first, last=last):
            loaded = []
            for g in range(ATT_GROUP):
                n = it * ATT_GROUP + g
                i = n >> shift
                r = n & (d - 1)
                iw = jnp.maximum(i - 1, 0)
                if d == 1:
                    qsl = pl.ds(pl.multiple_of(i * ATT_BLOCK, ATT_BLOCK), ATT_BLOCK)
                    wsl = pl.ds(pl.multiple_of(iw * ATT_BLOCK, ATT_BLOCK), 2 * ATT_BLOCK)
                    data = (_stack_heads(q_ref[qsl, :], low), k_ref[wsl, :], v_ref[wsl, :])
                    old_at = None
                    new_at = lambda s, qsl=qsl: nat_ref.at[s, qsl, :]
                elif d == 4:
                    qsl = pl.ds(pl.multiple_of(i * ATT_BLOCK, ATT_BLOCK), ATT_BLOCK)
                    wsl = pl.ds(pl.multiple_of(iw * ATT_BLOCK, ATT_BLOCK), 2 * ATT_BLOCK)
                    osl = pl.ds(r + (4 * ATT_BLOCK) * i, ATT_BLOCK, stride=4)
                    data = (jnp.concatenate([qs4_ref[0, r, qsl, :], qs4_ref[1, r, qsl, :]], axis=0),
                            k4b_ref[r, wsl, :], v4b_ref[r, wsl, :])
                    old_at = lambda s, osl=osl: nat_ref.at[s, osl, :]
                    new_at = lambda s, r=r, qsl=qsl: de4_ref.at[s, r, qsl, :]
                else:
                    r4 = r & 3
                    qsl = pl.ds((r >> 2) + (4 * ATT_BLOCK) * i, ATT_BLOCK, stride=4)
                    wsl = pl.ds((r >> 2) + (4 * ATT_BLOCK) * iw, 2 * ATT_BLOCK, stride=4)
                    data = (_stack_heads(q4_ref[r4, qsl, :].astype(BF16), low),
                            k4_ref[r4, wsl, :].astype(BF16), v4_ref[r4, wsl, :].astype(BF16))
                    old_at = lambda s, r4=r4, qsl=qsl: de4_ref.at[s, r4, qsl, :]
                    new_at = old_at
                old = None
                if old_at is not None:
                    old = (jnp.concatenate([old_at(1)[...], old_at(2)[...]], axis=0),
                           old_at(0)[...], old_at(3)[...])
                loaded.append((new_at, data, bias_ref[2 * bi + jnp.minimum(i, 1)], old))
            results = [_att_unit(qs, kw, vw, bias, old, low, ones, last)
                       for (_, (qs, kw, vw), bias, old) in loaded]
            for (new_at, *_), (m_new, acc, den) in zip(loaded, results):
                new_at(0)[...] = acc
                if not last:
                    new_at(1)[...] = m_new[:ATT_BLOCK]
                    new_at(2)[...] = m_new[ATT_BLOCK:]
                    new_at(3)[...] = den
            return carry

        lax.fori_loop(0, n_units // ATT_GROUP, group, 0)

    for r in range(4):
        nat_ref[0, pl.ds(r, quarter, stride=4), :] = de4_ref[0, r]
    o_ref[...] = nat_ref[0].astype(BF16)


def _dilated_attention(qkv):
    bsz, nblk, seq, _ = qkv.shape
    bias = _att_bias_tiles()
    specs = [pl.BlockSpec((None, None, seq, 128), lambda hp, b, j=j: (b, j * ATT_PAIRS + hp, 0, 0))
             for j in range(3)]
    specs.append(pl.BlockSpec((None,) + bias.shape[1:], lambda hp, b: (hp, 0, 0, 0)))
    body = functools.partial(_att_body, seq=seq)
    return pl.pallas_call(
        body,
        grid=(ATT_PAIRS, bsz),
        in_specs=specs,
        out_specs=pl.BlockSpec((None, None, seq, 128), lambda hp, b: (b, hp, 0, 0)),
        out_shape=jax.ShapeDtypeStruct((bsz, ATT_PAIRS, seq, 128), BF16),
        scratch_shapes=[pltpu.VMEM((seq, 128), F32)]
        + [pltpu.VMEM((4, seq // 4, 128), F32) for _ in range(3)]
        + [pltpu.VMEM((2, 4, seq // 4, 128), BF16),
           pltpu.VMEM((4, seq // 4, 128), BF16), pltpu.VMEM((4, seq // 4, 128), BF16),
           pltpu.VMEM((4, seq, 128), F32), pltpu.VMEM((4, 4, seq // 4, 128), F32)],
        compiler_params=_cparams(("parallel", "parallel")),
        name="dilated_attention",
    )(qkv, qkv, qkv, bias)


def _outln_odd_body(x_ref, a_ref, w_ref, g_ref, b_ref, rw_ref, rb_ref, o_ref, lg_ref):
    a = jnp.concatenate([a_ref[j] for j in range(a_ref.shape[0])], axis=-1)
    acc = jnp.dot(a, w_ref[...], preferred_element_type=F32)
    y = ALPHA * x_ref[...] + acc
    o = _layernorm(y, g_ref[...], b_ref[...])
    o_ref[...] = o
    nl = lg_ref.shape[-1]
    oh = o.astype(BF16)
    ol = (o - oh.astype(F32)).astype(BF16)
    both = jnp.dot(oh, rw_ref[...], preferred_element_type=F32)
    lg = both[:, :nl] + both[:, nl:] + jnp.dot(ol, rw_ref[:, :nl], preferred_element_type=F32)
    lg_ref[...] = lg + rb_ref[...]


def _outln_odd(x, att, w, g, b, rw, rb):
    bsz, seq, dim = x.shape
    tm = ROW_TILE
    npair = att.shape[1]
    rwh = rw.astype(BF16)
    rwl = (rw - rwh.astype(F32)).astype(BF16)
    rw2 = jnp.concatenate([rwh, rwl], axis=1)
    return pl.pallas_call(
        _outln_odd_body,
        grid=(bsz, seq // tm),
        in_specs=[pl.BlockSpec((None, tm, dim), lambda b_, s: (b_, s, 0)),
                  pl.BlockSpec((None, npair, tm, 128), lambda b_, s: (b_, 0, s, 0)),
                  _const_spec(w.shape), _const_spec(g.shape), _const_spec(b.shape),
                  _const_spec(rw2.shape), _const_spec(rb.shape)],
        out_specs=[pl.BlockSpec((None, tm, dim), lambda b_, s: (b_, s, 0)),
                   pl.BlockSpec((None, tm, 128), lambda b_, s: (b_, s, 0))],
        out_shape=[jax.ShapeDtypeStruct((bsz, seq, dim), F32),
                   jax.ShapeDtypeStruct((bsz, seq, 128), F32)],
        compiler_params=_cparams(("parallel", "parallel")),
        name="outproj_ln_odd",
    )(x, att, w, g, b, rw2, rb)


MOE_FF_BOUNDS = (0, 768, 1536, 2304, 2816)
MOE_NCHUNK = len(MOE_FF_BOUNDS) - 1
MOE_COPY_SHARE = (0.25, 0.25, 0.25, 0.25)
MOE_TRASH_BLOCKS = 2 + 2 * N_EXPERTS
MOE_NBUF = 3


def _route(logits, n_tok):
    top_val, top_idx = lax.top_k(logits, TOP_K)
    gates = jax.nn.softmax(top_val, axis=-1)
    n_assign = n_tok * TOP_K
    exp_flat = top_idx.reshape(n_assign).astype(jnp.int32)
    experts = jnp.arange(N_EXPERTS, dtype=jnp.int32)

    def lookup(table, idx):
        return jnp.sum(jnp.where(idx[:, None] == experts[None, :], table[None, :], 0), axis=1)

    counts = jnp.sum((exp_flat[:, None] == experts[None, :]).astype(jnp.int32), axis=0)
    starts = jnp.cumsum(counts) - counts
    padded_counts = (counts + MOE_BLOCK - 1) // MOE_BLOCK * MOE_BLOCK
    pends = jnp.cumsum(padded_counts)
    pstarts = pends - padded_counts
    order = jnp.argsort(exp_flat).astype(jnp.int32)
    n_blocks = -(-n_assign // MOE_BLOCK) + N_EXPERTS
    n_rows = n_blocks * MOE_BLOCK
    blk_exp = jnp.minimum(jnp.sum((pends[None, :] <= (jnp.arange(n_blocks, dtype=jnp.int32) * MOE_BLOCK)[:, None])
                                  .astype(jnp.int32), axis=1), N_EXPERTS - 1)
    rows = jnp.arange(n_rows, dtype=jnp.int32)
    row_exp = jnp.repeat(blk_exp, MOE_BLOCK)
    local = rows - lookup(pstarts, row_exp)
    valid = local < lookup(counts, row_exp)
    spos = jnp.clip(lookup(starts, row_exp) + local, 0, n_assign - 1)
    n_used = (pends[-1] // MOE_BLOCK).astype(jnp.int32)
    assign = jnp.where(valid, order[spos], -1)
    real_dst = (assign % TOP_K) * n_tok + assign // TOP_K
    in_blk = rows % MOE_BLOCK
    blk = rows // MOE_BLOCK
    trash_blk = jnp.where(blk < n_used, 1 + row_exp, 1 + N_EXPERTS + blk - n_used)
    dst_mid = jnp.where(valid, real_dst, n_assign + trash_blk * MOE_BLOCK + in_blk)
    dst_front = n_assign + jnp.arange(MOE_BLOCK, dtype=jnp.int32)
    back = jnp.zeros(((MOE_NBUF - 1) * MOE_BLOCK,), jnp.int32)
    dst_row = jnp.concatenate([dst_front, dst_mid, back]).reshape(n_blocks + MOE_NBUF, MOE_BLOCK)
    src_row = jnp.concatenate([back[:MOE_BLOCK], jnp.where(valid, assign // TOP_K, 0), back])
    src_row = src_row.reshape(n_blocks + MOE_NBUF, MOE_BLOCK)
    return gates, src_row, dst_row, blk_exp, n_used, n_blocks


def _moe_body(bexp_ref, nused_ref, src_ref, dst_ref, x_hbm, wg_ref, wu_ref, wd_ref, o_hbm,
              xf_ref, xb_ref, yb_ref, gsem, ssem, *, n_blocks, n_real):
    j = pl.program_id(0)
    slot = lax.rem(j, MOE_NBUF)
    slot_m1 = lax.rem(j + MOE_NBUF - 1, MOE_NBUF)
    slot_p1 = lax.rem(j + 1, MOE_NBUF)
    active = j < nused_ref[0]

    def gather_start(blk, n, s):
        tok = src_ref[blk + 1, n]
        pltpu.make_async_copy(x_hbm.at[pl.ds(tok, 1), :], xf_ref.at[s, pl.ds(n, 1), :], gsem.at[s]).start()

    def scatter_start(blk, n, s):
        dst = dst_ref[blk + 1, n]
        pltpu.make_async_copy(yb_ref.at[s, pl.ds(n, 1), :], o_hbm.at[pl.ds(dst, 1), :], ssem.at[s]).start()

    def gather_wait(s):
        pltpu.make_async_copy(x_hbm.at[pl.ds(0, MOE_BLOCK), :], xf_ref.at[s], gsem.at[s]).wait()

    def scatter_wait(s):
        pltpu.make_async_copy(yb_ref.at[s], o_hbm.at[pl.ds(0, MOE_BLOCK), :], ssem.at[s]).wait()

    @pl.when(j == 0)
    def _():
        yb_ref[...] = jnp.zeros_like(yb_ref)
        fills = [pltpu.make_async_copy(
            yb_ref.at[0], o_hbm.at[pl.ds(n_real + t * MOE_BLOCK, MOE_BLOCK), :], ssem.at[0])
            for t in range(MOE_TRASH_BLOCKS)]
        for fill in fills:
            fill.start()
        for fill in fills:
            fill.wait()

        def issue(n, carry):
            gather_start(0, n, 0)
            gather_start(1, n, 1)
            return carry

        lax.fori_loop(0, MOE_BLOCK, issue, 0)

    gather_wait(slot)

    @pl.when(j >= MOE_NBUF - 1)
    def _():
        scatter_wait(slot)

    @pl.when(active)
    def _():
        xb_ref[...] = xf_ref[slot].astype(BF16)

    edges = [round(MOE_BLOCK * sum(MOE_COPY_SHARE[:c])) for c in range(MOE_NCHUNK)] + [MOE_BLOCK]
    for c in range(MOE_NCHUNK):
        @pl.when(j + c < nused_ref[0] + c)
        def _(c=c):
            for n in range(edges[c], edges[c + 1]):
                gather_start(j + 2, n, slot_m1)
                scatter_start(j - 1, n, slot_m1)
            cs = slice(MOE_FF_BOUNDS[c], MOE_FF_BOUNDS[c + 1])
            xb = xb_ref[...]
            gt = jnp.dot(xb, wg_ref[:, cs], preferred_element_type=F32)
            up = jnp.dot(xb, wu_ref[:, cs], preferred_element_type=F32)
            hh = (gt * jax.nn.sigmoid(gt) * up).astype(BF16)
            part = jnp.dot(hh, wd_ref[cs, :], preferred_element_type=F32)
            if c == 0:
                yb_ref[slot] = part
            else:
                yb_ref[slot] += part

    @pl.when(jnp.logical_not(active))
    def _():
        def issue(n, carry):
            gather_start(j + 2, n, slot_m1)
            scatter_start(j - 1, n, slot_m1)
            return carry

        lax.fori_loop(0, MOE_BLOCK, issue, 0)

    @pl.when(j == n_blocks - 1)
    def _():
        def issue(n, carry):
            scatter_start(j, n, slot)
            return carry

        lax.fori_loop(0, MOE_BLOCK, issue, 0)
        scatter_wait(slot_p1)
        scatter_wait(slot_m1)
        scatter_wait(slot)
        gather_wait(slot_p1)
        gather_wait(slot_m1)


def _moe(x2d, wg, wu, wd, src_row, dst_row, blk_exp, n_used, n_blocks):
    n_tok, dim = x2d.shape
    last = jnp.maximum(n_used - 1, 0)
    jj = jnp.arange(n_blocks, dtype=jnp.int32)
    bexp_eff = blk_exp[jnp.minimum(jj, last)]
    n_real = TOP_K * n_tok
    out_rows = n_real + MOE_TRASH_BLOCKS * MOE_BLOCK

    def wmap(j, bexp, nused, src, dst):
        return (bexp[j], 0, 0)

    grid_spec = pltpu.PrefetchScalarGridSpec(
        num_scalar_prefetch=4,
        grid=(n_blocks,),
        in_specs=[pl.BlockSpec(memory_space=pl.ANY),
                  pl.BlockSpec((None, dim, D_FF), wmap),
                  pl.BlockSpec((None, dim, D_FF), wmap),
                  pl.BlockSpec((None, D_FF, dim), wmap)],
        out_specs=pl.BlockSpec(memory_space=pl.ANY),
        scratch_shapes=[pltpu.VMEM((MOE_NBUF, MOE_BLOCK, dim), F32),
                        pltpu.VMEM((MOE_BLOCK, dim), BF16),
                        pltpu.VMEM((MOE_NBUF, MOE_BLOCK, dim), F32),
                        pltpu.SemaphoreType.DMA((MOE_NBUF,)),
                        pltpu.SemaphoreType.DMA((MOE_NBUF,))],
    )
    body = functools.partial(_moe_body, n_blocks=n_blocks, n_real=n_real)
    return pl.pallas_call(
        body,
        grid_spec=grid_spec,
        out_shape=jax.ShapeDtypeStruct((out_rows, dim), F32),
        compiler_params=pltpu.CompilerParams(dimension_semantics=("arbitrary",),
                                             vmem_limit_bytes=MOE_VMEM_LIMIT),
        name="moe_swiglu",
    )(bexp_eff, n_used.reshape(1), src_row, dst_row, x2d, wg, wu, wd)


def _combine_body(x_ref, gp_ref, y0_ref, y1_ref, g_ref, b_ref, o_ref):
    gp = gp_ref[...]
    moe = gp[:, 0:1] * y0_ref[...] + gp[:, 1:2] * y1_ref[...]
    y = ALPHA * x_ref[...] + moe
    o_ref[...] = _layernorm(y, g_ref[...], b_ref[...])


def _combine(x2d, gates_pad, yrows, g, b):
    n_tok, dim = x2d.shape
    tm = ROW_TILE
    nt = n_tok // tm
    return pl.pallas_call(
        _combine_body,
        grid=(nt,),
        in_specs=[pl.BlockSpec((tm, dim), lambda i: (i, 0)),
                  pl.BlockSpec((tm, 128), lambda i: (i, 0)),
                  pl.BlockSpec((tm, dim), lambda i: (i, 0)),
                  pl.BlockSpec((tm, dim), lambda i: (nt + i, 0)),
                  _const_spec(g.shape), _const_spec(b.shape)],
        out_specs=pl.BlockSpec((tm, dim), lambda i: (i, 0)),
        out_shape=jax.ShapeDtypeStruct((n_tok, dim), F32),
        compiler_params=_cparams(("parallel",)),
        name="moe_combine_ln",
    )(x2d, gates_pad, yrows, yrows, g, b)


def _row(v):
    return v.astype(F32).reshape(1, -1)


def _even_layer(x, w_in, a_re, a_im, log_dt, b_re, b_im, c_re, c_im, d_skip, glu_w, glu_b,
                w_out, ln1_g, ln1_b, w_gate, w_up, w_down, ln2_g, ln2_b, cast_f32=()):
    bsz, seq, dim = x.shape
    h = _inproj(x, w_in.astype(BF16))
    wb, wc, lam = _s5_discretize(a_re, a_im, log_dt, b_re, b_im, c_re, c_im, bsz)
    ya = _s5(h, wb, wc, lam, _row(d_skip), glu_w.astype(BF16), _row(glu_b))
    yb = _retention(h, 1)
    w_out = w_out.astype(BF16)
    n_tok = bsz * seq
    flat = [w.reshape(-1, w.shape[-1]) for w in cast_f32]
    x2, casted = _even_tail(x.reshape(n_tok, dim), ya.reshape(n_tok, -1), yb.reshape(n_tok, -1),
                            w_out[:S5_CHANNELS], w_out[S5_CHANNELS:], _row(ln1_g), _row(ln1_b),
                            w_gate.astype(BF16), w_up.astype(BF16), w_down.astype(BF16),
                            _row(ln2_g), _row(ln2_b), cast=flat)
    casted = tuple(c.reshape(w.shape) for c, w in zip(casted, cast_f32))
    return x2.reshape(bsz, seq, dim), casted


def _odd_layer(x, w_qkv, w_out, ln1_g, ln1_b, router_w, router_b, w_gate, w_up, w_down, ln2_g, ln2_b):
    bsz, seq, dim = x.shape
    n_tok = bsz * seq
    qkv = _qkv(x, w_qkv.astype(BF16))
    att = _dilated_attention(qkv)
    rw = jnp.zeros((dim, 128), F32).at[:, :N_EXPERTS].set(router_w.astype(F32))
    rb = jnp.zeros((1, 128), F32).at[:, :N_EXPERTS].set(router_b.astype(F32)[None, :])
    x1, logits = _outln_odd(x, att, w_out.astype(BF16), _row(ln1_g), _row(ln1_b), rw, rb)
    x1 = x1.reshape(n_tok, dim)
    gates, src_row, dst_row, blk_exp, n_used, n_blocks = _route(logits.reshape(n_tok, 128)[:, :N_EXPERTS], n_tok)
    yrows = _moe(x1, w_gate.astype(BF16), w_up.astype(BF16), w_down.astype(BF16),
                 src_row, dst_row, blk_exp, n_used, n_blocks)
    gates_pad = jnp.zeros((n_tok, 128), F32).at[:, :TOP_K].set(gates)
    out = _combine(x1, gates_pad, yrows, _row(ln2_g), _row(ln2_b))
    return out.reshape(bsz, seq, dim)


def kernel(x, ev_w_in, ev_s5_a_re, ev_s5_a_im, ev_s5_log_dt, ev_s5_b_re, ev_s5_b_im, ev_s5_c_re, ev_s5_c_im, ev_s5_d, ev_s5_glu_w, ev_s5_glu_b, ev_w_out, ev_ln1_g, ev_ln1_b, ev_ffn_w_gate, ev_ffn_w_up, ev_ffn_w_down, ev_ln2_g, ev_ln2_b, od_w_qkv, od_w_out, od_ln1_g, od_ln1_b, od_router_w, od_router_b, od_moe_w_gate, od_moe_w_up, od_moe_w_down, od_ln2_g, od_ln2_b):
    moe_bf16 = None
    for layer in range(DEPTH):
        i = layer // 2
        if layer % 2 == 0:
            nxt = (od_moe_w_gate[i], od_moe_w_up[i], od_moe_w_down[i]) if layer + 1 < DEPTH else ()
            x, casted = _even_layer(x, ev_w_in[i], ev_s5_a_re[i], ev_s5_a_im[i], ev_s5_log_dt[i], ev_s5_b_re[i],
                                    ev_s5_b_im[i], ev_s5_c_re[i], ev_s5_c_im[i], ev_s5_d[i], ev_s5_glu_w[i],
                                    ev_s5_glu_b[i], ev_w_out[i], ev_ln1_g[i], ev_ln1_b[i], ev_ffn_w_gate[i],
                                    ev_ffn_w_up[i], ev_ffn_w_down[i], ev_ln2_g[i], ev_ln2_b[i], cast_f32=nxt)
            moe_bf16 = casted if nxt else None
        else:
            wg, wu, wd = moe_bf16 or (od_moe_w_gate[i], od_moe_w_up[i], od_moe_w_down[i])
            x = _odd_layer(x, od_w_qkv[i], od_w_out[i], od_ln1_g[i], od_ln1_b[i], od_router_w[i],
                           od_router_b[i], wg, wu, wd, od_ln2_g[i], od_ln2_b[i])
            moe_bf16 = None
    return x
```

```python
import functools
import math

import numpy as np
import jax
import jax.numpy as jnp
from jax import lax
from jax.experimental import pallas as pl
from jax.experimental.pallas import tpu as pltpu

F32 = jnp.float32
BF16 = jnp.bfloat16

LN_EPS = 1e-5
DEPTH = 2
ALPHA = (2.0 * DEPTH) ** 0.25

S5_CHANNELS = 512
S5_GROUPS = 32
S5_GROUP = 16
S5_STATE = 64
S5_NSTATE = S5_GROUPS * S5_STATE
S5_TIME_CHUNK = 64

RET_HEADS = 4
RET_HEAD_DIM = 128
RET_CHUNK = 128

ATT_HEADS = 16
ATT_HEAD_DIM = 64
ATT_PAIRS = ATT_HEADS // 2
ATT_BLOCK = 128
ATT_SPAN = 128
ATT_DILATIONS = (1, 4, 16)
NEG_BIG = -1e30
LOG2E = math.log2(math.e)
ATT_Q_SCALE = ATT_HEAD_DIM ** -0.5 * LOG2E

D_FF = 2816
N_EXPERTS = 8
TOP_K = 2
MOE_BLOCK = 512

V7X_VMEM_BYTES = 64 * 1024 * 1024
VMEM_LIMIT = V7X_VMEM_BYTES - 8 * 1024 * 1024
MOE_VMEM_LIMIT = V7X_VMEM_BYTES - 4 * 1024 * 1024
ROW_TILE = 512


def _cparams(sem):
    return pltpu.CompilerParams(dimension_semantics=sem, vmem_limit_bytes=VMEM_LIMIT)


def _layernorm(y, g, b):
    mu = jnp.mean(y, -1, keepdims=True)
    yc = y - mu
    var = jnp.mean(yc * yc, -1, keepdims=True)
    return yc * lax.rsqrt(var + LN_EPS) * g + b


def _const_spec(shape):
    nd = len(shape)
    return pl.BlockSpec(shape, lambda *_: (0,) * nd)


def _inproj_body(x_ref, w_ref, *rest):
    n_cast = (len(rest) - 1) // 2
    h_ref = rest[n_cast]
    for src, dst in zip(rest[:n_cast], rest[n_cast + 1:]):
        dst[...] = src[...].astype(BF16)
    xb = x_ref[...].astype(BF16)
    h_ref[...] = jnp.dot(xb, w_ref[...], preferred_element_type=F32).astype(BF16)


def _inproj(x, w, cast=()):
    bsz, seq, dim = x.shape
    tm = ROW_TILE
    nh = w.shape[1]
    ns = seq // tm
    steps = bsz * ns
    cast_specs = []
    for a in cast:
        assert a.shape[0] % (16 * steps) == 0, a.shape
        cast_specs.append(pl.BlockSpec((a.shape[0] // steps, a.shape[1]), lambda b, s: (b * ns + s, 0)))
    outs = pl.pallas_call(
        _inproj_body,
        grid=(bsz, ns),
        in_specs=[pl.BlockSpec((None, tm, dim), lambda b, s: (b, s, 0)),
                  _const_spec(w.shape)] + cast_specs,
        out_specs=[pl.BlockSpec((None, tm, nh), lambda b, s: (b, s, 0))] + cast_specs,
        out_shape=[jax.ShapeDtypeStruct((bsz, seq, nh), BF16)]
        + [jax.ShapeDtypeStruct(a.shape, BF16) for a in cast],
        compiler_params=_cparams(("parallel", "parallel")),
        name="inproj",
    )(x, w, *cast)
    return outs[0], tuple(outs[1:])


def _s5_disc_body(ar_ref, ai_ref, ldt_ref, br_ref, bi_ref, lr_ref, li_ref, bbr_ref, bbi_ref):
    ar = ar_ref[...]
    ai = ai_ref[...]
    dt = jnp.exp(ldt_ref[...])
    mag = jnp.exp(ar * dt)
    lr = mag * jnp.cos(ai * dt)
    li = mag * jnp.sin(ai * dt)
    den = ar * ar + ai * ai
    zr = ((lr - 1.0) * ar + li * ai) / den
    zi = (li * ar - (lr - 1.0) * ai) / den
    lr_ref[...] = lr
    li_ref[...] = li
    br = br_ref[...]
    bi = bi_ref[...]
    bbr_ref[...] = zr[:, None, :] * br - zi[:, None, :] * bi
    bbi_ref[...] = zr[:, None, :] * bi + zi[:, None, :] * br


def _s5_discretize(a_re, a_im, log_dt, b_re, b_im, c_re, c_im, bsz):
    n_g, n_p = a_re.shape
    n_c = b_re.shape[-1]
    ldt = jnp.broadcast_to(log_dt.astype(F32)[:, None], (n_g, n_p))
    brt = jnp.swapaxes(b_re.astype(F32), 1, 2)
    bit = jnp.swapaxes(b_im.astype(F32), 1, 2)
    gp = jax.ShapeDtypeStruct((n_g, n_p), F32)
    gcp = jax.ShapeDtypeStruct((n_g, n_c, n_p), F32)
    lr, li, bbr, bbi = pl.pallas_call(
        _s5_disc_body, out_shape=[gp, gp, gcp, gcp], name="s5_discretize",
    )(a_re.astype(F32), a_im.astype(F32), ldt, brt, bit)
    eye = jnp.eye(S5_GROUPS, dtype=F32)
    nch = S5_CHANNELS
    wbr = jnp.einsum('gcp,gh->gchp', bbr, eye).reshape(nch, S5_NSTATE)
    wbi = jnp.einsum('gcp,gh->gchp', bbi, eye).reshape(nch, S5_NSTATE)
    wb = jnp.concatenate([wbr, wbi], axis=1).astype(BF16)
    wcr = jnp.einsum('gcp,gh->hpgc', c_re.astype(F32), eye).reshape(S5_NSTATE, nch)
    wci = jnp.einsum('gcp,gh->hpgc', c_im.astype(F32), eye).reshape(S5_NSTATE, nch)
    wc = jnp.concatenate([wcr, wci], axis=0).astype(BF16)
    lam = jnp.stack([jnp.broadcast_to(lr.reshape(1, S5_NSTATE), (bsz, S5_NSTATE)),
                     jnp.broadcast_to(li.reshape(1, S5_NSTATE), (bsz, S5_NSTATE))])
    return wb, wc, lam


def _gelu_tanh(x):
    return 0.5 * x * (1.0 + jnp.tanh(math.sqrt(2.0 / math.pi) * (x + 0.044715 * (x * x * x))))


def _s5_body(u_ref, wb_ref, wc_ref, lam_ref, d_ref, gw_ref, gb_ref, o_ref, st_ref, utb_ref, ys_ref, *xs_refs,
             lc, bsz):
    ncb = len(xs_refs)

    @pl.when(pl.program_id(0) == 0)
    def _():
        st_ref[...] = jnp.zeros_like(st_ref)

    nlb = S5_CHANNELS // 128
    for b in range(bsz):
        ub = u_ref[b].astype(F32)
        for c in range(nlb):
            utb_ref[c, pl.ds(b, lc, stride=bsz), :] = ub[:, c * 128:(c + 1) * 128]
    uf = jnp.concatenate([utb_ref[c] for c in range(nlb)], axis=-1)
    ub = uf.astype(BF16)
    half_c = S5_CHANNELS // 2
    half_s = S5_NSTATE // 2
    width = S5_NSTATE // ncb
    ys = [None, None]
    for cb in range(ncb):
        xs_ref = xs_refs[cb]
        re = slice(cb * width, (cb + 1) * width)
        im = slice(S5_NSTATE + cb * width, S5_NSTATE + (cb + 1) * width)
        hq = (cb * width) // half_s
        ch = slice(hq * half_c, (hq + 1) * half_c)
        xs_ref[:, :width] = jnp.dot(ub[:, ch], wb_ref[ch, re], preferred_element_type=F32)
        xs_ref[:, width:] = jnp.dot(ub[:, ch], wb_ref[ch, im], preferred_element_type=F32)
        lr = lam_ref[0, :, re]
        li = lam_ref[1, :, re]
        xr = st_ref[:, re]
        xi = st_ref[:, im]
        for t in range(lc):
            rows = slice(t * bsz, (t + 1) * bsz)
            nxr = lr * xr - li * xi + xs_ref[rows, :width]
            nxi = lr * xi + li * xr + xs_ref[rows, width:]
            xs_ref[rows, :width] = nxr
            xs_ref[rows, width:] = nxi
            xr, xi = nxr, nxi
        st_ref[:, re] = xr
        st_ref[:, im] = xi
        part = jnp.dot(xs_ref[:, :width].astype(BF16), wc_ref[re, ch], preferred_element_type=F32)
        part = part - jnp.dot(xs_ref[:, width:].astype(BF16), wc_ref[im, ch], preferred_element_type=F32)
        ys[hq] = part if ys[hq] is None else ys[hq] + part
    y = jnp.concatenate(ys, axis=-1)
    y = y + d_ref[...] * uf
    y = _gelu_tanh(y)
    z = jnp.dot(y.astype(BF16), gw_ref[...], preferred_element_type=F32) + gb_ref[...]
    yo = y * jax.nn.sigmoid(z)
    for c in range(nlb):
        ys_ref[c] = yo[:, c * 128:(c + 1) * 128]
    for b in range(bsz):
        o_ref[b] = jnp.concatenate([ys_ref[c, pl.ds(b, lc, stride=bsz), :] for c in range(nlb)],
                                   axis=-1).astype(BF16)


def _s5(h, wb, wc, lam, d_skip, glu_w, glu_b):
    bsz, seq, _ = h.shape
    lc = S5_TIME_CHUNK
    blk = lc * bsz
    ncb = 4
    body = functools.partial(_s5_body, lc=lc, bsz=bsz)
    return pl.pallas_call(
        body,
        grid=(seq // lc,),
        in_specs=[pl.BlockSpec((bsz, lc, S5_CHANNELS), lambda s: (0, s, 0)),
                  _const_spec(wb.shape), _const_spec(wc.shape), _const_spec(lam.shape),
                  _const_spec(d_skip.shape), _const_spec(glu_w.shape), _const_spec(glu_b.shape)],
        out_specs=pl.BlockSpec((bsz, lc, S5_CHANNELS), lambda s: (0, s, 0)),
        out_shape=jax.ShapeDtypeStruct((bsz, seq, S5_CHANNELS), BF16),
        scratch_shapes=[pltpu.VMEM((bsz, 2 * S5_NSTATE), F32),
                        pltpu.VMEM((S5_CHANNELS // 128, blk, 128), F32),
                        pltpu.VMEM((S5_CHANNELS // 128, blk, 128), F32)]
        + [pltpu.VMEM((blk, 2 * S5_NSTATE // ncb), F32) for _ in range(ncb)],
        compiler_params=_cparams(("arbitrary",)),
        name="s5_mixer",
    )(h, wb, wc, lam, d_skip, glu_w, glu_b)


def _ret_constants():
    hh = np.arange(RET_HEADS, dtype=np.float64)
    log_gamma = np.log1p(-np.exp2(-5.0 - hh))
    pos = np.arange(RET_CHUNK, dtype=np.float64)
    diff = pos[:, None] - pos[None, :]
    scale = RET_HEAD_DIM ** -0.5
    decay = np.where(diff >= 0, np.exp(log_gamma[:, None, None] * np.maximum(diff, 0.0)), 0.0) * scale
    zeta = np.exp(log_gamma[:, None] * (RET_CHUNK - 1.0 - pos)) * scale
    xi = np.exp(log_gamma[:, None] * (pos + 1.0))
    chunk_decay = np.exp(log_gamma * RET_CHUNK)
    zeta_b = np.broadcast_to(zeta[:, :, None], (RET_HEADS, RET_CHUNK, RET_HEAD_DIM))
    xi_b = np.broadcast_to(xi[:, :, None], (RET_HEADS, RET_CHUNK, RET_HEAD_DIM))
    return (decay.astype(np.float32), np.ascontiguousarray(zeta_b).astype(np.float32),
            np.ascontiguousarray(xi_b).astype(np.float32), [float(c) for c in chunk_decay])


def _ret_body(q_ref, k_ref, v_ref, g_ref, dec_ref, zeta_ref, xi_ref, o_ref, st_ref, *, ts, chunk_decay):
    @pl.when(pl.program_id(1) == 0)
    def _():
        st_ref[...] = jnp.zeros_like(st_ref)

    n_chunks = ts // RET_CHUNK
    for h in range(RET_HEADS):
        cols = slice(h * RET_HEAD_DIM, (h + 1) * RET_HEAD_DIM)
        states = [st_ref[h]]
        for c in range(n_chunks):
            rows = slice(c * RET_CHUNK, (c + 1) * RET_CHUNK)
            kz = (k_ref[rows, cols].astype(F32) * zeta_ref[h]).astype(BF16)
            kv = lax.dot_general(kz, v_ref[rows, cols], (((0,), (0,)), ((), ())), preferred_element_type=F32)
            states.append(states[-1] * chunk_decay[h] + kv)
        st_ref[h] = states[n_chunks]
        for c in range(n_chunks):
            rows = slice(c * RET_CHUNK, (c + 1) * RET_CHUNK)
            q = q_ref[rows, cols]
            s = lax.dot_general(q, k_ref[rows, cols], (((1,), (1,)), ((), ())),
                                preferred_element_type=F32) * dec_ref[h]
            inner = jnp.dot(s.astype(BF16), v_ref[rows, cols], preferred_element_type=F32)
            cross = jnp.dot(q, states[c].astype(BF16), preferred_element_type=F32) * xi_ref[h]
            ret = inner + cross
            mu = jnp.mean(ret, -1, keepdims=True)
            rc = ret - mu
            var = jnp.mean(rc * rc, -1, keepdims=True)
            rn = rc * lax.rsqrt(var + LN_EPS)
            gg = g_ref[rows, cols].astype(F32)
            o_ref[rows, cols] = (gg * jax.nn.sigmoid(gg) * rn).astype(BF16)


def _retention(h, first_block):
    bsz, seq, _ = h.shape
    ts = ROW_TILE
    width = RET_HEADS * RET_HEAD_DIM
    decay, zeta_b, xi_b, chunk_decay = _ret_constants()
    body = functools.partial(_ret_body, ts=ts, chunk_decay=chunk_decay)

    def col(j):
        return pl.BlockSpec((None, ts, width), lambda b, s, j=j: (b, s, j))

    return pl.pallas_call(
        body,
        grid=(bsz, seq // ts),
        in_specs=[col(first_block), col(first_block + 1), col(first_block + 2), col(first_block + 3),
                  _const_spec(decay.shape), _const_spec(zeta_b.shape), _const_spec(xi_b.shape)],
        out_specs=pl.BlockSpec((None, ts, width), lambda b, s: (b, s, 0)),
        out_shape=jax.ShapeDtypeStruct((bsz, seq, width), BF16),
        scratch_shapes=[pltpu.VMEM((RET_HEADS, RET_HEAD_DIM, RET_HEAD_DIM), F32)],
        compiler_params=_cparams(("parallel", "arbitrary")),
        name="retention",
    )(h, h, h, h, jnp.asarray(decay), jnp.asarray(zeta_b), jnp.asarray(xi_b))


FFN_BOUNDS = (0, 1536, 2816)


def _even_tail_body(x_ref, ya_ref, yb_ref, w1_ref, w2_ref, g1_ref, b1_ref,
                    wg_ref, wu_ref, wd_ref, g2_ref, b2_ref, *rest):
    n_cast = (len(rest) - 1) // 2
    o_ref = rest[n_cast]
    for src, dst in zip(rest[:n_cast], rest[n_cast + 1:]):
        dst[...] = src[...].astype(BF16)
    mix = jnp.dot(ya_ref[...], w1_ref[...], preferred_element_type=F32)
    mix = mix + jnp.dot(yb_ref[...], w2_ref[...], preferred_element_type=F32)
    x = _layernorm(ALPHA * x_ref[...] + mix, g1_ref[...], b1_ref[...])
    xb = x.astype(BF16)
    acc = None
    for c in range(len(FFN_BOUNDS) - 1):
        cs = slice(FFN_BOUNDS[c], FFN_BOUNDS[c + 1])
        gt = jnp.dot(xb, wg_ref[:, cs], preferred_element_type=F32)
        up = jnp.dot(xb, wu_ref[:, cs], preferred_element_type=F32)
        hh = (gt * jax.nn.sigmoid(gt) * up).astype(BF16)
        part = jnp.dot(hh, wd_ref[cs, :], preferred_element_type=F32)
        acc = part if acc is None else acc + part
    y = ALPHA * x + acc
    o_ref[...] = _layernorm(y, g2_ref[...], b2_ref[...])


def _even_tail(x2d, ya2d, yb2d, w1, w2, g1, b1, wg, wu, wd, g2, b2, cast=()):
    n_tok, dim = x2d.shape
    tm = ROW_TILE
    steps = n_tok // tm
    single = pl.Buffered(1)

    def wspec(shape):
        return pl.BlockSpec(shape, lambda i: (0, 0), pipeline_mode=single)

    def rows(width):
        return pl.BlockSpec((tm, width), lambda i: (i, 0))

    cast_specs = []
    for a in cast:
        assert a.shape[0] % (16 * steps) == 0, a.shape
        cast_specs.append(pl.BlockSpec((a.shape[0] // steps, a.shape[1]), lambda i: (i, 0)))
    outs = pl.pallas_call(
        _even_tail_body,
        grid=(steps,),
        in_specs=[rows(dim), rows(ya2d.shape[1]), rows(yb2d.shape[1]),
                  wspec(w1.shape), wspec(w2.shape), _const_spec(g1.shape), _const_spec(b1.shape),
                  wspec(wg.shape), wspec(wu.shape), wspec(wd.shape),
                  _const_spec(g2.shape), _const_spec(b2.shape)] + cast_specs,
        out_specs=[rows(dim)] + cast_specs,
        out_shape=[jax.ShapeDtypeStruct((n_tok, dim), F32)]
        + [jax.ShapeDtypeStruct(a.shape, BF16) for a in cast],
        compiler_params=_cparams(("parallel",)),
        name="outproj_ffn_ln",
    )(x2d, ya2d, yb2d, w1, w2, g1, b1, wg, wu, wd, g2, b2, *cast)
    return outs[0], tuple(outs[1:])


def _qkv_body(x_ref, w_ref, o_ref):
    xb = x_ref[...].astype(BF16)
    r = jnp.dot(xb, w_ref[...], preferred_element_type=F32)
    nblk = o_ref.shape[0]
    for j in range(nblk):
        blk = r[:, j * 128:(j + 1) * 128]
        if j < ATT_PAIRS:
            blk = blk * ATT_Q_SCALE
        o_ref[j] = blk.astype(BF16)


def _qkv(x, w):
    bsz, seq, dim = x.shape
    tm = ROW_TILE
    nblk = w.shape[1] // 128
    return pl.pallas_call(
        _qkv_body,
        grid=(bsz, seq // tm),
        in_specs=[pl.BlockSpec((None, tm, dim), lambda b, s: (b, s, 0)),
                  _const_spec(w.shape)],
        out_specs=pl.BlockSpec((None, nblk, tm, 128), lambda b, s: (b, 0, s, 0)),
        out_shape=jax.ShapeDtypeStruct((bsz, nblk, seq, 128), BF16),
        compiler_params=_cparams(("parallel", "parallel")),
        name="qkv_proj",
    )(x, w)


ATT_GROUP = 16


def _att_bias_tiles():
    slopes = jnp.exp2(-8.0 * jnp.arange(1, ATT_HEADS + 1, dtype=F32) / ATT_HEADS).reshape(ATT_PAIRS, 2)
    slope_rows = jnp.repeat(slopes, ATT_BLOCK, axis=1)[:, None, :, None]
    qi = jnp.arange(2 * ATT_BLOCK, dtype=jnp.int32)[:, None] % ATT_BLOCK
    ki = jnp.arange(2 * ATT_BLOCK, dtype=jnp.int32)[None, :]
    tiles = []
    for d in ATT_DILATIONS:
        for oi in range(2):
            steps = (qi - ki + oi * ATT_BLOCK).astype(F32)[None, None]
            allowed = (steps >= 0.0) & (steps <= float(ATT_SPAN))
            tiles.append(jnp.where(allowed, -(slope_rows * (float(d) * LOG2E)) * steps, NEG_BIG))
    return jnp.concatenate(tiles, axis=1)


def _stack_heads(q, low):
    zero = jnp.zeros_like(q)
    return jnp.concatenate([jnp.where(low, q, zero), jnp.where(low, zero, q)], axis=0)


def _att_unit(qs, kw, vw, bias, old, low, ones, last):
    s = lax.dot_general(qs, kw, (((1,), (1,)), ((), ())), preferred_element_type=F32) + bias
    sa = s[:, :128]
    sb = s[:, 128:]
    mrow = jnp.max(jnp.maximum(sa, sb), axis=-1, keepdims=True)
    m_new = jnp.broadcast_to(mrow, (2 * ATT_BLOCK, 128))
    if old is not None:
        m_old, acc_old, den_old = old
        m_new = jnp.maximum(m_old, m_new)
    p = jnp.concatenate([jnp.exp2(sa - m_new), jnp.exp2(sb - m_new)], axis=1).astype(BF16)
    vaug = jnp.concatenate([vw, ones], axis=1)
    res = jnp.dot(p, vaug, preferred_element_type=F32)
    acc = jnp.where(low, res[:ATT_BLOCK, :128], res[ATT_BLOCK:, :128])
    den = jnp.where(low, res[:ATT_BLOCK, 128:], res[ATT_BLOCK:, 128:])
    if old is not None:
        a = jnp.exp2(m_old - m_new)
        a_pair = jnp.where(low, a[:ATT_BLOCK], a[ATT_BLOCK:])
        acc = a_pair * acc_old + acc
        den = a_pair * den_old + den
    if last:
        acc = acc / den
    return m_new, acc, den


def _att_body(q_ref, k_ref, v_ref, bias_ref, o_ref,
              tmp_ref, q4_ref, k4_ref, v4_ref, qs4_ref, k4b_ref, v4b_ref, nat_ref, de4_ref, *, seq):
    quarter = seq // 4
    for src, dst in ((q_ref, q4_ref), (k_ref, k4_ref), (v_ref, v4_ref)):
        tmp_ref[...] = src[...].astype(F32)
        for r in range(4):
            dst[r] = tmp_ref[pl.ds(r, quarter, stride=4), :]
    lane = lax.broadcasted_iota(jnp.int32, (ATT_BLOCK, 128), 1)
    low = lane < ATT_HEAD_DIM
    ones = jnp.ones((2 * ATT_BLOCK, 128), BF16)
    low_all = lax.broadcasted_iota(jnp.int32, (quarter, 128), 1) < ATT_HEAD_DIM
    for r in range(4):
        q4 = q4_ref[r].astype(BF16)
        qs4_ref[0, r] = jnp.where(low_all, q4, jnp.zeros_like(q4))
        qs4_ref[1, r] = jnp.where(low_all, jnp.zeros_like(q4), q4)
        k4b_ref[r] = k4_ref[r].astype(BF16)
        v4b_ref[r] = v4_ref[r].astype(BF16)

    n_units = seq // ATT_BLOCK
    for bi, d in enumerate(ATT_DILATIONS):
        first = bi == 0
        last = bi == len(ATT_DILATIONS) - 1
        shift = d.bit_length() - 1

        def group(it, carry, bi=bi, d=d, shift=shift, first=first, last=last):
            loaded = []
            for g in range(ATT_GROUP):
                n = it * ATT_GROUP + g
                i = n >> shift
                r = n & (d - 1)
                iw = jnp.maximum(i - 1, 0)
                if d == 1:
                    qsl = pl.ds(pl.multiple_of(i * ATT_BLOCK, ATT_BLOCK), ATT_BLOCK)
                    wsl = pl.ds(pl.multiple_of(iw * ATT_BLOCK, ATT_BLOCK), 2 * ATT_BLOCK)
                    data = (_stack_heads(q_ref[qsl, :], low), k_ref[wsl, :], v_ref[wsl, :])
                    old_at = None
                    new_at = lambda s, qsl=qsl: nat_ref.at[s, qsl, :]
                elif d == 4:
                    qsl = pl.ds(pl.multiple_of(i * ATT_BLOCK, ATT_BLOCK), ATT_BLOCK)
                    wsl = pl.ds(pl.multiple_of(iw * ATT_BLOCK, ATT_BLOCK), 2 * ATT_BLOCK)
                    osl = pl.ds(r + (4 * ATT_BLOCK) * i, ATT_BLOCK, stride=4)
                    data = (jnp.concatenate([qs4_ref[0, r, qsl, :], qs4_ref[1, r, qsl, :]], axis=0),
                            k4b_ref[r, wsl, :], v4b_ref[r, wsl, :])
                    old_at = lambda s, osl=osl: nat_ref.at[s, osl, :]
                    new_at = lambda s, r=r, qsl=qsl: de4_ref.at[s, r, qsl, :]
                else:
                    r4 = r & 3
                    qsl = pl.ds((r >> 2) + (4 * ATT_BLOCK) * i, ATT_BLOCK, stride=4)
                    wsl = pl.ds((r >> 2) + (4 * ATT_BLOCK) * iw, 2 * ATT_BLOCK, stride=4)
                    data = (_stack_heads(q4_ref[r4, qsl, :].astype(BF16), low),
                            k4_ref[r4, wsl, :].astype(BF16), v4_ref[r4, wsl, :].astype(BF16))
                    old_at = lambda s, r4=r4, qsl=qsl: de4_ref.at[s, r4, qsl, :]
                    new_at = old_at
                old = None
                if old_at is not None:
                    old = (jnp.concatenate([old_at(1)[...], old_at(2)[...]], axis=0),
                           old_at(0)[...], old_at(3)[...])
                loaded.append((new_at, data, bias_ref[2 * bi + jnp.minimum(i, 1)], old))
            results = [_att_unit(qs, kw, vw, bias, old, low, ones, last)
                       for (_, (qs, kw, vw), bias, old) in loaded]
            for (new_at, *_), (m_new, acc, den) in zip(loaded, results):
                new_at(0)[...] = acc
                if not last:
                    new_at(1)[...] = m_new[:ATT_BLOCK]
                    new_at(2)[...] = m_new[ATT_BLOCK:]
                    new_at(3)[...] = den
            return carry

        lax.fori_loop(0, n_units // ATT_GROUP, group, 0)

    for r in range(4):
        nat_ref[0, pl.ds(r, quarter, stride=4), :] = de4_ref[0, r]
    o_ref[...] = nat_ref[0].astype(BF16)


def _dilated_attention(qkv):
    bsz, nblk, seq, _ = qkv.shape
    bias = _att_bias_tiles()
    specs = [pl.BlockSpec((None, None, seq, 128), lambda hp, b, j=j: (b, j * ATT_PAIRS + hp, 0, 0))
             for j in range(3)]
    specs.append(pl.BlockSpec((None,) + bias.shape[1:], lambda hp, b: (hp, 0, 0, 0)))
    body = functools.partial(_att_body, seq=seq)
    return pl.pallas_call(
        body,
        grid=(ATT_PAIRS, bsz),
        in_specs=specs,
        out_specs=pl.BlockSpec((None, None, seq, 128), lambda hp, b: (b, hp, 0, 0)),
        out_shape=jax.ShapeDtypeStruct((bsz, ATT_PAIRS, seq, 128), BF16),
        scratch_shapes=[pltpu.VMEM((seq, 128), F32)]
        + [pltpu.VMEM((4, seq // 4, 128), F32) for _ in range(3)]
        + [pltpu.VMEM((2, 4, seq // 4, 128), BF16),
           pltpu.VMEM((4, seq // 4, 128), BF16), pltpu.VMEM((4, seq // 4, 128), BF16),
           pltpu.VMEM((4, seq, 128), F32), pltpu.VMEM((4, 4, seq // 4, 128), F32)],
        compiler_params=_cparams(("parallel", "parallel")),
        name="dilated_attention",
    )(qkv, qkv, qkv, bias)


def _outln_odd_body(x_ref, a_ref, w_ref, g_ref, b_ref, rw_ref, rb_ref, o_ref, lg_ref):
    a = jnp.concatenate([a_ref[j] for j in range(a_ref.shape[0])], axis=-1)
    acc = jnp.dot(a, w_ref[...], preferred_element_type=F32)
    y = ALPHA * x_ref[...] + acc
    o = _layernorm(y, g_ref[...], b_ref[...])
    o_ref[...] = o
    nl = lg_ref.shape[-1]
    oh = o.astype(BF16)
    ol = (o - oh.astype(F32)).astype(BF16)
    both = jnp.dot(oh, rw_ref[...], preferred_element_type=F32)
    lg = both[:, :nl] + both[:, nl:] + jnp.dot(ol, rw_ref[:, :nl], preferred_element_type=F32)
    lg_ref[...] = lg + rb_ref[...]


def _outln_odd(x, att, w, g, b, rw, rb):
    bsz, seq, dim = x.shape
    tm = ROW_TILE
    npair = att.shape[1]
    rwh = rw.astype(BF16)
    rwl = (rw - rwh.astype(F32)).astype(BF16)
    rw2 = jnp.concatenate([rwh, rwl], axis=1)
    return pl.pallas_call(
        _outln_odd_body,
        grid=(bsz, seq // tm),
        in_specs=[pl.BlockSpec((None, tm, dim), lambda b_, s: (b_, s, 0)),
                  pl.BlockSpec((None, npair, tm, 128), lambda b_, s: (b_, 0, s, 0)),
                  _const_spec(w.shape), _const_spec(g.shape), _const_spec(b.shape),
                  _const_spec(rw2.shape), _const_spec(rb.shape)],
        out_specs=[pl.BlockSpec((None, tm, dim), lambda b_, s: (b_, s, 0)),
                   pl.BlockSpec((None, tm, 128), lambda b_, s: (b_, s, 0))],
        out_shape=[jax.ShapeDtypeStruct((bsz, seq, dim), F32),
                   jax.ShapeDtypeStruct((bsz, seq, 128), F32)],
        compiler_params=_cparams(("parallel", "parallel")),
        name="outproj_ln_odd",
    )(x, att, w, g, b, rw2, rb)


MOE_FF_BOUNDS = (0, 768, 1536, 2304, 2816)
MOE_NCHUNK = len(MOE_FF_BOUNDS) - 1
MOE_COPY_SHARE = (0.25, 0.25, 0.25, 0.25)
MOE_TRASH_BLOCKS = 2 + 2 * N_EXPERTS
MOE_NBUF = 3


def _route(logits, n_tok):
    top_val, top_idx = lax.top_k(logits, TOP_K)
    gates = jax.nn.softmax(top_val, axis=-1)
    n_assign = n_tok * TOP_K
    exp_flat = top_idx.reshape(n_assign).astype(jnp.int32)
    experts = jnp.arange(N_EXPERTS, dtype=jnp.int32)

    def lookup(table, idx):
        return jnp.sum(jnp.where(idx[:, None] == experts[None, :], table[None, :], 0), axis=1)

    counts = jnp.sum((exp_flat[:, None] == experts[None, :]).astype(jnp.int32), axis=0)
    starts = jnp.cumsum(counts) - counts
    padded_counts = (counts + MOE_BLOCK - 1) // MOE_BLOCK * MOE_BLOCK
    pends = jnp.cumsum(padded_counts)
    pstarts = pends - padded_counts
    order = jnp.argsort(exp_flat).astype(jnp.int32)
    n_blocks = -(-n_assign // MOE_BLOCK) + N_EXPERTS
    n_rows = n_blocks * MOE_BLOCK
    blk_exp = jnp.minimum(jnp.sum((pends[None, :] <= (jnp.arange(n_blocks, dtype=jnp.int32) * MOE_BLOCK)[:, None])
                                  .astype(jnp.int32), axis=1), N_EXPERTS - 1)
    rows = jnp.arange(n_rows, dtype=jnp.int32)
    row_exp = jnp.repeat(blk_exp, MOE_BLOCK)
    local = rows - lookup(pstarts, row_exp)
    valid = local < lookup(counts, row_exp)
    spos = jnp.clip(lookup(starts, row_exp) + local, 0, n_assign - 1)
    n_used = (pends[-1] // MOE_BLOCK).astype(jnp.int32)
    assign = jnp.where(valid, order[spos], -1)
    real_dst = (assign % TOP_K) * n_tok + assign // TOP_K
    in_blk = rows % MOE_BLOCK
    blk = rows // MOE_BLOCK
    trash_blk = jnp.where(blk < n_used, 1 + row_exp, 1 + N_EXPERTS + blk - n_used)
    dst_mid = jnp.where(valid, real_dst, n_assign + trash_blk * MOE_BLOCK + in_blk)
    dst_front = n_assign + jnp.arange(MOE_BLOCK, dtype=jnp.int32)
    back = jnp.zeros(((MOE_NBUF - 1) * MOE_BLOCK,), jnp.int32)
    dst_row = jnp.concatenate([dst_front, dst_mid, back]).reshape(n_blocks + MOE_NBUF, MOE_BLOCK)
    src_row = jnp.concatenate([back[:MOE_BLOCK], jnp.where(valid, assign // TOP_K, 0), back])
    src_row = src_row.reshape(n_blocks + MOE_NBUF, MOE_BLOCK)
    return gates, src_row, dst_row, blk_exp, n_used, n_blocks


def _moe_body(bexp_ref, nused_ref, src_ref, dst_ref, x_hbm, wg_ref, wu_ref, wd_ref, o_hbm,
              xf_ref, xb_ref, yb_ref, gsem, ssem, *, n_blocks, n_real):
    j = pl.program_id(0)
    slot = lax.rem(j, MOE_NBUF)
    slot_m1 = lax.rem(j + MOE_NBUF - 1, MOE_NBUF)
    slot_p1 = lax.rem(j + 1, MOE_NBUF)
    active = j < nused_ref[0]

    def gather_start(blk, n, s):
        tok = src_ref[blk + 1, n]
        pltpu.make_async_copy(x_hbm.at[pl.ds(tok, 1), :], xf_ref.at[s, pl.ds(n, 1), :], gsem.at[s]).start()

    def scatter_start(blk, n, s):
        dst = dst_ref[blk + 1, n]
        pltpu.make_async_copy(yb_ref.at[s, pl.ds(n, 1), :], o_hbm.at[pl.ds(dst, 1), :], ssem.at[s]).start()

    def gather_wait(s):
        pltpu.make_async_copy(x_hbm.at[pl.ds(0, MOE_BLOCK), :], xf_ref.at[s], gsem.at[s]).wait()

    def scatter_wait(s):
        pltpu.make_async_copy(yb_ref.at[s], o_hbm.at[pl.ds(0, MOE_BLOCK), :], ssem.at[s]).wait()

    @pl.when(j == 0)
    def _():
        yb_ref[...] = jnp.zeros_like(yb_ref)
        fills = [pltpu.make_async_copy(
            yb_ref.at[0], o_hbm.at[pl.ds(n_real + t * MOE_BLOCK, MOE_BLOCK), :], ssem.at[0])
            for t in range(MOE_TRASH_BLOCKS)]
        for fill in fills:
            fill.start()
        for fill in fills:
            fill.wait()

        def issue(n, carry):
            gather_start(0, n, 0)
            gather_start(1, n, 1)
            return carry

        lax.fori_loop(0, MOE_BLOCK, issue, 0)

    gather_wait(slot)

    @pl.when(j >= MOE_NBUF - 1)
    def _():
        scatter_wait(slot)

    @pl.when(active)
    def _():
        xb_ref[...] = xf_ref[slot].astype(BF16)

    edges = [round(MOE_BLOCK * sum(MOE_COPY_SHARE[:c])) for c in range(MOE_NCHUNK)] + [MOE_BLOCK]
    for c in range(MOE_NCHUNK):
        @pl.when(j + c < nused_ref[0] + c)
        def _(c=c):
            for n in range(edges[c], edges[c + 1]):
                gather_start(j + 2, n, slot_m1)
                scatter_start(j - 1, n, slot_m1)
            cs = slice(MOE_FF_BOUNDS[c], MOE_FF_BOUNDS[c + 1])
            xb = xb_ref[...]
            gt = jnp.dot(xb, wg_ref[:, cs], preferred_element_type=F32)
            up = jnp.dot(xb, wu_ref[:, cs], preferred_element_type=F32)
            hh = (gt * jax.nn.sigmoid(gt) * up).astype(BF16)
            part = jnp.dot(hh, wd_ref[cs, :], preferred_element_type=F32)
            if c == 0:
                yb_ref[slot] = part
            else:
                yb_ref[slot] += part

    @pl.when(jnp.logical_not(active))
    def _():
        def issue(n, carry):
            gather_start(j + 2, n, slot_m1)
            scatter_start(j - 1, n, slot_m1)
            return carry

        lax.fori_loop(0, MOE_BLOCK, issue, 0)

    @pl.when(j == n_blocks - 1)
    def _():
        def issue(n, carry):
            scatter_start(j, n, slot)
            return carry

        lax.fori_loop(0, MOE_BLOCK, issue, 0)
        scatter_wait(slot_p1)
        scatter_wait(slot_m1)
        scatter_wait(slot)
        gather_wait(slot_p1)
        gather_wait(slot_m1)


def _moe(x2d, wg, wu, wd, src_row, dst_row, blk_exp, n_used, n_blocks):
    n_tok, dim = x2d.shape
    last = jnp.maximum(n_used - 1, 0)
    jj = jnp.arange(n_blocks, dtype=jnp.int32)
    bexp_eff = blk_exp[jnp.minimum(jj, last)]
    n_real = TOP_K * n_tok
    out_rows = n_real + MOE_TRASH_BLOCKS * MOE_BLOCK

    def wmap(j, bexp, nused, src, dst):
        return (bexp[j], 0, 0)

    grid_spec = pltpu.PrefetchScalarGridSpec(
        num_scalar_prefetch=4,
        grid=(n_blocks,),
        in_specs=[pl.BlockSpec(memory_space=pl.ANY),
                  pl.BlockSpec((None, dim, D_FF), wmap),
                  pl.BlockSpec((None, dim, D_FF), wmap),
                  pl.BlockSpec((None, D_FF, dim), wmap)],
        out_specs=pl.BlockSpec(memory_space=pl.ANY),
        scratch_shapes=[pltpu.VMEM((MOE_NBUF, MOE_BLOCK, dim), F32),
                        pltpu.VMEM((MOE_BLOCK, dim), BF16),
                        pltpu.VMEM((MOE_NBUF, MOE_BLOCK, dim), F32),
                        pltpu.SemaphoreType.DMA((MOE_NBUF,)),
                        pltpu.SemaphoreType.DMA((MOE_NBUF,))],
    )
    body = functools.partial(_moe_body, n_blocks=n_blocks, n_real=n_real)
    return pl.pallas_call(
        body,
        grid_spec=grid_spec,
        out_shape=jax.ShapeDtypeStruct((out_rows, dim), F32),
        compiler_params=pltpu.CompilerParams(dimension_semantics=("arbitrary",),
                                             vmem_limit_bytes=MOE_VMEM_LIMIT),
        name="moe_swiglu",
    )(bexp_eff, n_used.reshape(1), src_row, dst_row, x2d, wg, wu, wd)


def _combine_body(x_ref, gp_ref, y0_ref, y1_ref, g_ref, b_ref, o_ref):
    gp = gp_ref[...]
    moe = gp[:, 0:1] * y0_ref[...] + gp[:, 1:2] * y1_ref[...]
    y = ALPHA * x_ref[...] + moe
    o_ref[...] = _layernorm(y, g_ref[...], b_ref[...])


def _combine(x2d, gates_pad, yrows, g, b):
    n_tok, dim = x2d.shape
    tm = ROW_TILE
    nt = n_tok // tm
    return pl.pallas_call(
        _combine_body,
        grid=(nt,),
        in_specs=[pl.BlockSpec((tm, dim), lambda i: (i, 0)),
                  pl.BlockSpec((tm, 128), lambda i: (i, 0)),
                  pl.BlockSpec((tm, dim), lambda i: (i, 0)),
                  pl.BlockSpec((tm, dim), lambda i: (nt + i, 0)),
                  _const_spec(g.shape), _const_spec(b.shape)],
        out_specs=pl.BlockSpec((tm, dim), lambda i: (i, 0)),
        out_shape=jax.ShapeDtypeStruct((n_tok, dim), F32),
        compiler_params=_cparams(("parallel",)),
        name="moe_combine_ln",
    )(x2d, gates_pad, yrows, yrows, g, b)


def _row(v):
    return v.astype(F32).reshape(1, -1)


def _even_layer(x, w_in, a_re, a_im, log_dt, b_re, b_im, c_re, c_im, d_skip, glu_w, glu_b,
                w_out, ln1_g, ln1_b, w_gate, w_up, w_down, ln2_g, ln2_b, cast_f32=()):
    bsz, seq, dim = x.shape
    h, (w_gate, w_up, w_out) = _inproj(x, w_in.astype(BF16), cast=[w_gate, w_up, w_out])
    wb, wc, lam = _s5_discretize(a_re, a_im, log_dt, b_re, b_im, c_re, c_im, bsz)
    ya = _s5(h, wb, wc, lam, _row(d_skip), glu_w.astype(BF16), _row(glu_b))
    yb = _retention(h, 1)
    n_tok = bsz * seq
    flat = [w.reshape(-1, w.shape[-1]) for w in cast_f32]
    x2, casted = _even_tail(x.reshape(n_tok, dim), ya.reshape(n_tok, -1), yb.reshape(n_tok, -1),
                            w_out[:S5_CHANNELS], w_out[S5_CHANNELS:], _row(ln1_g), _row(ln1_b),
                            w_gate, w_up, w_down.astype(BF16),
                            _row(ln2_g), _row(ln2_b), cast=flat)
    casted = tuple(c.reshape(w.shape) for c, w in zip(casted, cast_f32))
    return x2.reshape(bsz, seq, dim), casted


def _odd_layer(x, w_qkv, w_out, ln1_g, ln1_b, router_w, router_b, w_gate, w_up, w_down, ln2_g, ln2_b):
    bsz, seq, dim = x.shape
    n_tok = bsz * seq
    qkv = _qkv(x, w_qkv.astype(BF16))
    att = _dilated_attention(qkv)
    rw = jnp.zeros((dim, 128), F32).at[:, :N_EXPERTS].set(router_w.astype(F32))
    rb = jnp.zeros((1, 128), F32).at[:, :N_EXPERTS].set(router_b.astype(F32)[None, :])
    x1, logits = _outln_odd(x, att, w_out.astype(BF16), _row(ln1_g), _row(ln1_b), rw, rb)
    x1 = x1.reshape(n_tok, dim)
    gates, src_row, dst_row, blk_exp, n_used, n_blocks = _route(logits.reshape(n_tok, 128)[:, :N_EXPERTS], n_tok)
    yrows = _moe(x1, w_gate.astype(BF16), w_up.astype(BF16), w_down.astype(BF16),
                 src_row, dst_row, blk_exp, n_used, n_blocks)
    gates_pad = jnp.zeros((n_tok, 128), F32).at[:, :TOP_K].set(gates)
    out = _combine(x1, gates_pad, yrows, _row(ln2_g), _row(ln2_b))
    return out.reshape(bsz, seq, dim)


def kernel(x, ev_w_in, ev_s5_a_re, ev_s5_a_im, ev_s5_log_dt, ev_s5_b_re, ev_s5_b_im, ev_s5_c_re, ev_s5_c_im, ev_s5_d, ev_s5_glu_w, ev_s5_glu_b, ev_w_out, ev_ln1_g, ev_ln1_b, ev_ffn_w_gate, ev_ffn_w_up, ev_ffn_w_down, ev_ln2_g, ev_ln2_b, od_w_qkv, od_w_out, od_ln1_g, od_ln1_b, od_router_w, od_router_b, od_moe_w_gate, od_moe_w_up, od_moe_w_down, od_ln2_g, od_ln2_b):
    moe_bf16 = None
    for layer in range(DEPTH):
        i = layer // 2
        if layer % 2 == 0:
            nxt = (od_moe_w_gate[i], od_moe_w_up[i], od_moe_w_down[i]) if layer + 1 < DEPTH else ()
            x, casted = _even_layer(x, ev_w_in[i], ev_s5_a_re[i], ev_s5_a_im[i], ev_s5_log_dt[i], ev_s5_b_re[i],
                                    ev_s5_b_im[i], ev_s5_c_re[i], ev_s5_c_im[i], ev_s5_d[i], ev_s5_glu_w[i],
                                    ev_s5_glu_b[i], ev_w_out[i], ev_ln1_g[i], ev_ln1_b[i], ev_ffn_w_gate[i],
                                    ev_ffn_w_up[i], ev_ffn_w_down[i], ev_ln2_g[i], ev_ln2_b[i], cast_f32=nxt)
            moe_bf16 = casted if nxt else None
        else:
            wg, wu, wd = moe_bf16 or (od_moe_w_gate[i], od_moe_w_up[i], od_moe_w_down[i])
            x = _odd_layer(x, od_w_qkv[i], od_w_out[i], od_ln1_g[i], od_ln1_b[i], od_router_w[i],
                           od_router_b[i], wg, wu, wd, od_ln2_g[i], od_ln2_b[i])
            moe_bf16 = None
    return x
```
